```python
import math
import jax, jax.numpy as jnp
from jax import lax
import numpy as np

D_MODEL = 1024
BATCH = 2
SEQ = 8192
DEPTH = 2
DEC_BATCH = 32
DEC_SEQ = 4
PAST_LEN = 16384
PAGE_SIZE = 128

N_MIXERS = 4
MIX_W = D_MODEL // N_MIXERS
N_HEADS_MIX = 4
HEAD_DIM = MIX_W // N_HEADS_MIX
DILATIONS = ((128, 1), (512, 4), (2048, 16))
MAX_WINDOW = max(w for w, _ in DILATIONS)
ATTN_BLOCK = 128
N_BUCKETS = 32
MAX_DISTANCE = MAX_WINDOW
MLSTM_CHUNK = 128
HGRN_CHUNK = 16
MLP_CHUNK = 128
D_FF = 4 * D_MODEL
EPS = 1e-6
NEG = -1e30
LB_FLOOR = 1e-30

SPLIT_SIZES = (MIX_W, MIX_W, MIX_W,
               MIX_W, MIX_W, MIX_W, MIX_W, N_HEADS_MIX, N_HEADS_MIX,
               MIX_W, MIX_W,
               MIX_W, MIX_W, MIX_W, MIX_W)
D_IN = sum(SPLIT_SIZES)
SPLIT_POINTS = tuple(int(c) for c in np.cumsum(SPLIT_SIZES)[:-1])

kernel_name = 'hybrid_dilated_mlstm_gmlp_hgrn2_step'


def rmsnorm(x, g):
    xf = x.astype(jnp.float32)
    y = xf * lax.rsqrt(jnp.mean(xf * xf, axis=-1, keepdims=True) + EPS)
    return (y * g.astype(jnp.float32)).astype(x.dtype)


def headnorm(h, g):
    return rmsnorm(h, g.reshape(N_HEADS_MIX, HEAD_DIM))


def rel_bucket(dist):
    max_exact = N_BUCKETS // 2
    d = jnp.maximum(dist, 1).astype(jnp.float32)
    large = max_exact + (jnp.log(d / max_exact) / math.log(MAX_DISTANCE / max_exact)
                         * (N_BUCKETS - max_exact)).astype(jnp.int32)
    large = jnp.clip(large, max_exact, N_BUCKETS - 1)
    return jnp.where(dist < max_exact, dist, large)


def _to_chunks(a, L):
    B, T, H = a.shape[:3]
    a = a.astype(jnp.float32).reshape((B, T // L, L, H) + a.shape[3:])
    return jnp.moveaxis(a, (1, 3), (0, 2))


def _from_chunks(a):
    nc, B, H, L = a.shape[:4]
    a = jnp.moveaxis(a, (0, 2), (1, 3))
    return a.reshape((B, nc * L, H) + a.shape[4:])


def dilated_window_attention(q, k, v, k_past, v_past, rel_bias):
    B, T, H, dh = q.shape
    P = k_past.shape[1]
    pad = MAX_WINDOW - P
    f32 = jnp.float32
    zeros = jnp.zeros((B, pad, H, dh), f32)
    kp = jnp.concatenate([zeros, k_past.astype(f32), k.astype(f32)], axis=1)
    vp = jnp.concatenate([zeros, v_past.astype(f32), v.astype(f32)], axis=1)
    qf = q.astype(f32) * (HEAD_DIM ** -0.5)
    qb = ATTN_BLOCK if T % ATTN_BLOCK == 0 else T
    nb = T // qb
    patterns = []
    for w, d in DILATIONS:
        offs = jnp.arange(w // d + 1, dtype=jnp.int32) * d
        bias = rel_bias[rel_bucket(offs)].T.astype(f32)
        patterns.append((offs, bias))

    def block(s):
        qblk = lax.dynamic_slice_in_dim(qf, s, qb, axis=1)
        qpos = MAX_WINDOW + s + jnp.arange(qb, dtype=jnp.int32)
        lses, outs = [], []
        for offs, bias in patterns:
            idx = qpos[:, None] - offs[None, :]
            kg = jnp.take(kp, idx, axis=1, mode='clip')
            vg = jnp.take(vp, idx, axis=1, mode='clip')
            logits = jnp.einsum('bqhd,bqjhd->bhqj', qblk, kg) + bias[None, :, None, :]
            logits = jnp.where((idx >= pad)[None, None], logits, NEG)
            m = jnp.max(logits, axis=-1, keepdims=True)
            p = jnp.exp(logits - m)
            den = jnp.sum(p, axis=-1)
            o = jnp.einsum('bhqj,bqjhd->bqhd', p, vg) / jnp.transpose(den, (0, 2, 1))[..., None]
            lses.append(jnp.transpose(m[..., 0] + jnp.log(den), (0, 2, 1)))
            outs.append(o)
        wts = jax.nn.softmax(jnp.stack(lses, 0), axis=0)
        return jnp.einsum('pbqh,pbqhd->bqhd', wts, jnp.stack(outs, 0))

    out = lax.map(block, jnp.arange(nb, dtype=jnp.int32) * qb)
    return jnp.moveaxis(out, 0, 1).reshape(B, T, H, dh)


def mlstm_chunkwise(q, k, v, i_pre, log_f, C0, n0, m0):
    B, T, H, dh = q.shape
    L = MLSTM_CHUNK if T % MLSTM_CHUNK == 0 else T
    qc, kc, vc = _to_chunks(q, L), _to_chunks(k, L) * (dh ** -0.5), _to_chunks(v, L)
    ic, fc = _to_chunks(i_pre, L), _to_chunks(log_f, L)
    causal = jnp.tril(jnp.ones((L, L), bool))

    def step(carry, xs):
        C, n, m = carry
        qt, kt, vt, it, ft = xs
        b = jnp.cumsum(ft, axis=-1)
        D = jnp.where(causal, b[..., :, None] - b[..., None, :] + it[..., None, :], NEG)
        g = b + m[..., None]
        m_t = jnp.maximum(g, jnp.max(D, axis=-1))
        Dexp = jnp.exp(D - m_t[..., None])
        gexp = jnp.exp(g - m_t)
        S = jnp.einsum('bhtd,bhsd->bhts', qt, kt) * Dexp
        num = jnp.einsum('bhts,bhsd->bhtd', S, vt) + gexp[..., None] * jnp.einsum('bhvk,bhtk->bhtv', C, qt)
        nq = jnp.sum(S, axis=-1) + gexp * jnp.einsum('bhk,bhtk->bht', n, qt)
        h = num / jnp.maximum(jnp.abs(nq), jnp.exp(-m_t))[..., None]
        m_new = m_t[..., -1]
        wk = jnp.exp(b[..., -1:] - b + it - m_new[..., None])
        dc = jnp.exp(b[..., -1] + m - m_new)
        C_new = dc[..., None, None] * C + jnp.einsum('bhs,bhsv,bhsk->bhvk', wk, vt, kt)
        n_new = dc[..., None] * n + jnp.einsum('bhs,bhsk->bhk', wk, kt)
        return (C_new, n_new, m_new), h

    f32 = jnp.float32
    (C, n, m), h = lax.scan(step, (C0.astype(f32), n0.astype(f32), m0.astype(f32)), (qc, kc, vc, ic, fc))
    return _from_chunks(h), C, n, m


def hgrn2_chunkwise(q, log_f, k, v, S0):
    B, T, H, dk = q.shape
    L = HGRN_CHUNK if T % HGRN_CHUNK == 0 else T
    qc, fc, kc, vc = _to_chunks(q, L), _to_chunks(log_f, L), _to_chunks(k, L), _to_chunks(v, L)
    causal = jnp.tril(jnp.ones((L, L), bool))

    def step(S, xs):
        qt, ft, kt, vt = xs
        b = jnp.cumsum(ft, axis=2)
        diff = jnp.where(causal[..., None], b[:, :, :, None, :] - b[:, :, None, :, :], NEG)
        A = jnp.einsum('bhtd,bhsd,bhtsd->bhts', qt, kt, jnp.exp(diff))
        o = jnp.einsum('bhts,bhsv->bhtv', A, vt) + jnp.einsum('bhtd,bhdv->bhtv', qt * jnp.exp(b), S)
        bL = b[:, :, -1:, :]
        S_new = jnp.exp(bL[:, :, 0, :])[..., None] * S + jnp.einsum('bhsd,bhsv->bhdv', kt * jnp.exp(bL - b), vt)
        return S_new, o

    S, o = lax.scan(step, S0.astype(jnp.float32), (qc, fc, kc, vc))
    return _from_chunks(o), S


def chunk_spatial_gate(u, v, w_s, b_s):
    B, T, H, c = v.shape
    nc = -(-T // MLP_CHUNK)
    Tp = nc * MLP_CHUNK
    vp = jnp.pad(v, ((0, 0), (0, Tp - T), (0, 0), (0, 0))).reshape(B, nc, MLP_CHUNK, H, c)
    w = jnp.where(jnp.tril(jnp.ones((MLP_CHUNK, MLP_CHUNK), bool)), w_s, 0)
    s = jnp.einsum('hts,bnshc->bnthc', w, vp) + b_s.T[None, None, :, :, None]
    return u * s.reshape(B, Tp, H, c)[:, :T]


def layer(x, k_past, v_past, C0, n0, m0, S0, keep,
          w_in, w_out, g_attn, g_mlp, w_up, w_down, b_i, b_f, g_mlstm, g_cv, w_s, b_s, lb, g_hgrn, rel_bias):
    B, T, _ = x.shape
    H, dh = N_HEADS_MIX, HEAD_DIM
    f32 = jnp.float32
    z = rmsnorm(x, g_attn) @ w_in
    aq, ak, av, bq, bk, bv, bo, bi, bf, cu, cv, dq, df, di, dg = jnp.split(z, SPLIT_POINTS, axis=-1)
    heads = lambda a: a.reshape(B, T, H, dh)
    ka, va = heads(ak), heads(av)
    out_a = dilated_window_attention(heads(aq), ka, va, k_past, v_past, rel_bias).reshape(B, T, MIX_W)
    k_keep = jnp.concatenate([k_past.astype(ka.dtype), ka], axis=1)[:, -keep:]
    v_keep = jnp.concatenate([v_past.astype(va.dtype), va], axis=1)[:, -keep:]
    i_pre = bi.astype(f32) + b_i.astype(f32)
    log_fb = jax.nn.log_sigmoid(bf.astype(f32) + b_f.astype(f32))
    hb, C1, n1, m1 = mlstm_chunkwise(heads(bq), heads(bk), heads(bv), i_pre, log_fb, C0, n0, m0)
    out_b = jax.nn.sigmoid(bo.astype(f32)) * headnorm(hb, g_mlstm).reshape(B, T, MIX_W)
    v_rows = heads(rmsnorm(jax.nn.gelu(cv), g_cv))
    out_c = chunk_spatial_gate(heads(jax.nn.gelu(cu)), v_rows, w_s, b_s).reshape(B, T, MIX_W)
    lbf = lb.astype(f32)
    dff = df.astype(f32)
    log_fd = jnp.logaddexp(jnp.log(jnp.maximum(lbf, LB_FLOOR)), jnp.log1p(-lbf) + jax.nn.log_sigmoid(dff))
    kd = (1.0 - lbf) * jax.nn.sigmoid(-dff)
    hd, S1 = hgrn2_chunkwise(heads(dq), heads(log_fd), heads(kd), heads(di), S0)
    out_d = headnorm(hd, g_hgrn).reshape(B, T, MIX_W) * jax.nn.silu(dg.astype(f32))
    mix = jnp.concatenate([out_a.astype(x.dtype), out_b.astype(x.dtype),
                           out_c.astype(x.dtype), out_d.astype(x.dtype)], axis=-1)
    x = x + mix @ w_out
    hm = rmsnorm(x, g_mlp) @ w_up
    x = x + jnp.square(jax.nn.relu(hm)) @ w_down
    return x, k_keep, v_keep, C1, n1, m1, S1, v_rows


def setup_inputs(seed: int = 0) -> dict:
    key = jax.random.key(seed)
    ks = jax.random.split(key, 24)
    H, dh = N_HEADS_MIX, HEAD_DIM
    wb = min(MAX_WINDOW, PAST_LEN)
    nrm = lambda k, shape, s: jax.random.normal(k, shape, jnp.float32) * s
    b_f = jnp.broadcast_to(jnp.linspace(3.0, 6.0, H, dtype=jnp.float32), (DEPTH, H)) + nrm(ks[14], (DEPTH, H), 0.1)
    return {
        'x_prompt': nrm(ks[0], (BATCH, SEQ, D_MODEL), 1.0),
        'x_sample': nrm(ks[1], (DEC_BATCH, DEC_SEQ, D_MODEL), 1.0),
        'cache_k_win': nrm(ks[2], (DEPTH, DEC_BATCH, wb, H, dh), 1.0),
        'cache_v_win': nrm(ks[3], (DEPTH, DEC_BATCH, wb, H, dh), 1.0),
        'state_mlstm_C': nrm(ks[4], (DEPTH, DEC_BATCH, H, dh, dh), 0.1),
        'state_mlstm_n': nrm(ks[5], (DEPTH, DEC_BATCH, H, dh), 0.1),
        'state_mlstm_m': nrm(ks[6], (DEPTH, DEC_BATCH, H), 1.0),
        'state_hgrn_S': nrm(ks[7], (DEPTH, DEC_BATCH, H, dh, dh), 0.5),
        'rel_bias': nrm(ks[8], (N_BUCKETS, H), 0.5),
        'w_in': nrm(ks[9], (DEPTH, D_MODEL, D_IN), D_MODEL ** -0.5),
        'w_out': nrm(ks[10], (DEPTH, D_MODEL, D_MODEL), D_MODEL ** -0.5),
        'g_attn': 1.0 + nrm(ks[11], (DEPTH, D_MODEL), 0.01),
        'g_mlp': 1.0 + nrm(ks[12], (DEPTH, D_MODEL), 0.01),
        'w_up': nrm(ks[13], (DEPTH, D_MODEL, D_FF), D_MODEL ** -0.5),
        'w_down': nrm(ks[15], (DEPTH, D_FF, D_MODEL), D_FF ** -0.5),
        'b_i': nrm(ks[16], (DEPTH, H), 0.1),
        'b_f': b_f,
        'g_mlstm': 1.0 + nrm(ks[17], (DEPTH, MIX_W), 0.01),
        'g_cv': 1.0 + nrm(ks[18], (DEPTH, MIX_W), 0.01),
        'w_s': nrm(ks[19], (DEPTH, H, MLP_CHUNK, MLP_CHUNK), MLP_CHUNK ** -0.5),
        'b_s': 1.0 + nrm(ks[20], (DEPTH, H, MLP_CHUNK), 0.01),
        'hgrn_lb': nrm(ks[21], (DEPTH, MIX_W), 0.5),
        'g_hgrn': 1.0 + nrm(ks[22], (DEPTH, MIX_W), 0.01),
        'g_final': 1.0 + nrm(ks[23], (D_MODEL,), 0.01),
    }


def reference(x_prompt, x_sample, cache_k_win, cache_v_win, state_mlstm_C, state_mlstm_n, state_mlstm_m,
              state_hgrn_S, rel_bias, w_in, w_out, g_attn, g_mlp, w_up, w_down, b_i, b_f, g_mlstm, g_cv,
              w_s, b_s, hgrn_lb, g_hgrn, g_final):
    f32 = jnp.float32
    H, dh = N_HEADS_MIX, HEAD_DIM
    B, T = x_prompt.shape[:2]
    keep_prompt = min(MAX_WINDOW, T)
    keep_sample = cache_k_win.shape[2]
    sm = jax.nn.softmax(hgrn_lb.astype(f32), axis=0)
    lb_all = jnp.cumsum(sm, axis=0) - sm[0:1]
    empty = jnp.zeros((B, 0, H, dh), x_prompt.dtype)
    zC = jnp.zeros((B, H, dh, dh), f32)
    zn = jnp.zeros((B, H, dh), f32)
    zm = jnp.zeros((B, H), f32)
    xp, xs = x_prompt, x_sample
    kwp, vwp, kws, vws = [], [], [], []
    Cp, np_, mp, Cs, ns, ms = [], [], [], [], [], []
    Sp, Ss, cvs = [], [], []
    for l in range(DEPTH):
        wl = (w_in[l], w_out[l], g_attn[l], g_mlp[l], w_up[l], w_down[l], b_i[l], b_f[l],
              g_mlstm[l], g_cv[l], w_s[l], b_s[l], lb_all[l], g_hgrn[l], rel_bias)
        xp, k1, v1, C1, n1, m1, S1, _ = layer(xp, empty, empty, zC, zn, zm, zC, keep_prompt, *wl)
        kwp.append(k1); vwp.append(v1); Cp.append(C1); np_.append(n1); mp.append(m1); Sp.append(S1)
        xs, k2, v2, C2, n2, m2, S2, cv2 = layer(xs, cache_k_win[l], cache_v_win[l], state_mlstm_C[l],
                                                state_mlstm_n[l], state_mlstm_m[l], state_hgrn_S[l],
                                                keep_sample, *wl)
        kws.append(k2); vws.append(v2); Cs.append(C2); ns.append(n2); ms.append(m2); Ss.append(S2); cvs.append(cv2)
    y_prompt = rmsnorm(xp, g_final)
    y_sample = rmsnorm(xs, g_final)
    return (y_prompt, y_sample,
            jnp.stack(kwp), jnp.stack(vwp), jnp.stack(kws), jnp.stack(vws),
            jnp.stack(Cp), jnp.stack(np_), jnp.stack(mp),
            jnp.stack(Cs), jnp.stack(ns), jnp.stack(ms),
            jnp.stack(Sp), jnp.stack(Ss), jnp.stack(cvs))
```

```python
import functools
import math

import numpy as np
import jax
import jax.numpy as jnp
from jax import lax
from jax.experimental import pallas as pl
from jax.experimental.pallas import tpu as pltpu

F32 = jnp.float32
BF16 = jnp.bfloat16

D_MODEL = 1024
N_HEADS = 4
HEAD_DIM = 64
MIX_W = N_HEADS * HEAD_DIM
DILATIONS = ((128, 1), (512, 4), (2048, 16))
MAX_WINDOW = 2048
N_BUCKETS = 32
D_FF = 4 * D_MODEL
EPS = 1e-6
NEG = -1e30
LB_FLOOR = 1e-30
CHUNK = 128
N_GATE_COLS = 2 * N_HEADS
GATE_W = 128
Z_WIDTHS = (3 * MIX_W, 4 * MIX_W, GATE_W, 2 * MIX_W, 4 * MIX_W)
Z_TOTAL = sum(Z_WIDTHS)
ATTN_SCALE = HEAD_DIM ** -0.5
N_LEVELS = 7
VMEM_LIMIT = 48 * 1024 * 1024


def _dot(a, b):
    return jnp.dot(a, b, preferred_element_type=F32)


def _dot_nt(a, b):
    return lax.dot_general(a, b, (((1,), (1,)), ((), ())), preferred_element_type=F32)


def _dot_tn(a, b):
    return lax.dot_general(a, b, (((0,), (0,)), ((), ())), preferred_element_type=F32)


def _rms(x, g):
    return x * lax.rsqrt(jnp.mean(x * x, axis=-1, keepdims=True) + EPS) * g


def _log_sigmoid(x):
    return jnp.minimum(x, 0.0) - jnp.log1p(jnp.exp(-jnp.abs(x)))


def _dot_01(m01, f):
    hi = f.astype(BF16)
    r1 = f - hi.astype(F32)
    mid = r1.astype(BF16)
    lo = (r1 - mid.astype(F32)).astype(BF16)
    return _dot(m01, hi) + _dot(m01, mid) + _dot(m01, lo)


def _head(h):
    return slice(h * HEAD_DIM, (h + 1) * HEAD_DIM)


def _params(*sem):
    return pltpu.CompilerParams(dimension_semantics=sem, vmem_limit_bytes=VMEM_LIMIT)


def _inproj_body(x_ref, g_ref, w_ref, wg_lo_ref, *out_refs):
    hn = _rms(x_ref[...], g_ref[...])
    h = hn.astype(BF16)
    off = 0
    for o_ref in out_refs:
        n = o_ref.shape[-1]
        z = _dot(h, w_ref[:, off:off + n])
        if n == GATE_W:
            h_lo = (hn - h.astype(F32)).astype(BF16)
            z = z + _dot(h, wg_lo_ref[...]) + _dot(h_lo, w_ref[:, off:off + n])
        o_ref[...] = z
        off += n


def _inproj(x2d, g, w, wg_lo, tm):
    n = x2d.shape[0]
    return pl.pallas_call(
        _inproj_body,
        grid=(n // tm,),
        in_specs=[pl.BlockSpec((tm, D_MODEL), lambda i: (i, 0)),
                  pl.BlockSpec((1, D_MODEL), lambda i: (0, 0)),
                  pl.BlockSpec((D_MODEL, Z_TOTAL), lambda i: (0, 0)),
                  pl.BlockSpec((D_MODEL, GATE_W), lambda i: (0, 0))],
        out_specs=[pl.BlockSpec((tm, wd), lambda i: (i, 0)) for wd in Z_WIDTHS],
        out_shape=[jax.ShapeDtypeStruct((n, wd), F32) for wd in Z_WIDTHS],
        compiler_params=_params("parallel"),
        name="inproj",
    )(x2d, g, w, wg_lo)


def _attn_body(q_ref, kp_ref, kc_ref, vp_ref, vc_ref, tab_ref, o_ref, l_ref):
    first = pl.program_id(2) == 0
    q = q_ref[0] * ATTN_SCALE
    kp, kc, vp, vc = kp_ref[0], kc_ref[0], vp_ref[0], vc_ref[0]
    for h in range(N_HEADS):
        sl = _head(h)
        qh = q[:, sl].astype(BF16)
        sp = _dot_nt(qh, kp[:, sl].astype(BF16)) + tab_ref[h, :, 0:CHUNK]
        sc = _dot_nt(qh, kc[:, sl].astype(BF16)) + tab_ref[h, :, CHUNK:2 * CHUNK]
        sp = jnp.where(first, NEG, sp)
        m = jnp.maximum(jnp.max(sp, axis=1, keepdims=True), jnp.max(sc, axis=1, keepdims=True))
        pp = jnp.exp(sp - m)
        pc = jnp.exp(sc - m)
        den = jnp.sum(pp, axis=1, keepdims=True) + jnp.sum(pc, axis=1, keepdims=True)
        o = _dot(pp.astype(BF16), vp[:, sl].astype(BF16)) + _dot(pc.astype(BF16), vc[:, sl].astype(BF16))
        o_ref[0, :, sl] = o / den
        l_ref[0, :, sl] = jnp.broadcast_to(m + jnp.log(den), (CHUNK, HEAD_DIM))


def _attn_prompt(za, tab, d):
    B, T, _ = za.shape
    nb = T // d // CHUNK
    zv = za.reshape(B, T // d, d * 3 * MIX_W)
    blk = (1, CHUNK, MIX_W)
    prev = lambda i: jnp.maximum(i - 1, 0)
    o, l = pl.pallas_call(
        _attn_body,
        grid=(B, d, nb),
        in_specs=[pl.BlockSpec(blk, lambda b, r, i: (b, i, 3 * r)),
                  pl.BlockSpec(blk, lambda b, r, i: (b, prev(i), 3 * r + 1)),
                  pl.BlockSpec(blk, lambda b, r, i: (b, i, 3 * r + 1)),
                  pl.BlockSpec(blk, lambda b, r, i: (b, prev(i), 3 * r + 2)),
                  pl.BlockSpec(blk, lambda b, r, i: (b, i, 3 * r + 2)),
                  pl.BlockSpec((N_HEADS, CHUNK, 2 * CHUNK), lambda b, r, i: (0, 0, 0))],
        out_specs=[pl.BlockSpec(blk, lambda b, r, i: (b, i, r))] * 2,
        out_shape=[jax.ShapeDtypeStruct((B, T // d, d * MIX_W), F32)] * 2,
        compiler_params=_params("parallel", "parallel", "arbitrary"),
        name=f"attn_d{d}",
    )(zv, zv, zv, zv, zv, tab)
    return o.reshape(B * T, MIX_W), l.reshape(B * T, MIX_W)


SAMPLE_ROWS = 8


def _attn_sample_body(q_ref, kpast_ref, vpast_ref, knew_ref, vnew_ref, tpast_ref, tnew_ref, o_ref):
    lane_head = lax.broadcasted_iota(jnp.int32, (SAMPLE_ROWS, MIX_W), 1) // HEAD_DIM
    q8 = q_ref[0] * ATTN_SCALE
    qrows = jnp.concatenate([jnp.where(lane_head == h, q8, 0.0) for h in range(N_HEADS)], axis=0).astype(BF16)
    kpast = kpast_ref[0].astype(BF16)
    vpast = vpast_ref[0].astype(BF16)
    knew = knew_ref[0].astype(BF16)
    vnew = vnew_ref[0].astype(BF16)
    sp = _dot_nt(qrows, kpast)
    sn = _dot_nt(qrows, knew)
    outs, lses = [], []
    for p in range(len(DILATIONS)):
        lp = sp + tpast_ref[p]
        ln = sn + tnew_ref[p]
        m = jnp.maximum(jnp.max(lp, axis=1, keepdims=True), jnp.max(ln, axis=1, keepdims=True))
        pp = jnp.exp(lp - m)
        pn = jnp.exp(ln - m)
        den = jnp.sum(pp, axis=1, keepdims=True) + jnp.sum(pn, axis=1, keepdims=True)
        outs.append((_dot(pp.astype(BF16), vpast) + _dot(pn.astype(BF16), vnew)) / den)
        lses.append(m + jnp.log(den))
    mx = jnp.maximum(jnp.maximum(lses[0], lses[1]), lses[2])
    es = [jnp.exp(l - mx) for l in lses]
    merged = (es[0] * outs[0] + es[1] * outs[1] + es[2] * outs[2]) / (es[0] + es[1] + es[2])
    acc = jnp.zeros((SAMPLE_ROWS, MIX_W), F32)
    for h in range(N_HEADS):
        acc = acc + jnp.where(lane_head == h, merged[h * SAMPLE_ROWS:(h + 1) * SAMPLE_ROWS], 0.0)
    o_ref[0] = acc


def _attn_sample(q8, kpast, vpast, knew, vnew, tpast, tnew):
    B, P, _ = kpast.shape
    rows = N_HEADS * SAMPLE_ROWS
    npat = len(DILATIONS)
    return pl.pallas_call(
        _attn_sample_body,
        grid=(B,),
        in_specs=[pl.BlockSpec((1, SAMPLE_ROWS, MIX_W), lambda b: (b, 0, 0)),
                  pl.BlockSpec((1, P, MIX_W), lambda b: (b, 0, 0)),
                  pl.BlockSpec((1, P, MIX_W), lambda b: (b, 0, 0)),
                  pl.BlockSpec((1, CHUNK, MIX_W), lambda b: (b, 0, 0)),
                  pl.BlockSpec((1, CHUNK, MIX_W), lambda b: (b, 0, 0)),
                  pl.BlockSpec((npat, rows, P), lambda b: (0, 0, 0)),
                  pl.BlockSpec((npat, rows, CHUNK), lambda b: (0, 0, 0))],
        out_specs=pl.BlockSpec((1, SAMPLE_ROWS, MIX_W), lambda b: (b, 0, 0)),
        out_shape=jax.ShapeDtypeStruct((B, SAMPLE_ROWS, MIX_W), F32),
        compiler_params=_params("parallel"),
        name="attn_sample",
    )(q8, kpast, vpast, knew, vnew, tpast, tnew)


def _mlstm_body(zq_ref, zk_ref, zv_ref, zo_ref, zg_ref, bias_ref, gn_ref, c0_ref, n0_ref, m0_ref,
                out_ref, c_out, n_out, m_out, c_s, n_s, m_s, *, n_valid):
    c = pl.program_id(1)

    @pl.when(c == 0)
    def _():
        c_s[...] = c0_ref[0]
        n_s[...] = n0_ref[0]
        m_s[...] = m0_ref[0]

    row = lax.broadcasted_iota(jnp.int32, (CHUNK, CHUNK), 0)
    col = lax.broadcasted_iota(jnp.int32, (CHUNK, CHUNK), 1)
    causal = row >= col
    tril = causal.astype(BF16)
    gates = zg_ref[0]
    ipre = gates + bias_ref[0:1, :]
    logf = _log_sigmoid(gates + bias_ref[1:2, :])
    bcum = _dot_01(tril, logf)
    bcum_t = bcum.T
    ipre_t = ipre.T
    q = zq_ref[0]
    k = zk_ref[0] * ATTN_SCALE
    v = zv_ref[0]
    og = zo_ref[0]
    last = n_valid - 1
    valid_rows = lax.broadcasted_iota(jnp.int32, (CHUNK, 1), 0) < n_valid
    for h in range(N_HEADS):
        sl = _head(h)
        fcol = N_HEADS + h
        bcol = bcum[:, fcol:fcol + 1]
        brow = bcum_t[fcol:fcol + 1, :]
        irow = ipre_t[h:h + 1, :]
        icol = ipre[:, h:h + 1]
        mprev = m_s[:, h:h + 1]
        dmat = jnp.where(causal, bcol - brow + irow, NEG)
        g = bcol + mprev
        mt = jnp.maximum(g, jnp.max(dmat, axis=1, keepdims=True))
        dexp = jnp.exp(dmat - mt)
        gexp = jnp.exp(g - mt)
        qh, kh, vh = q[:, sl], k[:, sl], v[:, sl]
        qb, kb, vb = qh.astype(BF16), kh.astype(BF16), vh.astype(BF16)
        s = _dot_nt(qb, kb) * dexp
        ch = c_s[h]
        nh = n_s[h:h + 1, :]
        num = _dot(s.astype(BF16), vb) + gexp * _dot_nt(qb, ch.astype(BF16))
        nq = jnp.sum(s, axis=1, keepdims=True) + gexp * jnp.sum(qh * nh, axis=1, keepdims=True)
        hh = num / jnp.maximum(jnp.abs(nq), jnp.exp(-mt))
        mnew = mt[last:last + 1, :]
        blast = bcol[last:last + 1, :]
        wk = jnp.exp(blast - bcol + icol - mnew)
        if n_valid < CHUNK:
            wk = jnp.where(valid_rows, wk, 0.0)
        dc = jnp.exp(blast + mprev - mnew)
        c_s[h] = dc * ch + _dot_tn((vh * wk).astype(BF16), kb)
        n_s[h:h + 1, :] = dc * nh + jnp.sum(wk * kh, axis=0, keepdims=True)
        m_s[:, h:h + 1] = mnew
        out_ref[0, :, sl] = jax.nn.sigmoid(og[:, sl]) * _rms(hh, gn_ref[:, sl])

    @pl.when(c == pl.num_programs(1) - 1)
    def _():
        c_out[0] = c_s[...]
        n_out[0] = n_s[...]
        m_out[0] = m_s[...]


def _mlstm(zb, zg, bias, gn, c0, n0, m0, n_valid):
    B, T, _ = zb.shape
    nc = T // CHUNK
    blk = (1, CHUNK, MIX_W)
    zspec = lambda j: pl.BlockSpec(blk, lambda b, c: (b, c, j))
    st4 = pl.BlockSpec((1, N_HEADS, HEAD_DIM, HEAD_DIM), lambda b, c: (b, 0, 0, 0))
    st3 = pl.BlockSpec((1, N_HEADS, HEAD_DIM), lambda b, c: (b, 0, 0))
    st2 = pl.BlockSpec((1, 1, N_HEADS), lambda b, c: (b, 0, 0))
    return pl.pallas_call(
        functools.partial(_mlstm_body, n_valid=n_valid),
        grid=(B, nc),
        in_specs=[zspec(0), zspec(1), zspec(2), zspec(3),
                  pl.BlockSpec((1, CHUNK, GATE_W), lambda b, c: (b, c, 0)),
                  pl.BlockSpec((2, GATE_W), lambda b, c: (0, 0)),
                  pl.BlockSpec((1, MIX_W), lambda b, c: (0, 0)),
                  st4, st3, st2],
        out_specs=[pl.BlockSpec(blk, lambda b, c: (b, c, 0)), st4, st3, st2],
        out_shape=[jax.ShapeDtypeStruct((B, T, MIX_W), F32),
                   jax.ShapeDtypeStruct((B, N_HEADS, HEAD_DIM, HEAD_DIM), F32),
                   jax.ShapeDtypeStruct((B, N_HEADS, HEAD_DIM), F32),
                   jax.ShapeDtypeStruct((B, 1, N_HEADS), F32)],
        scratch_shapes=[pltpu.VMEM((N_HEADS, HEAD_DIM, HEAD_DIM), F32),
                        pltpu.VMEM((N_HEADS, HEAD_DIM), F32),
                        pltpu.VMEM((1, N_HEADS), F32)],
        compiler_params=_params("parallel", "arbitrary"),
        name="mlstm",
    )(zb, zb, zb, zb, zg, bias, gn, c0, n0, m0)


def _gate_body(zc_ref, w_ref, bs_ref, gcv_ref, out_ref, vrow_ref):
    z = zc_ref[0]
    u = jax.nn.gelu(z[:, :MIX_W])
    vn = _rms(jax.nn.gelu(z[:, MIX_W:]), gcv_ref[...])
    vrow_ref[0] = vn
    row = lax.broadcasted_iota(jnp.int32, (CHUNK, CHUNK), 0)
    col = lax.broadcasted_iota(jnp.int32, (CHUNK, CHUNK), 1)
    vb = vn.astype(BF16)
    for h in range(N_HEADS):
        sl = _head(h)
        w = jnp.where(row >= col, w_ref[h], 0.0).astype(BF16)
        s = _dot(w, vb[:, sl]) + bs_ref[:, h:h + 1]
        out_ref[0, :, sl] = u[:, sl] * s


def _gate(zc, w_s, bs_t, gcv):
    B, T, _ = zc.shape
    blk = (1, CHUNK, MIX_W)
    return pl.pallas_call(
        _gate_body,
        grid=(B, T // CHUNK),
        in_specs=[pl.BlockSpec((1, CHUNK, 2 * MIX_W), lambda b, c: (b, c, 0)),
                  pl.BlockSpec((N_HEADS, CHUNK, CHUNK), lambda b, c: (0, 0, 0)),
                  pl.BlockSpec((CHUNK, N_HEADS), lambda b, c: (0, 0)),
                  pl.BlockSpec((1, MIX_W), lambda b, c: (0, 0))],
        out_specs=[pl.BlockSpec(blk, lambda b, c: (b, c, 0))] * 2,
        out_shape=[jax.ShapeDtypeStruct((B, T, MIX_W), F32)] * 2,
        compiler_params=_params("parallel", "parallel"),
        name="gate",
    )(zc, w_s, bs_t, gcv)


def _hgrn_tables(n_valid):
    p = np.arange(CHUNK)[:, None]
    u = np.arange(CHUNK)[None, :]
    mats = []
    for l in range(N_LEVELS):
        m = CHUNK >> (l + 1)
        start = (p // m) * m
        odd = ((p // m) % 2) == 1
        mats.append(np.where(odd, (u >= start) & (u <= p), (u > p) & (u <= start + m - 1)))
    mats.append(u <= p)
    mats.append((u > p) & (u <= n_valid - 1))
    mall = np.concatenate(mats, axis=0).astype(np.float32)
    t = np.arange(CHUNK)[:, None]
    s = np.arange(CHUNK)[None, :]
    x = t ^ s
    top = np.floor(np.log2(np.maximum(x, 1))).astype(np.int32)
    lvl = np.where(s < t, N_LEVELS - 1 - top, np.where(s == t, N_LEVELS, N_LEVELS + 1)).astype(np.int32)
    return jnp.asarray(mall, BF16), jnp.asarray(lvl)


def _hgrn_body(zq_ref, zf_ref, zi_ref, zg_ref, lb_ref, gn_ref, mall_ref, lvl_ref, s0_ref,
               out_ref, s_out, s_s, *, n_valid):
    c = pl.program_id(1)

    @pl.when(c == 0)
    def _():
        s_s[...] = s0_ref[0]

    q = zq_ref[0]
    fx = zf_ref[0]
    v = zi_ref[0]
    lb = lb_ref[...]
    a = jnp.log(jnp.maximum(lb, LB_FLOOR))
    ct = jnp.log1p(-lb) + _log_sigmoid(fx)
    logf = jnp.maximum(a, ct) + jnp.log1p(jnp.exp(-jnp.abs(a - ct)))
    kd = (1.0 - lb) * jax.nn.sigmoid(-fx)
    gsum = _dot_01(mall_ref[...], logf)
    lvl = lvl_ref[...]
    amat = [jnp.zeros((CHUNK, CHUNK), F32) for _ in range(N_HEADS)]
    for l in range(N_LEVELS):
        e = jnp.exp(gsum[l * CHUNK:(l + 1) * CHUNK])
        qt = (q * e).astype(BF16)
        kt = (kd * e).astype(BF16)
        for h in range(N_HEADS):
            sl = _head(h)
            amat[h] = amat[h] + jnp.where(lvl == l, _dot_nt(qt[:, sl], kt[:, sl]), 0.0)
    qb = q.astype(BF16)
    kb = kd.astype(BF16)
    vb = v.astype(BF16)
    bcum = gsum[N_LEVELS * CHUNK:(N_LEVELS + 1) * CHUNK]
    q_in = (q * jnp.exp(bcum)).astype(BF16)
    valid_rows = lax.broadcasted_iota(jnp.int32, (CHUNK, 1), 0) < n_valid
    k_out = jnp.where(valid_rows, kd * jnp.exp(gsum[(N_LEVELS + 1) * CHUNK:]), 0.0).astype(BF16)
    e_last = jnp.exp(bcum[n_valid - 1:n_valid, :])
    og = zg_ref[0]
    for h in range(N_HEADS):
        sl = _head(h)
        ah = amat[h] + jnp.where(lvl == N_LEVELS, _dot_nt(qb[:, sl], kb[:, sl]), 0.0)
        sh = s_s[h]
        o = _dot(ah.astype(BF16), vb[:, sl]) + _dot_nt(q_in[:, sl], sh.astype(BF16))
        s_s[h] = e_last[:, sl] * sh + _dot_tn(vb[:, sl], k_out[:, sl])
        gate = og[:, sl]
        out_ref[0, :, sl] = _rms(o, gn_ref[:, sl]) * (gate * jax.nn.sigmoid(gate))

    @pl.when(c == pl.num_programs(1) - 1)
    def _():
        s_out[0] = s_s[...]


def _hgrn(zd, lb, gn, s0_t, n_valid):
    B, T, _ = zd.shape
    mall, lvl = _hgrn_tables(n_valid)
    blk = (1, CHUNK, MIX_W)
    zspec = lambda j: pl.BlockSpec(blk, lambda b, c: (b, c, j))
    st4 = pl.BlockSpec((1, N_HEADS, HEAD_DIM, HEAD_DIM), lambda b, c: (b, 0, 0, 0))
    vec = pl.BlockSpec((1, MIX_W), lambda b, c: (0, 0))
    return pl.pallas_call(
        functools.partial(_hgrn_body, n_valid=n_valid),
        grid=(B, T // CHUNK),
        in_specs=[zspec(0), zspec(1), zspec(2), zspec(3), vec, vec,
                  pl.BlockSpec(mall.shape, lambda b, c: (0, 0)),
                  pl.BlockSpec(lvl.shape, lambda b, c: (0, 0)),
                  st4],
        out_specs=[pl.BlockSpec(blk, lambda b, c: (b, c, 0)), st4],
        out_shape=[jax.ShapeDtypeStruct((B, T, MIX_W), F32),
                   jax.ShapeDtypeStruct((B, N_HEADS, HEAD_DIM, HEAD_DIM), F32)],
        scratch_shapes=[pltpu.VMEM((N_HEADS, HEAD_DIM, HEAD_DIM), F32)],
        compiler_params=_params("parallel", "arbitrary"),
        name="hgrn",
    )(zd, zd, zd, zd, lb, gn, mall, lvl, s0_t)


def _post_body(*refs, n_parts, final):
    x_ref = refs[0]
    n_part_refs = 1 if n_parts == 1 else 2 * n_parts
    parts = refs[1:1 + n_part_refs]
    ob_ref, oc_ref, od_ref, wout_ref, gm_ref, wup_ref, wdn_ref, gf_ref, y_ref, x1_s, h_s, acc_s = refs[1 + n_part_refs:]
    j = pl.program_id(1)

    @pl.when(j == 0)
    def _():
        if n_parts == 1:
            oa = parts[0][...]
        else:
            os_ = [r[...] for r in parts[:n_parts]]
            ls = [r[...] for r in parts[n_parts:]]
            mx = jnp.maximum(jnp.maximum(ls[0], ls[1]), ls[2])
            es = [jnp.exp(l - mx) for l in ls]
            oa = (es[0] * os_[0] + es[1] * os_[1] + es[2] * os_[2]) / (es[0] + es[1] + es[2])
        x1 = x_ref[...]
        for i, o in enumerate((oa, ob_ref[...], oc_ref[...], od_ref[...])):
            x1 = x1 + _dot(o.astype(BF16), wout_ref[i * MIX_W:(i + 1) * MIX_W, :])
        x1_s[...] = x1
        h_s[...] = _rms(x1, gm_ref[...]).astype(BF16)
        acc_s[...] = jnp.zeros_like(acc_s)

    up = jnp.maximum(_dot(h_s[...], wup_ref[...]), 0.0)
    acc_s[...] += _dot((up * up).astype(BF16), wdn_ref[...])

    @pl.when(j == pl.num_programs(1) - 1)
    def _():
        x2 = x1_s[...] + acc_s[...]
        y_ref[...] = _rms(x2, gf_ref[...]) if final else x2


def _post(x2d, parts, ob, oc, od, wout, gm, wup, wdn, gf, tm, tf, final):
    n = x2d.shape[0]
    n_parts = 1 if len(parts) == 1 else len(parts) // 2
    row = lambda wd: pl.BlockSpec((tm, wd), lambda i, j: (i, 0))
    vec = pl.BlockSpec((1, D_MODEL), lambda i, j: (0, 0))
    return pl.pallas_call(
        functools.partial(_post_body, n_parts=n_parts, final=final),
        grid=(n // tm, D_FF // tf),
        in_specs=[row(D_MODEL)] + [row(MIX_W)] * (len(parts) + 3)
                 + [pl.BlockSpec((D_MODEL, D_MODEL), lambda i, j: (0, 0)), vec,
                    pl.BlockSpec((D_MODEL, tf), lambda i, j: (0, j)),
                    pl.BlockSpec((tf, D_MODEL), lambda i, j: (j, 0)), vec],
        out_specs=row(D_MODEL),
        out_shape=jax.ShapeDtypeStruct((n, D_MODEL), F32),
        scratch_shapes=[pltpu.VMEM((tm, D_MODEL), F32), pltpu.VMEM((tm, D_MODEL), BF16),
                        pltpu.VMEM((tm, D_MODEL), F32)],
        compiler_params=_params("parallel", "arbitrary"),
        name="post",
    )(x2d, *parts, ob, oc, od, wout, gm, wup, wdn, gf)


def _rel_bucket(dist):
    max_exact = N_BUCKETS // 2
    d = jnp.maximum(dist, 1).astype(F32)
    large = max_exact + (jnp.log(d / max_exact) / math.log(MAX_WINDOW / max_exact)
                         * (N_BUCKETS - max_exact)).astype(jnp.int32)
    large = jnp.clip(large, max_exact, N_BUCKETS - 1)
    return jnp.where(dist < max_exact, dist, large)


def _pattern_bias(rel_bias, w, d):
    offs = jnp.arange(w // d + 1, dtype=jnp.int32) * d
    return rel_bias[_rel_bucket(offs)].T.astype(F32)


def _prompt_table(bias):
    j = np.arange(CHUNK)[:, None] + CHUNK - np.arange(2 * CHUNK)[None, :]
    valid = (j >= 0) & (j <= CHUNK)
    return jnp.where(valid[None], bias[:, np.clip(j, 0, CHUNK)], NEG)


def _sample_tables(bias, d, n_new, past):
    t = np.minimum(np.arange(SAMPLE_ROWS), n_new - 1)[:, None]
    delta = past + t - np.arange(past)[None, :]
    member = (delta % d == 0) & (delta // d <= CHUNK)
    tpast = jnp.where(member[None], bias[:, np.clip(delta // d, 0, CHUNK)], NEG)
    u = np.arange(CHUNK)[None, :]
    delta = t - u
    member = (delta >= 0) & (delta % d == 0) & (u < n_new)
    tnew = jnp.where(member[None], bias[:, np.clip(delta // d, 0, CHUNK)], NEG)
    rows = N_HEADS * SAMPLE_ROWS
    return tpast.reshape(rows, past), tnew.reshape(rows, CHUNK)


def _pad_rows(a, rows):
    return jnp.pad(a, ((0, 0), (0, rows - a.shape[1]), (0, 0)))


def kernel(x_prompt, x_sample, cache_k_win, cache_v_win, state_mlstm_C, state_mlstm_n, state_mlstm_m, state_hgrn_S, rel_bias, w_in, w_out, g_attn, g_mlp, w_up, w_down, b_i, b_f, g_mlstm, g_cv, w_s, b_s, hgrn_lb, g_hgrn, g_final):
    depth = w_in.shape[0]
    B, T, _ = x_prompt.shape
    Bs, Ts, _ = x_sample.shape
    past = cache_k_win.shape[2]
    assert past == MAX_WINDOW and T % (MAX_WINDOW) == 0 and Ts <= SAMPLE_ROWS
    H, dh = N_HEADS, HEAD_DIM
    keep_p = min(MAX_WINDOW, T)

    sm = jax.nn.softmax(hgrn_lb.astype(F32), axis=0)
    lb_all = jnp.cumsum(sm, axis=0) - sm[0:1]
    biases = [_pattern_bias(rel_bias, w, d) for w, d in DILATIONS]
    tabs_p = [_prompt_table(bb) for bb in biases]
    tabs_s = [_sample_tables(bb, d, Ts, past) for bb, (_, d) in zip(biases, DILATIONS)]
    tpast = jnp.stack([a for a, _ in tabs_s])
    tnew = jnp.stack([b for _, b in tabs_s])

    a_end = Z_WIDTHS[0] + Z_WIDTHS[1]
    g_end = a_end + N_GATE_COLS
    zeros_c = jnp.zeros((B, H, dh, dh), F32)
    zeros_n = jnp.zeros((B, H, dh), F32)
    zeros_m = jnp.zeros((B, 1, H), F32)
    vec = lambda a: a.reshape(1, -1).astype(F32)

    xp = x_prompt.reshape(B * T, D_MODEL)
    xs = x_sample.reshape(Bs * Ts, D_MODEL)
    outs = [[] for _ in range(13)]
    for l in range(depth):
        wl = w_in[l]
        w_gate = jnp.pad(wl[:, a_end:g_end], ((0, 0), (0, GATE_W - N_GATE_COLS)))
        w_z = jnp.concatenate([wl[:, :a_end], w_gate, wl[:, g_end:]], axis=1).astype(BF16)
        wg_lo = (w_gate - w_gate.astype(BF16).astype(F32)).astype(BF16)
        wout_b, wup_b, wdn_b = w_out[l].astype(BF16), w_up[l].astype(BF16), w_down[l].astype(BF16)
        gate_bias = jnp.zeros((2, GATE_W), F32).at[0, :H].set(b_i[l]).at[1, H:2 * H].set(b_f[l])
        bs_t = b_s[l].T.astype(F32)
        final = l == depth - 1
        gf = vec(g_final)

        za, zb, zg, zc, zd = _inproj(xp, vec(g_attn[l]), w_z, wg_lo, 512)
        za3 = za.reshape(B, T, -1)
        parts_o, parts_l = [], []
        for tab, (_, d) in zip(tabs_p, DILATIONS):
            o, lse = _attn_prompt(za3, tab, d)
            parts_o.append(o)
            parts_l.append(lse)
        ob, c1, n1, m1 = _mlstm(zb.reshape(B, T, -1), zg.reshape(B, T, -1), gate_bias, vec(g_mlstm[l]),
                                zeros_c, zeros_n, zeros_m, CHUNK)
        oc, _ = _gate(zc.reshape(B, T, -1), w_s[l], bs_t, vec(g_cv[l]))
        od, s1 = _hgrn(zd.reshape(B, T, -1), vec(lb_all[l]), vec(g_hgrn[l]), zeros_c, CHUNK)
        xp = _post(xp, parts_o + parts_l, ob.reshape(B * T, -1), oc.reshape(B * T, -1), od.reshape(B * T, -1),
                   wout_b, vec(g_mlp[l]), wup_b, wdn_b, gf, 512, 1024, final)
        outs[0].append(za3[:, T - keep_p:, MIX_W:2 * MIX_W].reshape(B, keep_p, H, dh))
        outs[1].append(za3[:, T - keep_p:, 2 * MIX_W:].reshape(B, keep_p, H, dh))
        outs[4].append(c1)
        outs[5].append(n1)
        outs[6].append(m1.reshape(B, H))
        outs[10].append(jnp.swapaxes(s1, -1, -2))

        za, zb, zg, zc, zd = _inproj(xs, vec(g_attn[l]), w_z, wg_lo, Bs * Ts)
        za3 = za.reshape(Bs, Ts, -1)
        k_new, v_new = za3[:, :, MIX_W:2 * MIX_W], za3[:, :, 2 * MIX_W:]
        k_past = cache_k_win[l].reshape(Bs, past, MIX_W)
        v_past = cache_v_win[l].reshape(Bs, past, MIX_W)
        oa = _attn_sample(_pad_rows(za3[:, :, :MIX_W], SAMPLE_ROWS), k_past, v_past,
                          _pad_rows(k_new, CHUNK), _pad_rows(v_new, CHUNK), tpast, tnew)[:, :Ts]
        pad3 = lambda z: _pad_rows(z.reshape(Bs, Ts, -1), CHUNK)
        ob, c2, n2, m2 = _mlstm(pad3(zb), pad3(zg), gate_bias, vec(g_mlstm[l]),
                                state_mlstm_C[l], state_mlstm_n[l], state_mlstm_m[l].reshape(Bs, 1, H), Ts)
        oc, vrows = _gate(pad3(zc), w_s[l], bs_t, vec(g_cv[l]))
        od, s2 = _hgrn(pad3(zd), vec(lb_all[l]), vec(g_hgrn[l]), jnp.swapaxes(state_hgrn_S[l], -1, -2), Ts)
        flat = lambda o: o[:, :Ts].reshape(Bs * Ts, MIX_W)
        xs = _post(xs, [flat(oa)], flat(ob), flat(oc), flat(od),
                   wout_b, vec(g_mlp[l]), wup_b, wdn_b, gf, Bs * Ts, 1024, final)
        outs[2].append(jnp.concatenate([k_past, k_new], axis=1)[:, -past:].reshape(Bs, past, H, dh))
        outs[3].append(jnp.concatenate([v_past, v_new], axis=1)[:, -past:].reshape(Bs, past, H, dh))
        outs[7].append(c2)
        outs[8].append(n2)
        outs[9].append(m2.reshape(Bs, H))
        outs[11].append(jnp.swapaxes(s2, -1, -2))
        outs[12].append(vrows[:, :Ts].reshape(Bs, Ts, H, dh))

    return (xp.reshape(B, T, D_MODEL), xs.reshape(Bs, Ts, D_MODEL)) + tuple(jnp.stack(o) for o in outs)
```

```python
import functools
import math

import numpy as np
import jax
import jax.numpy as jnp
from jax import lax
from jax.experimental import pallas as pl
from jax.experimental.pallas import tpu as pltpu

F32 = jnp.float32
BF16 = jnp.bfloat16

D_MODEL = 1024
N_HEADS = 4
HEAD_DIM = 64
MIX_W = N_HEADS * HEAD_DIM
DILATIONS = ((128, 1), (512, 4), (2048, 16))
MAX_WINDOW = 2048
N_BUCKETS = 32
D_FF = 4 * D_MODEL
EPS = 1e-6
NEG = -1e30
LB_FLOOR = 1e-30
CHUNK = 128
N_GATE_COLS = 2 * N_HEADS
GATE_W = 128
Z_WIDTHS = (3 * MIX_W, 4 * MIX_W, GATE_W, 2 * MIX_W, 4 * MIX_W)
Z_TOTAL = sum(Z_WIDTHS)
ATTN_SCALE = HEAD_DIM ** -0.5
N_LEVELS = 7
VMEM_LIMIT = 48 * 1024 * 1024


def _dot(a, b):
    return jnp.dot(a, b, preferred_element_type=F32)


def _dot_nt(a, b):
    return lax.dot_general(a, b, (((1,), (1,)), ((), ())), preferred_element_type=F32)


def _dot_tn(a, b):
    return lax.dot_general(a, b, (((0,), (0,)), ((), ())), preferred_element_type=F32)


def _rms(x, g):
    return x * lax.rsqrt(jnp.mean(x * x, axis=-1, keepdims=True) + EPS) * g


def _log_sigmoid(x):
    return jnp.minimum(x, 0.0) - jnp.log1p(jnp.exp(-jnp.abs(x)))


def _dot_01(m01, f):
    hi = f.astype(BF16)
    r1 = f - hi.astype(F32)
    mid = r1.astype(BF16)
    lo = (r1 - mid.astype(F32)).astype(BF16)
    return _dot(m01, hi) + _dot(m01, mid) + _dot(m01, lo)


def _head(h):
    return slice(h * HEAD_DIM, (h + 1) * HEAD_DIM)


def _params(*sem):
    return pltpu.CompilerParams(dimension_semantics=sem, vmem_limit_bytes=VMEM_LIMIT)


def _inproj_body(x_ref, g_ref, w_ref, wg_lo_ref, *out_refs):
    hn = _rms(x_ref[...], g_ref[...])
    h = hn.astype(BF16)
    off = 0
    for o_ref in out_refs:
        n = o_ref.shape[-1]
        z = _dot(h, w_ref[:, off:off + n])
        if n == GATE_W:
            h_lo = (hn - h.astype(F32)).astype(BF16)
            z = z + _dot(h, wg_lo_ref[...]) + _dot(h_lo, w_ref[:, off:off + n])
        o_ref[...] = z
        off += n


def _inproj(x2d, g, w, wg_lo, tm):
    n = x2d.shape[0]
    return pl.pallas_call(
        _inproj_body,
        grid=(n // tm,),
        in_specs=[pl.BlockSpec((tm, D_MODEL), lambda i: (i, 0)),
                  pl.BlockSpec((1, D_MODEL), lambda i: (0, 0)),
                  pl.BlockSpec((D_MODEL, Z_TOTAL), lambda i: (0, 0)),
                  pl.BlockSpec((D_MODEL, GATE_W), lambda i: (0, 0))],
        out_specs=[pl.BlockSpec((tm, wd), lambda i: (i, 0)) for wd in Z_WIDTHS],
        out_shape=[jax.ShapeDtypeStruct((n, wd), F32) for wd in Z_WIDTHS],
        compiler_params=_params("parallel"),
        name="inproj",
    )(x2d, g, w, wg_lo)


ATTN_QBLOCKS = 4


def _attn_body(q_ref, kp_ref, kc_ref, vp_ref, vc_ref, tab_ref, o_ref, l_ref, k_s, v_s):
    first = pl.program_id(2) == 0
    k_s[0:CHUNK, :] = kp_ref[0].astype(BF16)
    k_s[CHUNK:, :] = kc_ref[0].astype(BF16)
    v_s[0:CHUNK, :] = vp_ref[0].astype(BF16)
    v_s[CHUNK:, :] = vc_ref[0].astype(BF16)
    col = lax.broadcasted_iota(jnp.int32, (CHUNK, 2 * CHUNK), 1)
    for j in range(q_ref.shape[1] // CHUNK):
        rows = slice(j * CHUNK, (j + 1) * CHUNK)
        q = (q_ref[0, rows, :] * ATTN_SCALE).astype(BF16)
        kw = k_s[j * CHUNK:(j + 2) * CHUNK, :]
        vw = v_s[j * CHUNK:(j + 2) * CHUNK, :]
        for h in range(N_HEADS):
            sl = _head(h)
            s = _dot_nt(q[:, sl], kw[:, sl]) + tab_ref[h]
            if j == 0:
                s = jnp.where(first & (col < CHUNK), NEG, s)
            m = jnp.max(jnp.maximum(s[:, :CHUNK], s[:, CHUNK:]), axis=1, keepdims=True)
            p = jnp.exp(s - m)
            den = jnp.sum(p[:, :CHUNK] + p[:, CHUNK:], axis=1, keepdims=True)
            o_ref[0, rows, sl] = _dot(p.astype(BF16), vw[:, sl]) / den
            l_ref[0, rows, sl] = jnp.broadcast_to(m + jnp.log(den), (CHUNK, HEAD_DIM))


def _attn_prompt(za, tab, d):
    B, T, _ = za.shape
    qblocks = min(ATTN_QBLOCKS, T // d // CHUNK)
    rows = qblocks * CHUNK
    nb = T // d // rows
    zv = za.reshape(B, T // d, d * 3 * MIX_W)
    blk = (1, rows, MIX_W)
    pblk = (1, CHUNK, MIX_W)
    prev = lambda i: jnp.maximum(qblocks * i - 1, 0)
    o, l = pl.pallas_call(
        _attn_body,
        grid=(B, d, nb),
        in_specs=[pl.BlockSpec(blk, lambda b, r, i: (b, i, 3 * r)),
                  pl.BlockSpec(pblk, lambda b, r, i: (b, prev(i), 3 * r + 1)),
                  pl.BlockSpec(blk, lambda b, r, i: (b, i, 3 * r + 1)),
                  pl.BlockSpec(pblk, lambda b, r, i: (b, prev(i), 3 * r + 2)),
                  pl.BlockSpec(blk, lambda b, r, i: (b, i, 3 * r + 2)),
                  pl.BlockSpec((N_HEADS, CHUNK, 2 * CHUNK), lambda b, r, i: (0, 0, 0))],
        out_specs=[pl.BlockSpec(blk, lambda b, r, i: (b, i, r))] * 2,
        out_shape=[jax.ShapeDtypeStruct((B, T // d, d * MIX_W), F32)] * 2,
        scratch_shapes=[pltpu.VMEM((rows + CHUNK, MIX_W), BF16)] * 2,
        compiler_params=_params("parallel", "parallel", "arbitrary"),
        name=f"attn_d{d}",
    )(zv, zv, zv, zv, zv, tab)
    return o.reshape(B * T, MIX_W), l.reshape(B * T, MIX_W)


SAMPLE_ROWS = 8


def _attn_sample_body(q_ref, kpast_ref, vpast_ref, knew_ref, vnew_ref, tpast_ref, tnew_ref, o_ref):
    lane_head = lax.broadcasted_iota(jnp.int32, (SAMPLE_ROWS, MIX_W), 1) // HEAD_DIM
    q8 = q_ref[0] * ATTN_SCALE
    qrows = jnp.concatenate([jnp.where(lane_head == h, q8, 0.0) for h in range(N_HEADS)], axis=0).astype(BF16)
    kpast = kpast_ref[0].astype(BF16)
    vpast = vpast_ref[0].astype(BF16)
    knew = knew_ref[0].astype(BF16)
    vnew = vnew_ref[0].astype(BF16)
    sp = _dot_nt(qrows, kpast)
    sn = _dot_nt(qrows, knew)
    outs, lses = [], []
    for p in range(len(DILATIONS)):
        lp = sp + tpast_ref[p]
        ln = sn + tnew_ref[p]
        m = jnp.maximum(jnp.max(lp, axis=1, keepdims=True), jnp.max(ln, axis=1, keepdims=True))
        pp = jnp.exp(lp - m)
        pn = jnp.exp(ln - m)
        den = jnp.sum(pp, axis=1, keepdims=True) + jnp.sum(pn, axis=1, keepdims=True)
        outs.append((_dot(pp.astype(BF16), vpast) + _dot(pn.astype(BF16), vnew)) / den)
        lses.append(m + jnp.log(den))
    mx = jnp.maximum(jnp.maximum(lses[0], lses[1]), lses[2])
    es = [jnp.exp(l - mx) for l in lses]
    merged = (es[0] * outs[0] + es[1] * outs[1] + es[2] * outs[2]) / (es[0] + es[1] + es[2])
    acc = jnp.zeros((SAMPLE_ROWS, MIX_W), F32)
    for h in range(N_HEADS):
        acc = acc + jnp.where(lane_head == h, merged[h * SAMPLE_ROWS:(h + 1) * SAMPLE_ROWS], 0.0)
    o_ref[0] = acc


def _attn_sample(q8, kpast, vpast, knew, vnew, tpast, tnew):
    B, P, _ = kpast.shape
    rows = N_HEADS * SAMPLE_ROWS
    npat = len(DILATIONS)
    return pl.pallas_call(
        _attn_sample_body,
        grid=(B,),
        in_specs=[pl.BlockSpec((1, SAMPLE_ROWS, MIX_W), lambda b: (b, 0, 0)),
                  pl.BlockSpec((1, P, MIX_W), lambda b: (b, 0, 0)),
                  pl.BlockSpec((1, P, MIX_W), lambda b: (b, 0, 0)),
                  pl.BlockSpec((1, CHUNK, MIX_W), lambda b: (b, 0, 0)),
                  pl.BlockSpec((1, CHUNK, MIX_W), lambda b: (b, 0, 0)),
                  pl.BlockSpec((npat, rows, P), lambda b: (0, 0, 0)),
                  pl.BlockSpec((npat, rows, CHUNK), lambda b: (0, 0, 0))],
        out_specs=pl.BlockSpec((1, SAMPLE_ROWS, MIX_W), lambda b: (b, 0, 0)),
        out_shape=jax.ShapeDtypeStruct((B, SAMPLE_ROWS, MIX_W), F32),
        compiler_params=_params("parallel"),
        name="attn_sample",
    )(q8, kpast, vpast, knew, vnew, tpast, tnew)


def _mlstm_body(zq_ref, zk_ref, zv_ref, zo_ref, zg_ref, bias_ref, gn_ref, c0_ref, n0_ref, m0_ref,
                out_ref, c_out, n_out, m_out, c_s, n_s, m_s, *, n_valid):
    c = pl.program_id(1)

    @pl.when(c == 0)
    def _():
        c_s[...] = c0_ref[0]
        n_s[...] = n0_ref[0]
        m_s[...] = m0_ref[0]

    row = lax.broadcasted_iota(jnp.int32, (CHUNK, CHUNK), 0)
    col = lax.broadcasted_iota(jnp.int32, (CHUNK, CHUNK), 1)
    causal = row >= col
    tril = causal.astype(BF16)
    gates = zg_ref[0]
    ipre = gates + bias_ref[0:1, :]
    logf = _log_sigmoid(gates + bias_ref[1:2, :])
    bcum = _dot_01(tril, logf)
    bcum_t = bcum.T
    ipre_t = ipre.T
    q = zq_ref[0]
    k = zk_ref[0] * ATTN_SCALE
    v = zv_ref[0]
    og = zo_ref[0]
    last = n_valid - 1
    valid_rows = lax.broadcasted_iota(jnp.int32, (CHUNK, 1), 0) < n_valid
    for h in range(N_HEADS):
        sl = _head(h)
        fcol = N_HEADS + h
        bcol = bcum[:, fcol:fcol + 1]
        brow = bcum_t[fcol:fcol + 1, :]
        irow = ipre_t[h:h + 1, :]
        icol = ipre[:, h:h + 1]
        mprev = m_s[:, h:h + 1]
        dmat = jnp.where(causal, bcol - brow + irow, NEG)
        g = bcol + mprev
        mt = jnp.maximum(g, jnp.max(dmat, axis=1, keepdims=True))
        dexp = jnp.exp(dmat - mt)
        gexp = jnp.exp(g - mt)
        qh, kh, vh = q[:, sl], k[:, sl], v[:, sl]
        qb, kb, vb = qh.astype(BF16), kh.astype(BF16), vh.astype(BF16)
        s = _dot_nt(qb, kb) * dexp
        ch = c_s[h]
        nh = n_s[h:h + 1, :]
        num = _dot(s.astype(BF16), vb) + gexp * _dot_nt(qb, ch.astype(BF16))
        nq = jnp.sum(s, axis=1, keepdims=True) + gexp * jnp.sum(qh * nh, axis=1, keepdims=True)
        hh = num / jnp.maximum(jnp.abs(nq), jnp.exp(-mt))
        mnew = mt[last:last + 1, :]
        blast = bcol[last:last + 1, :]
        wk = jnp.exp(blast - bcol + icol - mnew)
        if n_valid < CHUNK:
            wk = jnp.where(valid_rows, wk, 0.0)
        dc = jnp.exp(blast + mprev - mnew)
        c_s[h] = dc * ch + _dot_tn((vh * wk).astype(BF16), kb)
        n_s[h:h + 1, :] = dc * nh + jnp.sum(wk * kh, axis=0, keepdims=True)
        m_s[:, h:h + 1] = mnew
        out_ref[0, :, sl] = jax.nn.sigmoid(og[:, sl]) * _rms(hh, gn_ref[:, sl])

    @pl.when(c == pl.num_programs(1) - 1)
    def _():
        c_out[0] = c_s[...]
        n_out[0] = n_s[...]
        m_out[0] = m_s[...]


def _mlstm(zb, zg, bias, gn, c0, n0, m0, n_valid):
    B, T, _ = zb.shape
    nc = T // CHUNK
    blk = (1, CHUNK, MIX_W)
    zspec = lambda j: pl.BlockSpec(blk, lambda b, c: (b, c, j))
    st4 = pl.BlockSpec((1, N_HEADS, HEAD_DIM, HEAD_DIM), lambda b, c: (b, 0, 0, 0))
    st3 = pl.BlockSpec((1, N_HEADS, HEAD_DIM), lambda b, c: (b, 0, 0))
    st2 = pl.BlockSpec((1, 1, N_HEADS), lambda b, c: (b, 0, 0))
    return pl.pallas_call(
        functools.partial(_mlstm_body, n_valid=n_valid),
        grid=(B, nc),
        in_specs=[zspec(0), zspec(1), zspec(2), zspec(3),
                  pl.BlockSpec((1, CHUNK, GATE_W), lambda b, c: (b, c, 0)),
                  pl.BlockSpec((2, GATE_W), lambda b, c: (0, 0)),
                  pl.BlockSpec((1, MIX_W), lambda b, c: (0, 0)),
                  st4, st3, st2],
        out_specs=[pl.BlockSpec(blk, lambda b, c: (b, c, 0)), st4, st3, st2],
        out_shape=[jax.ShapeDtypeStruct((B, T, MIX_W), F32),
                   jax.ShapeDtypeStruct((B, N_HEADS, HEAD_DIM, HEAD_DIM), F32),
                   jax.ShapeDtypeStruct((B, N_HEADS, HEAD_DIM), F32),
                   jax.ShapeDtypeStruct((B, 1, N_HEADS), F32)],
        scratch_shapes=[pltpu.VMEM((N_HEADS, HEAD_DIM, HEAD_DIM), F32),
                        pltpu.VMEM((N_HEADS, HEAD_DIM), F32),
                        pltpu.VMEM((1, N_HEADS), F32)],
        compiler_params=_params("parallel", "arbitrary"),
        name="mlstm",
    )(zb, zb, zb, zb, zg, bias, gn, c0, n0, m0)


def _gate_body(zc_ref, w_ref, bs_ref, gcv_ref, out_ref, vrow_ref):
    z = zc_ref[0]
    u = jax.nn.gelu(z[:, :MIX_W])
    vn = _rms(jax.nn.gelu(z[:, MIX_W:]), gcv_ref[...])
    vrow_ref[0] = vn
    row = lax.broadcasted_iota(jnp.int32, (CHUNK, CHUNK), 0)
    col = lax.broadcasted_iota(jnp.int32, (CHUNK, CHUNK), 1)
    vb = vn.astype(BF16)
    for h in range(N_HEADS):
        sl = _head(h)
        w = jnp.where(row >= col, w_ref[h], 0.0).astype(BF16)
        s = _dot(w, vb[:, sl]) + bs_ref[:, h:h + 1]
        out_ref[0, :, sl] = u[:, sl] * s


def _gate(zc, w_s, bs_t, gcv):
    B, T, _ = zc.shape
    blk = (1, CHUNK, MIX_W)
    return pl.pallas_call(
        _gate_body,
        grid=(B, T // CHUNK),
        in_specs=[pl.BlockSpec((1, CHUNK, 2 * MIX_W), lambda b, c: (b, c, 0)),
                  pl.BlockSpec((N_HEADS, CHUNK, CHUNK), lambda b, c: (0, 0, 0)),
                  pl.BlockSpec((CHUNK, N_HEADS), lambda b, c: (0, 0)),
                  pl.BlockSpec((1, MIX_W), lambda b, c: (0, 0))],
        out_specs=[pl.BlockSpec(blk, lambda b, c: (b, c, 0))] * 2,
        out_shape=[jax.ShapeDtypeStruct((B, T, MIX_W), F32)] * 2,
        compiler_params=_params("parallel", "parallel"),
        name="gate",
    )(zc, w_s, bs_t, gcv)


def _hgrn_tables(n_valid):
    p = np.arange(CHUNK)[:, None]
    u = np.arange(CHUNK)[None, :]
    mats = []
    for l in range(N_LEVELS):
        m = CHUNK >> (l + 1)
        start = (p // m) * m
        odd = ((p // m) % 2) == 1
        mats.append(np.where(odd, (u >= start) & (u <= p), (u > p) & (u <= start + m - 1)))
    mats.append(u <= p)
    mats.append((u > p) & (u <= n_valid - 1))
    mall = np.concatenate(mats, axis=0).astype(np.float32)
    t = np.arange(CHUNK)[:, None]
    s = np.arange(CHUNK)[None, :]
    x = t ^ s
    top = np.floor(np.log2(np.maximum(x, 1))).astype(np.int32)
    lvl = np.where(s < t, N_LEVELS - 1 - top, np.where(s == t, N_LEVELS, N_LEVELS + 1)).astype(np.int32)
    return jnp.asarray(mall, BF16), jnp.asarray(lvl)


def _hgrn_body(zq_ref, zf_ref, zi_ref, zg_ref, lb_ref, gn_ref, mall_ref, lvl_ref, s0_ref,
               out_ref, s_out, s_s, *, n_valid):
    c = pl.program_id(1)

    @pl.when(c == 0)
    def _():
        s_s[...] = s0_ref[0]

    q = zq_ref[0]
    fx = zf_ref[0]
    v = zi_ref[0]
    lb = lb_ref[...]
    a = jnp.log(jnp.maximum(lb, LB_FLOOR))
    ct = jnp.log1p(-lb) + _log_sigmoid(fx)
    logf = jnp.maximum(a, ct) + jnp.log1p(jnp.exp(-jnp.abs(a - ct)))
    kd = (1.0 - lb) * jax.nn.sigmoid(-fx)
    gsum = _dot_01(mall_ref[...], logf)
    lvl = lvl_ref[...]
    amat = [jnp.zeros((CHUNK, CHUNK), F32) for _ in range(N_HEADS)]
    for l in range(N_LEVELS):
        e = jnp.exp(gsum[l * CHUNK:(l + 1) * CHUNK])
        qt = (q * e).astype(BF16)
        kt = (kd * e).astype(BF16)
        for h in range(N_HEADS):
            sl = _head(h)
            amat[h] = amat[h] + jnp.where(lvl == l, _dot_nt(qt[:, sl], kt[:, sl]), 0.0)
    qb = q.astype(BF16)
    kb = kd.astype(BF16)
    vb = v.astype(BF16)
    bcum = gsum[N_LEVELS * CHUNK:(N_LEVELS + 1) * CHUNK]
    q_in = (q * jnp.exp(bcum)).astype(BF16)
    valid_rows = lax.broadcasted_iota(jnp.int32, (CHUNK, 1), 0) < n_valid
    k_out = jnp.where(valid_rows, kd * jnp.exp(gsum[(N_LEVELS + 1) * CHUNK:]), 0.0).astype(BF16)
    e_last = jnp.exp(bcum[n_valid - 1:n_valid, :])
    og = zg_ref[0]
    for h in range(N_HEADS):
        sl = _head(h)
        ah = amat[h] + jnp.where(lvl == N_LEVELS, _dot_nt(qb[:, sl], kb[:, sl]), 0.0)
        sh = s_s[h]
        o = _dot(ah.astype(BF16), vb[:, sl]) + _dot_nt(q_in[:, sl], sh.astype(BF16))
        s_s[h] = e_last[:, sl] * sh + _dot_tn(vb[:, sl], k_out[:, sl])
        gate = og[:, sl]
        out_ref[0, :, sl] = _rms(o, gn_ref[:, sl]) * (gate * jax.nn.sigmoid(gate))

    @pl.when(c == pl.num_programs(1) - 1)
    def _():
        s_out[0] = s_s[...]


def _hgrn(zd, lb, gn, s0_t, n_valid):
    B, T, _ = zd.shape
    mall, lvl = _hgrn_tables(n_valid)
    blk = (1, CHUNK, MIX_W)
    zspec = lambda j: pl.BlockSpec(blk, lambda b, c: (b, c, j))
    st4 = pl.BlockSpec((1, N_HEADS, HEAD_DIM, HEAD_DIM), lambda b, c: (b, 0, 0, 0))
    vec = pl.BlockSpec((1, MIX_W), lambda b, c: (0, 0))
    return pl.pallas_call(
        functools.partial(_hgrn_body, n_valid=n_valid),
        grid=(B, T // CHUNK),
        in_specs=[zspec(0), zspec(1), zspec(2), zspec(3), vec, vec,
                  pl.BlockSpec(mall.shape, lambda b, c: (0, 0)),
                  pl.BlockSpec(lvl.shape, lambda b, c: (0, 0)),
                  st4],
        out_specs=[pl.BlockSpec(blk, lambda b, c: (b, c, 0)), st4],
        out_shape=[jax.ShapeDtypeStruct((B, T, MIX_W), F32),
                   jax.ShapeDtypeStruct((B, N_HEADS, HEAD_DIM, HEAD_DIM), F32)],
        scratch_shapes=[pltpu.VMEM((N_HEADS, HEAD_DIM, HEAD_DIM), F32)],
        compiler_params=_params("parallel", "arbitrary"),
        name="hgrn",
    )(zd, zd, zd, zd, lb, gn, mall, lvl, s0_t)


def _post_body(*refs, n_parts, final):
    x_ref = refs[0]
    n_part_refs = 1 if n_parts == 1 else 2 * n_parts
    parts = refs[1:1 + n_part_refs]
    ob_ref, oc_ref, od_ref, wout_ref, gm_ref, wup_ref, wdn_ref, gf_ref, y_ref, x1_s, h_s, acc_s = refs[1 + n_part_refs:]
    j = pl.program_id(1)

    @pl.when(j == 0)
    def _():
        if n_parts == 1:
            oa = parts[0][...]
        else:
            os_ = [r[...] for r in parts[:n_parts]]
            ls = [r[...] for r in parts[n_parts:]]
            mx = jnp.maximum(jnp.maximum(ls[0], ls[1]), ls[2])
            es = [jnp.exp(l - mx) for l in ls]
            oa = (es[0] * os_[0] + es[1] * os_[1] + es[2] * os_[2]) / (es[0] + es[1] + es[2])
        x1 = x_ref[...]
        for i, o in enumerate((oa, ob_ref[...], oc_ref[...], od_ref[...])):
            x1 = x1 + _dot(o.astype(BF16), wout_ref[i * MIX_W:(i + 1) * MIX_W, :])
        x1_s[...] = x1
        h_s[...] = _rms(x1, gm_ref[...]).astype(BF16)
        acc_s[...] = jnp.zeros_like(acc_s)

    up = jnp.maximum(_dot(h_s[...], wup_ref[...]), 0.0)
    acc_s[...] += _dot((up * up).astype(BF16), wdn_ref[...])

    @pl.when(j == pl.num_programs(1) - 1)
    def _():
        x2 = x1_s[...] + acc_s[...]
        y_ref[...] = _rms(x2, gf_ref[...]) if final else x2


def _post(x2d, parts, ob, oc, od, wout, gm, wup, wdn, gf, tm, tf, final):
    n = x2d.shape[0]
    n_parts = 1 if len(parts) == 1 else len(parts) // 2
    row = lambda wd: pl.BlockSpec((tm, wd), lambda i, j: (i, 0))
    vec = pl.BlockSpec((1, D_MODEL), lambda i, j: (0, 0))
    return pl.pallas_call(
        functools.partial(_post_body, n_parts=n_parts, final=final),
        grid=(n // tm, D_FF // tf),
        in_specs=[row(D_MODEL)] + [row(MIX_W)] * (len(parts) + 3)
                 + [pl.BlockSpec((D_MODEL, D_MODEL), lambda i, j: (0, 0)), vec,
                    pl.BlockSpec((D_MODEL, tf), lambda i, j: (0, j)),
                    pl.BlockSpec((tf, D_MODEL), lambda i, j: (j, 0)), vec],
        out_specs=row(D_MODEL),
        out_shape=jax.ShapeDtypeStruct((n, D_MODEL), F32),
        scratch_shapes=[pltpu.VMEM((tm, D_MODEL), F32), pltpu.VMEM((tm, D_MODEL), BF16),
                        pltpu.VMEM((tm, D_MODEL), F32)],
        compiler_params=_params("parallel", "arbitrary"),
        name="post",
    )(x2d, *parts, ob, oc, od, wout, gm, wup, wdn, gf)


def _rel_bucket(dist):
    max_exact = N_BUCKETS // 2
    d = jnp.maximum(dist, 1).astype(F32)
    large = max_exact + (jnp.log(d / max_exact) / math.log(MAX_WINDOW / max_exact)
                         * (N_BUCKETS - max_exact)).astype(jnp.int32)
    large = jnp.clip(large, max_exact, N_BUCKETS - 1)
    return jnp.where(dist < max_exact, dist, large)


def _pattern_bias(rel_bias, w, d):
    offs = jnp.arange(w // d + 1, dtype=jnp.int32) * d
    return rel_bias[_rel_bucket(offs)].T.astype(F32)


def _prompt_table(bias):
    cols = 2 * CHUNK
    u = jnp.concatenate([bias[:, ::-1], jnp.full((N_HEADS, cols - CHUNK), NEG, F32)], axis=1)
    return jnp.tile(u, (1, CHUNK))[:, :CHUNK * cols].reshape(N_HEADS, CHUNK, cols)


def _sample_tables(bias, d, n_new, past):
    comb = jnp.concatenate([bias[:, :, None], jnp.full((N_HEADS, CHUNK + 1, d - 1), NEG, F32)], axis=2)
    comb = comb.reshape(N_HEADS, (CHUNK + 1) * d)
    length = past + SAMPLE_ROWS + 1
    comb = comb[:, :length]
    comb = jnp.pad(comb, ((0, 0), (0, length - comb.shape[1])), constant_values=NEG)
    rev = comb[:, ::-1]
    tpast, tnew = [], []
    for row in range(SAMPLE_ROWS):
        t = min(row, n_new - 1)
        start = length - 1 - past - t
        tpast.append(rev[:, start:start + past])
        start = length - 1 - t
        tnew.append(jnp.pad(rev[:, start:start + t + 1], ((0, 0), (0, CHUNK - t - 1)), constant_values=NEG))
    rows = N_HEADS * SAMPLE_ROWS
    return jnp.stack(tpast, axis=1).reshape(rows, past), jnp.stack(tnew, axis=1).reshape(rows, CHUNK)


def _pad_rows(a, rows):
    return jnp.pad(a, ((0, 0), (0, rows - a.shape[1]), (0, 0)))


def kernel(x_prompt, x_sample, cache_k_win, cache_v_win, state_mlstm_C, state_mlstm_n, state_mlstm_m, state_hgrn_S, rel_bias, w_in, w_out, g_attn, g_mlp, w_up, w_down, b_i, b_f, g_mlstm, g_cv, w_s, b_s, hgrn_lb, g_hgrn, g_final):
    depth = w_in.shape[0]
    B, T, _ = x_prompt.shape
    Bs, Ts, _ = x_sample.shape
    past = cache_k_win.shape[2]
    assert past == MAX_WINDOW and T % (MAX_WINDOW) == 0 and Ts <= SAMPLE_ROWS
    H, dh = N_HEADS, HEAD_DIM
    keep_p = min(MAX_WINDOW, T)

    sm = jax.nn.softmax(hgrn_lb.astype(F32), axis=0)
    lb_all = jnp.cumsum(sm, axis=0) - sm[0:1]
    biases = [_pattern_bias(rel_bias, w, d) for w, d in DILATIONS]
    tabs_p = [_prompt_table(bb) for bb in biases]
    tabs_s = [_sample_tables(bb, d, Ts, past) for bb, (_, d) in zip(biases, DILATIONS)]
    tpast = jnp.stack([a for a, _ in tabs_s])
    tnew = jnp.stack([b for _, b in tabs_s])

    a_end = Z_WIDTHS[0] + Z_WIDTHS[1]
    g_end = a_end + N_GATE_COLS
    zeros_c = jnp.zeros((B, H, dh, dh), F32)
    zeros_n = jnp.zeros((B, H, dh), F32)
    zeros_m = jnp.zeros((B, 1, H), F32)
    vec = lambda a: a.reshape(1, -1).astype(F32)

    xp = x_prompt.reshape(B * T, D_MODEL)
    xs = x_sample.reshape(Bs * Ts, D_MODEL)
    outs = [[] for _ in range(13)]
    for l in range(depth):
        wl = w_in[l]
        w_gate = jnp.pad(wl[:, a_end:g_end], ((0, 0), (0, GATE_W - N_GATE_COLS)))
        w_z = jnp.concatenate([wl[:, :a_end], w_gate, wl[:, g_end:]], axis=1).astype(BF16)
        wg_lo = (w_gate - w_gate.astype(BF16).astype(F32)).astype(BF16)
        wout_b, wup_b, wdn_b = w_out[l].astype(BF16), w_up[l].astype(BF16), w_down[l].astype(BF16)
        gate_bias = jnp.zeros((2, GATE_W), F32).at[0, :H].set(b_i[l]).at[1, H:2 * H].set(b_f[l])
        bs_t = b_s[l].T.astype(F32)
        final = l == depth - 1
        gf = vec(g_final)

        za, zb, zg, zc, zd = _inproj(xp, vec(g_attn[l]), w_z, wg_lo, 512)
        za3 = za.reshape(B, T, -1)
        parts_o, parts_l = [], []
        for tab, (_, d) in zip(tabs_p, DILATIONS):
            o, lse = _attn_prompt(za3, tab, d)
            parts_o.append(o)
            parts_l.append(lse)
        ob, c1, n1, m1 = _mlstm(zb.reshape(B, T, -1), zg.reshape(B, T, -1), gate_bias, vec(g_mlstm[l]),
                                zeros_c, zeros_n, zeros_m, CHUNK)
        oc, _ = _gate(zc.reshape(B, T, -1), w_s[l], bs_t, vec(g_cv[l]))
        od, s1 = _hgrn(zd.reshape(B, T, -1), vec(lb_all[l]), vec(g_hgrn[l]), zeros_c, CHUNK)
        xp = _post(xp, parts_o + parts_l, ob.reshape(B * T, -1), oc.reshape(B * T, -1), od.reshape(B * T, -1),
                   wout_b, vec(g_mlp[l]), wup_b, wdn_b, gf, 512, 1024, final)
        outs[0].append(za3[:, T - keep_p:, MIX_W:2 * MIX_W].reshape(B, keep_p, H, dh))
        outs[1].append(za3[:, T - keep_p:, 2 * MIX_W:].reshape(B, keep_p, H, dh))
        outs[4].append(c1)
        outs[5].append(n1)
        outs[6].append(m1.reshape(B, H))
        outs[10].append(jnp.swapaxes(s1, -1, -2))

        za, zb, zg, zc, zd = _inproj(xs, vec(g_attn[l]), w_z, wg_lo, Bs * Ts)
        za3 = za.reshape(Bs, Ts, -1)
        k_new, v_new = za3[:, :, MIX_W:2 * MIX_W], za3[:, :, 2 * MIX_W:]
        k_past = cache_k_win[l].reshape(Bs, past, MIX_W)
        v_past = cache_v_win[l].reshape(Bs, past, MIX_W)
        oa = _attn_sample(_pad_rows(za3[:, :, :MIX_W], SAMPLE_ROWS), k_past, v_past,
                          _pad_rows(k_new, CHUNK), _pad_rows(v_new, CHUNK), tpast, tnew)[:, :Ts]
        pad3 = lambda z: _pad_rows(z.reshape(Bs, Ts, -1), CHUNK)
        ob, c2, n2, m2 = _mlstm(pad3(zb), pad3(zg), gate_bias, vec(g_mlstm[l]),
                                state_mlstm_C[l], state_mlstm_n[l], state_mlstm_m[l].reshape(Bs, 1, H), Ts)
        oc, vrows = _gate(pad3(zc), w_s[l], bs_t, vec(g_cv[l]))
        od, s2 = _hgrn(pad3(zd), vec(lb_all[l]), vec(g_hgrn[l]), jnp.swapaxes(state_hgrn_S[l], -1, -2), Ts)
        flat = lambda o: o[:, :Ts].reshape(Bs * Ts, MIX_W)
        xs = _post(xs, [flat(oa)], flat(ob), flat(oc), flat(od),
                   wout_b, vec(g_mlp[l]), wup_b, wdn_b, gf, Bs * Ts, 1024, final)
        outs[2].append(jnp.concatenate([k_past, k_new], axis=1)[:, -past:].reshape(Bs, past, H, dh))
        outs[3].append(jnp.concatenate([v_past, v_new], axis=1)[:, -past:].reshape(Bs, past, H, dh))
        outs[7].append(c2)
        outs[8].append(n2)
        outs[9].append(m2.reshape(Bs, H))
        outs[11].append(jnp.swapaxes(s2, -1, -2))
        outs[12].append(vrows[:, :Ts].reshape(Bs, Ts, H, dh))

    return (xp.reshape(B, T, D_MODEL), xs.reshape(Bs, Ts, D_MODEL)) + tuple(jnp.stack(o) for o in outs)
```

```python
import functools
import math

import numpy as np
import jax
import jax.numpy as jnp
from jax import lax
from jax.experimental import pallas as pl
from jax.experimental.pallas import tpu as pltpu

F32 = jnp.float32
BF16 = jnp.bfloat16

D_MODEL = 1024
N_HEADS = 4
HEAD_DIM = 64
MIX_W = N_HEADS * HEAD_DIM
DILATIONS = ((128, 1), (512, 4), (2048, 16))
MAX_WINDOW = 2048
N_BUCKETS = 32
D_FF = 4 * D_MODEL
EPS = 1e-6
NEG = -1e30
LB_FLOOR = 1e-30
CHUNK = 128
N_GATE_COLS = 2 * N_HEADS
GATE_W = 128
Z_WIDTHS = (3 * MIX_W, 4 * MIX_W, GATE_W, 2 * MIX_W, 4 * MIX_W)
Z_TOTAL = sum(Z_WIDTHS)
ATTN_SCALE = HEAD_DIM ** -0.5
N_LEVELS = 7
VMEM_LIMIT = 48 * 1024 * 1024


def _dot(a, b):
    return jnp.dot(a, b, preferred_element_type=F32)


def _dot_nt(a, b):
    return lax.dot_general(a, b, (((1,), (1,)), ((), ())), preferred_element_type=F32)


def _dot_tn(a, b):
    return lax.dot_general(a, b, (((0,), (0,)), ((), ())), preferred_element_type=F32)


def _rms(x, g):
    return x * lax.rsqrt(jnp.mean(x * x, axis=-1, keepdims=True) + EPS) * g


def _log_sigmoid(x):
    return jnp.minimum(x, 0.0) - jnp.log1p(jnp.exp(-jnp.abs(x)))


def _dot_01(m01, f):
    hi = f.astype(BF16)
    r1 = f - hi.astype(F32)
    mid = r1.astype(BF16)
    lo = (r1 - mid.astype(F32)).astype(BF16)
    return _dot(m01, hi) + _dot(m01, mid) + _dot(m01, lo)


def _head(h):
    return slice(h * HEAD_DIM, (h + 1) * HEAD_DIM)


def _params(*sem):
    return pltpu.CompilerParams(dimension_semantics=sem, vmem_limit_bytes=VMEM_LIMIT)


def _inproj_body(x_ref, g_ref, w_ref, wg_lo_ref, *out_refs):
    hn = _rms(x_ref[...], g_ref[...])
    h = hn.astype(BF16)
    off = 0
    for o_ref in out_refs:
        n = o_ref.shape[-1]
        z = _dot(h, w_ref[:, off:off + n])
        if n == GATE_W:
            h_lo = (hn - h.astype(F32)).astype(BF16)
            z = z + _dot(h, wg_lo_ref[...]) + _dot(h_lo, w_ref[:, off:off + n])
        o_ref[...] = z
        off += n


def _inproj(x2d, g, w, wg_lo, tm):
    n = x2d.shape[0]
    return pl.pallas_call(
        _inproj_body,
        grid=(n // tm,),
        in_specs=[pl.BlockSpec((tm, D_MODEL), lambda i: (i, 0)),
                  pl.BlockSpec((1, D_MODEL), lambda i: (0, 0)),
                  pl.BlockSpec((D_MODEL, Z_TOTAL), lambda i: (0, 0)),
                  pl.BlockSpec((D_MODEL, GATE_W), lambda i: (0, 0))],
        out_specs=[pl.BlockSpec((tm, wd), lambda i: (i, 0)) for wd in Z_WIDTHS],
        out_shape=[jax.ShapeDtypeStruct((n, wd), F32) for wd in Z_WIDTHS],
        compiler_params=_params("parallel"),
        name="inproj",
    )(x2d, g, w, wg_lo)


ATTN_ROWS = MAX_WINDOW
LANES = 128
UNITS_PER_ITER = 4


def _attn_body(q0, q1, kp0, kp1, kc0, kc1, vp0, vp1, vc0, vc1, tab_ref, o_ref, os_ref, ls_ref):
    first = pl.program_id(1) == 0
    col = lax.broadcasted_iota(jnp.int32, (CHUNK, 2 * CHUNK), 1)
    q_refs, kp_refs, kc_refs, vp_refs, vc_refs = (q0, q1), (kp0, kp1), (kc0, kc1), (vp0, vp1), (vc0, vc1)

    def rows_of(start, d):
        if d == 1:
            return pl.ds(start if isinstance(start, int) else pl.multiple_of(start, CHUNK), CHUNK)
        return pl.ds(start, CHUNK, stride=d)

    def unit(p, d, r, c, prev_in_block):
        start = r + d * CHUNK * c
        cur = rows_of(start, d)
        prev = rows_of(start - d * CHUNK, d) if prev_in_block else rows_of(r + ATTN_ROWS - d * CHUNK, d)
        for half in range(2):
            k_prev_ref, v_prev_ref = (kc_refs, vc_refs) if prev_in_block else (kp_refs, vp_refs)
            q = (q_refs[half][0, cur, :] * ATTN_SCALE).astype(BF16)
            kw = jnp.concatenate([k_prev_ref[half][0, prev, :], kc_refs[half][0, cur, :]], axis=0).astype(BF16)
            vw = jnp.concatenate([v_prev_ref[half][0, prev, :], vc_refs[half][0, cur, :]], axis=0).astype(BF16)
            outs, lses = [], []
            for hh in range(2):
                sl = _head(hh)
                s = _dot_nt(q[:, sl], kw[:, sl]) + tab_ref[p, 2 * half + hh]
                if not prev_in_block:
                    s = jnp.where(first & (col < CHUNK), NEG, s)
                m = jnp.max(jnp.maximum(s[:, :CHUNK], s[:, CHUNK:]), axis=1, keepdims=True)
                e = jnp.exp(s - m)
                den = jnp.sum(e[:, :CHUNK] + e[:, CHUNK:], axis=1, keepdims=True)
                outs.append(_dot(e.astype(BF16), vw[:, sl]) / den)
                lses.append(jnp.broadcast_to(m + jnp.log(den), (CHUNK, HEAD_DIM)))
            os_ref[p, half, cur, :] = jnp.concatenate(outs, axis=1)
            ls_ref[p, half, cur, :] = jnp.concatenate(lses, axis=1)

    for p, (_, d) in enumerate(DILATIONS):
        nblk = ATTN_ROWS // d // CHUNK
        if nblk >= UNITS_PER_ITER:
            def per_subsequence(r, carry, p=p, d=d, nblk=nblk):
                for c in range(UNITS_PER_ITER):
                    unit(p, d, r, c, c > 0)

                def group(g, carry2):
                    for u in range(UNITS_PER_ITER):
                        unit(p, d, r, g * UNITS_PER_ITER + u, True)
                    return carry2
                return lax.fori_loop(1, nblk // UNITS_PER_ITER, group, carry)
            if d == 1:
                per_subsequence(0, 0)
            else:
                lax.fori_loop(0, d, per_subsequence, 0)
        else:
            def group(g, carry, p=p, d=d):
                for u in range(UNITS_PER_ITER):
                    unit(p, d, g * UNITS_PER_ITER + u, 0, False)
                return carry
            lax.fori_loop(0, d // UNITS_PER_ITER, group, 0)

    def merge(i, carry):
        rows = pl.ds(pl.multiple_of(i * CHUNK, CHUNK), CHUNK)
        for half in range(2):
            ls = [ls_ref[p, half, rows, :] for p in range(len(DILATIONS))]
            mx = jnp.maximum(jnp.maximum(ls[0], ls[1]), ls[2])
            es = [jnp.exp(l - mx) for l in ls]
            num = es[0] * os_ref[0, half, rows, :] + es[1] * os_ref[1, half, rows, :] + es[2] * os_ref[2, half, rows, :]
            o_ref[0, rows, half * LANES:(half + 1) * LANES] = num / (es[0] + es[1] + es[2])
        return carry
    lax.fori_loop(0, ATTN_ROWS // CHUNK, merge, 0)


def _attn_prompt(za, tabs):
    B, T, _ = za.shape
    blk = (1, ATTN_ROWS, LANES)
    cur = lambda j: pl.BlockSpec(blk, lambda b, i: (b, i, j))
    prv = lambda j: pl.BlockSpec(blk, lambda b, i: (b, jnp.maximum(i - 1, 0), j))
    npat = len(DILATIONS)
    return pl.pallas_call(
        _attn_body,
        grid=(B, T // ATTN_ROWS),
        in_specs=[cur(0), cur(1), prv(2), prv(3), cur(2), cur(3), prv(4), prv(5), cur(4), cur(5),
                  pl.BlockSpec((npat, N_HEADS, CHUNK, 2 * CHUNK), lambda b, i: (0, 0, 0, 0))],
        out_specs=pl.BlockSpec((1, ATTN_ROWS, MIX_W), lambda b, i: (b, i, 0)),
        out_shape=jax.ShapeDtypeStruct((B, T, MIX_W), F32),
        scratch_shapes=[pltpu.VMEM((npat, 2, ATTN_ROWS, LANES), F32)] * 2,
        compiler_params=_params("parallel", "arbitrary"),
        name="attn_prompt",
    )(*([za] * 10), tabs)


SAMPLE_ROWS = 8


def _attn_sample_body(q_ref, kt_ref, vt_ref, knt_ref, vnt_ref, tpast_ref, tnew_ref, o_ref, kw_ref, vw_ref, *, n_new):
    npat = len(DILATIONS)
    q8 = q_ref[0] * ATTN_SCALE
    for h in range(N_HEADS):
        sl = _head(h)
        qh = q8[:, sl].astype(BF16)
        kt, vt = kt_ref[0, 0, h], vt_ref[0, 0, h]
        knt, vnt = knt_ref[0, h], vnt_ref[0, h]
        sp = _dot(qh, kt.astype(BF16))
        sn = _dot(qh, knt.astype(BF16))
        pps, pns, dens, lses = [], [], [], []
        for p in range(npat):
            lp = sp + tpast_ref[p, h]
            ln = sn + tnew_ref[p, h]
            m = jnp.maximum(jnp.max(lp, axis=1, keepdims=True), jnp.max(ln, axis=1, keepdims=True))
            pps.append(jnp.exp(lp - m))
            pns.append(jnp.exp(ln - m))
            dens.append(jnp.sum(pps[-1], axis=1, keepdims=True) + jnp.sum(pns[-1], axis=1, keepdims=True))
            lses.append(m + jnp.log(dens[-1]))
        pv = (_dot_nt(jnp.concatenate(pps, axis=0).astype(BF16), vt.astype(BF16))
              + _dot_nt(jnp.concatenate(pns, axis=0).astype(BF16), vnt.astype(BF16)))
        mx = jnp.maximum(jnp.maximum(lses[0], lses[1]), lses[2])
        es = [jnp.exp(l - mx) for l in lses]
        num = sum(es[p] * pv[p * SAMPLE_ROWS:(p + 1) * SAMPLE_ROWS] / dens[p] for p in range(npat))
        o_ref[0, :, sl] = num / (es[0] + es[1] + es[2])
        kw_ref[0, h] = jnp.concatenate([kt[:, n_new:], knt[:, :n_new]], axis=1)
        vw_ref[0, h] = jnp.concatenate([vt[:, n_new:], vnt[:, :n_new]], axis=1)


def _attn_sample(layer, q8, kt_all, vt_all, knt, vnt, tpast, tnew, n_new):
    _, B, _, _, P = kt_all.shape
    npat = len(DILATIONS)
    cache = pl.BlockSpec((1, 1, N_HEADS, HEAD_DIM, P), lambda b: (layer, b, 0, 0, 0))
    new = pl.BlockSpec((1, N_HEADS, HEAD_DIM, LANES), lambda b: (b, 0, 0, 0))
    win = pl.BlockSpec((1, N_HEADS, HEAD_DIM, P), lambda b: (b, 0, 0, 0))
    rows = pl.BlockSpec((1, SAMPLE_ROWS, MIX_W), lambda b: (b, 0, 0))
    return pl.pallas_call(
        functools.partial(_attn_sample_body, n_new=n_new),
        grid=(B,),
        in_specs=[rows, cache, cache, new, new,
                  pl.BlockSpec((npat, N_HEADS, SAMPLE_ROWS, P), lambda b: (0, 0, 0, 0)),
                  pl.BlockSpec((npat, N_HEADS, SAMPLE_ROWS, LANES), lambda b: (0, 0, 0, 0))],
        out_specs=[rows, win, win],
        out_shape=[jax.ShapeDtypeStruct((B, SAMPLE_ROWS, MIX_W), F32),
                   jax.ShapeDtypeStruct((B, N_HEADS, HEAD_DIM, P), F32),
                   jax.ShapeDtypeStruct((B, N_HEADS, HEAD_DIM, P), F32)],
        compiler_params=_params("parallel"),
        name="attn_sample",
    )(q8, kt_all, vt_all, knt, vnt, tpast, tnew)


def _mlstm_body(zq_ref, zk_ref, zv_ref, zo_ref, zg_ref, bias_ref, gn_ref, c0_ref, n0_ref, m0_ref,
                out_ref, c_out, n_out, m_out, c_s, n_s, m_s, *, n_valid):
    c = pl.program_id(1)

    @pl.when(c == 0)
    def _():
        c_s[...] = c0_ref[0]
        n_s[...] = n0_ref[0]
        m_s[...] = m0_ref[0]

    row = lax.broadcasted_iota(jnp.int32, (CHUNK, CHUNK), 0)
    col = lax.broadcasted_iota(jnp.int32, (CHUNK, CHUNK), 1)
    causal = row >= col
    tril = causal.astype(BF16)
    gates = zg_ref[0]
    ipre = gates + bias_ref[0:1, :]
    logf = _log_sigmoid(gates + bias_ref[1:2, :])
    bcum = _dot_01(tril, logf)
    bcum_t = bcum.T
    ipre_t = ipre.T
    q = zq_ref[0]
    k = zk_ref[0] * ATTN_SCALE
    v = zv_ref[0]
    og = zo_ref[0]
    last = n_valid - 1
    valid_rows = lax.broadcasted_iota(jnp.int32, (CHUNK, 1), 0) < n_valid
    for h in range(N_HEADS):
        sl = _head(h)
        fcol = N_HEADS + h
        bcol = bcum[:, fcol:fcol + 1]
        brow = bcum_t[fcol:fcol + 1, :]
        irow = ipre_t[h:h + 1, :]
        icol = ipre[:, h:h + 1]
        mprev = m_s[:, h:h + 1]
        dmat = jnp.where(causal, bcol - brow + irow, NEG)
        g = bcol + mprev
        mt = jnp.maximum(g, jnp.max(dmat, axis=1, keepdims=True))
        dexp = jnp.exp(dmat - mt)
        gexp = jnp.exp(g - mt)
        qh, kh, vh = q[:, sl], k[:, sl], v[:, sl]
        qb, kb, vb = qh.astype(BF16), kh.astype(BF16), vh.astype(BF16)
        s = _dot_nt(qb, kb) * dexp
        ch = c_s[h]
        nh = n_s[h:h + 1, :]
        num = _dot(s.astype(BF16), vb) + gexp * _dot_nt(qb, ch.astype(BF16))
        nq = jnp.sum(s, axis=1, keepdims=True) + gexp * jnp.sum(qh * nh, axis=1, keepdims=True)
        hh = num / jnp.maximum(jnp.abs(nq), jnp.exp(-mt))
        mnew = mt[last:last + 1, :]
        blast = bcol[last:last + 1, :]
        wk = jnp.exp(blast - bcol + icol - mnew)
        if n_valid < CHUNK:
            wk = jnp.where(valid_rows, wk, 0.0)
        dc = jnp.exp(blast + mprev - mnew)
        c_s[h] = dc * ch + _dot_tn((vh * wk).astype(BF16), kb)
        n_s[h:h + 1, :] = dc * nh + jnp.sum(wk * kh, axis=0, keepdims=True)
        m_s[:, h:h + 1] = mnew
        out_ref[0, :, sl] = jax.nn.sigmoid(og[:, sl]) * _rms(hh, gn_ref[:, sl])

    @pl.when(c == pl.num_programs(1) - 1)
    def _():
        c_out[0] = c_s[...]
        n_out[0] = n_s[...]
        m_out[0] = m_s[...]


def _mlstm(zb, zg, bias, gn, c0, n0, m0, n_valid):
    B, T, _ = zb.shape
    nc = T // CHUNK
    blk = (1, CHUNK, MIX_W)
    zspec = lambda j: pl.BlockSpec(blk, lambda b, c: (b, c, j))
    st4 = pl.BlockSpec((1, N_HEADS, HEAD_DIM, HEAD_DIM), lambda b, c: (b, 0, 0, 0))
    st3 = pl.BlockSpec((1, N_HEADS, HEAD_DIM), lambda b, c: (b, 0, 0))
    st2 = pl.BlockSpec((1, 1, N_HEADS), lambda b, c: (b, 0, 0))
    return pl.pallas_call(
        functools.partial(_mlstm_body, n_valid=n_valid),
        grid=(B, nc),
        in_specs=[zspec(0), zspec(1), zspec(2), zspec(3),
                  pl.BlockSpec((1, CHUNK, GATE_W), lambda b, c: (b, c, 0)),
                  pl.BlockSpec((2, GATE_W), lambda b, c: (0, 0)),
                  pl.BlockSpec((1, MIX_W), lambda b, c: (0, 0)),
                  st4, st3, st2],
        out_specs=[pl.BlockSpec(blk, lambda b, c: (b, c, 0)), st4, st3, st2],
        out_shape=[jax.ShapeDtypeStruct((B, T, MIX_W), F32),
                   jax.ShapeDtypeStruct((B, N_HEADS, HEAD_DIM, HEAD_DIM), F32),
                   jax.ShapeDtypeStruct((B, N_HEADS, HEAD_DIM), F32),
                   jax.ShapeDtypeStruct((B, 1, N_HEADS), F32)],
        scratch_shapes=[pltpu.VMEM((N_HEADS, HEAD_DIM, HEAD_DIM), F32),
                        pltpu.VMEM((N_HEADS, HEAD_DIM), F32),
                        pltpu.VMEM((1, N_HEADS), F32)],
        compiler_params=_params("parallel", "arbitrary"),
        name="mlstm",
    )(zb, zb, zb, zb, zg, bias, gn, c0, n0, m0)


def _gate_body(zc_ref, w_ref, bs_ref, gcv_ref, out_ref, vrow_ref):
    z = zc_ref[0]
    u = jax.nn.gelu(z[:, :MIX_W])
    vn = _rms(jax.nn.gelu(z[:, MIX_W:]), gcv_ref[...])
    vrow_ref[0] = vn
    row = lax.broadcasted_iota(jnp.int32, (CHUNK, CHUNK), 0)
    col = lax.broadcasted_iota(jnp.int32, (CHUNK, CHUNK), 1)
    vb = vn.astype(BF16)
    for h in range(N_HEADS):
        sl = _head(h)
        w = jnp.where(row >= col, w_ref[h], 0.0).astype(BF16)
        s = _dot(w, vb[:, sl]) + bs_ref[:, h:h + 1]
        out_ref[0, :, sl] = u[:, sl] * s


def _gate(zc, w_s, bs_t, gcv):
    B, T, _ = zc.shape
    blk = (1, CHUNK, MIX_W)
    return pl.pallas_call(
        _gate_body,
        grid=(B, T // CHUNK),
        in_specs=[pl.BlockSpec((1, CHUNK, 2 * MIX_W), lambda b, c: (b, c, 0)),
                  pl.BlockSpec((N_HEADS, CHUNK, CHUNK), lambda b, c: (0, 0, 0)),
                  pl.BlockSpec((CHUNK, N_HEADS), lambda b, c: (0, 0)),
                  pl.BlockSpec((1, MIX_W), lambda b, c: (0, 0))],
        out_specs=[pl.BlockSpec(blk, lambda b, c: (b, c, 0))] * 2,
        out_shape=[jax.ShapeDtypeStruct((B, T, MIX_W), F32)] * 2,
        compiler_params=_params("parallel", "parallel"),
        name="gate",
    )(zc, w_s, bs_t, gcv)


def _hgrn_tables(n_valid):
    p = np.arange(CHUNK)[:, None]
    u = np.arange(CHUNK)[None, :]
    mats = []
    for l in range(N_LEVELS):
        m = CHUNK >> (l + 1)
        start = (p // m) * m
        odd = ((p // m) % 2) == 1
        mats.append(np.where(odd, (u >= start) & (u <= p), (u > p) & (u <= start + m - 1)))
    mats.append(u <= p)
    mats.append((u > p) & (u <= n_valid - 1))
    mall = np.concatenate(mats, axis=0).astype(np.float32)
    t = np.arange(CHUNK)[:, None]
    s = np.arange(CHUNK)[None, :]
    x = t ^ s
    top = np.floor(np.log2(np.maximum(x, 1))).astype(np.int32)
    lvl = np.where(s < t, N_LEVELS - 1 - top, np.where(s == t, N_LEVELS, N_LEVELS + 1)).astype(np.int32)
    return jnp.asarray(mall, BF16), jnp.asarray(lvl)


def _hgrn_body(zq_ref, zf_ref, zi_ref, zg_ref, lb_ref, gn_ref, mall_ref, lvl_ref, s0_ref,
               out_ref, s_out, s_s, *, n_valid):
    c = pl.program_id(1)

    @pl.when(c == 0)
    def _():
        s_s[...] = s0_ref[0]

    q = zq_ref[0]
    fx = zf_ref[0]
    v = zi_ref[0]
    lb = lb_ref[...]
    a = jnp.log(jnp.maximum(lb, LB_FLOOR))
    ct = jnp.log1p(-lb) + _log_sigmoid(fx)
    logf = jnp.maximum(a, ct) + jnp.log1p(jnp.exp(-jnp.abs(a - ct)))
    kd = (1.0 - lb) * jax.nn.sigmoid(-fx)
    gsum = _dot_01(mall_ref[...], logf)
    lvl = lvl_ref[...]
    amat = [jnp.zeros((CHUNK, CHUNK), F32) for _ in range(N_HEADS)]
    for l in range(N_LEVELS):
        e = jnp.exp(gsum[l * CHUNK:(l + 1) * CHUNK])
        qt = (q * e).astype(BF16)
        kt = (kd * e).astype(BF16)
        for h in range(N_HEADS):
            sl = _head(h)
            amat[h] = amat[h] + jnp.where(lvl == l, _dot_nt(qt[:, sl], kt[:, sl]), 0.0)
    qb = q.astype(BF16)
    kb = kd.astype(BF16)
    vb = v.astype(BF16)
    bcum = gsum[N_LEVELS * CHUNK:(N_LEVELS + 1) * CHUNK]
    q_in = (q * jnp.exp(bcum)).astype(BF16)
    valid_rows = lax.broadcasted_iota(jnp.int32, (CHUNK, 1), 0) < n_valid
    k_out = jnp.where(valid_rows, kd * jnp.exp(gsum[(N_LEVELS + 1) * CHUNK:]), 0.0).astype(BF16)
    e_last = jnp.exp(bcum[n_valid - 1:n_valid, :])
    og = zg_ref[0]
    for h in range(N_HEADS):
        sl = _head(h)
        ah = amat[h] + jnp.where(lvl == N_LEVELS, _dot_nt(qb[:, sl], kb[:, sl]), 0.0)
        sh = s_s[h]
        o = _dot(ah.astype(BF16), vb[:, sl]) + _dot_nt(q_in[:, sl], sh.astype(BF16))
        s_s[h] = e_last[:, sl] * sh + _dot_tn(vb[:, sl], k_out[:, sl])
        gate = og[:, sl]
        out_ref[0, :, sl] = _rms(o, gn_ref[:, sl]) * (gate * jax.nn.sigmoid(gate))

    @pl.when(c == pl.num_programs(1) - 1)
    def _():
        s_out[0] = s_s[...]


def _hgrn(zd, lb, gn, s0_t, n_valid):
    B, T, _ = zd.shape
    mall, lvl = _hgrn_tables(n_valid)
    blk = (1, CHUNK, MIX_W)
    zspec = lambda j: pl.BlockSpec(blk, lambda b, c: (b, c, j))
    st4 = pl.BlockSpec((1, N_HEADS, HEAD_DIM, HEAD_DIM), lambda b, c: (b, 0, 0, 0))
    vec = pl.BlockSpec((1, MIX_W), lambda b, c: (0, 0))
    return pl.pallas_call(
        functools.partial(_hgrn_body, n_valid=n_valid),
        grid=(B, T // CHUNK),
        in_specs=[zspec(0), zspec(1), zspec(2), zspec(3), vec, vec,
                  pl.BlockSpec(mall.shape, lambda b, c: (0, 0)),
                  pl.BlockSpec(lvl.shape, lambda b, c: (0, 0)),
                  st4],
        out_specs=[pl.BlockSpec(blk, lambda b, c: (b, c, 0)), st4],
        out_shape=[jax.ShapeDtypeStruct((B, T, MIX_W), F32),
                   jax.ShapeDtypeStruct((B, N_HEADS, HEAD_DIM, HEAD_DIM), F32)],
        scratch_shapes=[pltpu.VMEM((N_HEADS, HEAD_DIM, HEAD_DIM), F32)],
        compiler_params=_params("parallel", "arbitrary"),
        name="hgrn",
    )(zd, zd, zd, zd, lb, gn, mall, lvl, s0_t)


def _post_body(x_ref, oa_ref, ob_ref, oc_ref, od_ref, wout_ref, gm_ref, wup_ref, wdn_ref, gf_ref, y_ref,
               x1_s, h_s, acc_s, *, final):
    j = pl.program_id(1)

    @pl.when(j == 0)
    def _():
        x1 = x_ref[...]
        for i, o_ref in enumerate((oa_ref, ob_ref, oc_ref, od_ref)):
            x1 = x1 + _dot(o_ref[...].astype(BF16), wout_ref[i * MIX_W:(i + 1) * MIX_W, :])
        x1_s[...] = x1
        h_s[...] = _rms(x1, gm_ref[...]).astype(BF16)
        acc_s[...] = jnp.zeros_like(acc_s)

    up = jnp.maximum(_dot(h_s[...], wup_ref[...]), 0.0)
    acc_s[...] += _dot((up * up).astype(BF16), wdn_ref[...])

    @pl.when(j == pl.num_programs(1) - 1)
    def _():
        x2 = x1_s[...] + acc_s[...]
        y_ref[...] = _rms(x2, gf_ref[...]) if final else x2


def _post(x2d, oa, ob, oc, od, wout, gm, wup, wdn, gf, tm, tf, final):
    n = x2d.shape[0]
    row = lambda wd: pl.BlockSpec((tm, wd), lambda i, j: (i, 0))
    vec = pl.BlockSpec((1, D_MODEL), lambda i, j: (0, 0))
    return pl.pallas_call(
        functools.partial(_post_body, final=final),
        grid=(n // tm, D_FF // tf),
        in_specs=[row(D_MODEL)] + [row(MIX_W)] * 4
                 + [pl.BlockSpec((D_MODEL, D_MODEL), lambda i, j: (0, 0)), vec,
                    pl.BlockSpec((D_MODEL, tf), lambda i, j: (0, j)),
                    pl.BlockSpec((tf, D_MODEL), lambda i, j: (j, 0)), vec],
        out_specs=row(D_MODEL),
        out_shape=jax.ShapeDtypeStruct((n, D_MODEL), F32),
        scratch_shapes=[pltpu.VMEM((tm, D_MODEL), F32), pltpu.VMEM((tm, D_MODEL), BF16),
                        pltpu.VMEM((tm, D_MODEL), F32)],
        compiler_params=_params("parallel", "arbitrary"),
        name="post",
    )(x2d, oa, ob, oc, od, wout, gm, wup, wdn, gf)


def _rel_bucket(dist):
    max_exact = N_BUCKETS // 2
    d = jnp.maximum(dist, 1).astype(F32)
    large = max_exact + (jnp.log(d / max_exact) / math.log(MAX_WINDOW / max_exact)
                         * (N_BUCKETS - max_exact)).astype(jnp.int32)
    large = jnp.clip(large, max_exact, N_BUCKETS - 1)
    return jnp.where(dist < max_exact, dist, large)


def _pattern_bias(rel_bias, w, d):
    offs = jnp.arange(w // d + 1, dtype=jnp.int32) * d
    return rel_bias[_rel_bucket(offs)].T.astype(F32)


def _prompt_table(bias):
    cols = 2 * CHUNK
    u = jnp.concatenate([bias[:, ::-1], jnp.full((N_HEADS, cols - CHUNK), NEG, F32)], axis=1)
    return jnp.tile(u, (1, CHUNK))[:, :CHUNK * cols].reshape(N_HEADS, CHUNK, cols)


def _sample_tables(bias, d, n_new, past):
    comb = jnp.concatenate([bias[:, :, None], jnp.full((N_HEADS, CHUNK + 1, d - 1), NEG, F32)], axis=2)
    comb = comb.reshape(N_HEADS, (CHUNK + 1) * d)
    length = past + SAMPLE_ROWS + 1
    comb = comb[:, :length]
    comb = jnp.pad(comb, ((0, 0), (0, length - comb.shape[1])), constant_values=NEG)
    rev = comb[:, ::-1]
    tpast, tnew = [], []
    for row in range(SAMPLE_ROWS):
        t = min(row, n_new - 1)
        start = length - 1 - past - t
        tpast.append(rev[:, start:start + past])
        start = length - 1 - t
        tnew.append(jnp.pad(rev[:, start:start + t + 1], ((0, 0), (0, CHUNK - t - 1)), constant_values=NEG))
    return jnp.stack(tpast, axis=1), jnp.stack(tnew, axis=1)


def _pad_rows(a, rows):
    return jnp.pad(a, ((0, 0), (0, rows - a.shape[1]), (0, 0)))


def kernel(x_prompt, x_sample, cache_k_win, cache_v_win, state_mlstm_C, state_mlstm_n, state_mlstm_m, state_hgrn_S, rel_bias, w_in, w_out, g_attn, g_mlp, w_up, w_down, b_i, b_f, g_mlstm, g_cv, w_s, b_s, hgrn_lb, g_hgrn, g_final):
    depth = w_in.shape[0]
    B, T, _ = x_prompt.shape
    Bs, Ts, _ = x_sample.shape
    past = cache_k_win.shape[2]
    assert past == MAX_WINDOW and T % (MAX_WINDOW) == 0 and Ts <= SAMPLE_ROWS
    H, dh = N_HEADS, HEAD_DIM
    keep_p = min(MAX_WINDOW, T)

    sm = jax.nn.softmax(hgrn_lb.astype(F32), axis=0)
    lb_all = jnp.cumsum(sm, axis=0) - sm[0:1]
    biases = [_pattern_bias(rel_bias, w, d) for w, d in DILATIONS]
    tabs_p = jnp.stack([_prompt_table(bb) for bb in biases])
    tabs_s = [_sample_tables(bb, d, Ts, past) for bb, (_, d) in zip(biases, DILATIONS)]
    tpast = jnp.stack([a for a, _ in tabs_s])
    tnew = jnp.stack([b for _, b in tabs_s])
    kt_all = jnp.transpose(cache_k_win, (0, 1, 3, 4, 2))
    vt_all = jnp.transpose(cache_v_win, (0, 1, 3, 4, 2))

    a_end = Z_WIDTHS[0] + Z_WIDTHS[1]
    g_end = a_end + N_GATE_COLS
    zeros_c = jnp.zeros((B, H, dh, dh), F32)
    zeros_n = jnp.zeros((B, H, dh), F32)
    zeros_m = jnp.zeros((B, 1, H), F32)
    vec = lambda a: a.reshape(1, -1).astype(F32)

    xp = x_prompt.reshape(B * T, D_MODEL)
    xs = x_sample.reshape(Bs * Ts, D_MODEL)
    outs = [[] for _ in range(13)]
    for l in range(depth):
        wl = w_in[l]
        w_gate = jnp.pad(wl[:, a_end:g_end], ((0, 0), (0, GATE_W - N_GATE_COLS)))
        w_z = jnp.concatenate([wl[:, :a_end], w_gate, wl[:, g_end:]], axis=1).astype(BF16)
        wg_lo = (w_gate - w_gate.astype(BF16).astype(F32)).astype(BF16)
        wout_b, wup_b, wdn_b = w_out[l].astype(BF16), w_up[l].astype(BF16), w_down[l].astype(BF16)
        gate_bias = jnp.zeros((2, GATE_W), F32).at[0, :H].set(b_i[l]).at[1, H:2 * H].set(b_f[l])
        bs_t = b_s[l].T.astype(F32)
        final = l == depth - 1
        gf = vec(g_final)

        za, zb, zg, zc, zd = _inproj(xp, vec(g_attn[l]), w_z, wg_lo, 512)
        za3 = za.reshape(B, T, -1)
        oa = _attn_prompt(za3, tabs_p)
        ob, c1, n1, m1 = _mlstm(zb.reshape(B, T, -1), zg.reshape(B, T, -1), gate_bias, vec(g_mlstm[l]),
                                zeros_c, zeros_n, zeros_m, CHUNK)
        oc, _ = _gate(zc.reshape(B, T, -1), w_s[l], bs_t, vec(g_cv[l]))
        od, s1 = _hgrn(zd.reshape(B, T, -1), vec(lb_all[l]), vec(g_hgrn[l]), zeros_c, CHUNK)
        xp = _post(xp, oa.reshape(B * T, -1), ob.reshape(B * T, -1), oc.reshape(B * T, -1), od.reshape(B * T, -1),
                   wout_b, vec(g_mlp[l]), wup_b, wdn_b, gf, 512, 1024, final)
        outs[0].append(za3[:, T - keep_p:, MIX_W:2 * MIX_W].reshape(B, keep_p, H, dh))
        outs[1].append(za3[:, T - keep_p:, 2 * MIX_W:].reshape(B, keep_p, H, dh))
        outs[4].append(c1)
        outs[5].append(n1)
        outs[6].append(m1.reshape(B, H))
        outs[10].append(jnp.swapaxes(s1, -1, -2))

        za, zb, zg, zc, zd = _inproj(xs, vec(g_attn[l]), w_z, wg_lo, Bs * Ts)
        za3 = za.reshape(Bs, Ts, -1)
        new_t = lambda a: jnp.pad(jnp.transpose(a.reshape(Bs, Ts, H, dh), (0, 2, 3, 1)),
                                  ((0, 0), (0, 0), (0, 0), (0, LANES - Ts)))
        oa, k_win, v_win = _attn_sample(l, _pad_rows(za3[:, :, :MIX_W], SAMPLE_ROWS), kt_all, vt_all,
                                        new_t(za3[:, :, MIX_W:2 * MIX_W]), new_t(za3[:, :, 2 * MIX_W:]),
                                        tpast, tnew, Ts)
        pad3 = lambda z: _pad_rows(z.reshape(Bs, Ts, -1), CHUNK)
        ob, c2, n2, m2 = _mlstm(pad3(zb), pad3(zg), gate_bias, vec(g_mlstm[l]),
                                state_mlstm_C[l], state_mlstm_n[l], state_mlstm_m[l].reshape(Bs, 1, H), Ts)
        oc, vrows = _gate(pad3(zc), w_s[l], bs_t, vec(g_cv[l]))
        od, s2 = _hgrn(pad3(zd), vec(lb_all[l]), vec(g_hgrn[l]), jnp.swapaxes(state_hgrn_S[l], -1, -2), Ts)
        flat = lambda o: o[:, :Ts].reshape(Bs * Ts, MIX_W)
        xs = _post(xs, flat(oa), flat(ob), flat(oc), flat(od),
                   wout_b, vec(g_mlp[l]), wup_b, wdn_b, gf, Bs * Ts, 1024, final)
        outs[2].append(jnp.transpose(k_win, (0, 3, 1, 2)))
        outs[3].append(jnp.transpose(v_win, (0, 3, 1, 2)))
        outs[7].append(c2)
        outs[8].append(n2)
        outs[9].append(m2.reshape(Bs, H))
        outs[11].append(jnp.swapaxes(s2, -1, -2))
        outs[12].append(vrows[:, :Ts].reshape(Bs, Ts, H, dh))

    return (xp.reshape(B, T, D_MODEL), xs.reshape(Bs, Ts, D_MODEL)) + tuple(jnp.stack(o) for o in outs)
```

```python
import functools
import math

import numpy as np
import jax
import jax.numpy as jnp
from jax import lax
from jax.experimental import pallas as pl
from jax.experimental.pallas import tpu as pltpu

F32 = jnp.float32
BF16 = jnp.bfloat16

D_MODEL = 1024
N_HEADS = 4
HEAD_DIM = 64
MIX_W = N_HEADS * HEAD_DIM
DILATIONS = ((128, 1), (512, 4), (2048, 16))
MAX_WINDOW = 2048
N_BUCKETS = 32
D_FF = 4 * D_MODEL
EPS = 1e-6
NEG = -1e30
LB_FLOOR = 1e-30
CHUNK = 128
N_GATE_COLS = 2 * N_HEADS
GATE_W = 128
Z_WIDTHS = (3 * MIX_W, 4 * MIX_W, GATE_W, 2 * MIX_W, 4 * MIX_W)
Z_TOTAL = sum(Z_WIDTHS)
ATTN_SCALE = HEAD_DIM ** -0.5
N_LEVELS = 7
VMEM_LIMIT = 48 * 1024 * 1024
PROMPT_SEQS_PER_STEP = 2
SAMPLE_SEQS_PER_STEP = 4
GATE_ROWS = 4 * CHUNK


def _dot(a, b):
    return jnp.dot(a, b, preferred_element_type=F32)


def _dot_nt(a, b):
    return lax.dot_general(a, b, (((1,), (1,)), ((), ())), preferred_element_type=F32)


def _dot_tn(a, b):
    return lax.dot_general(a, b, (((0,), (0,)), ((), ())), preferred_element_type=F32)


def _rms(x, g):
    return x * lax.rsqrt(jnp.mean(x * x, axis=-1, keepdims=True) + EPS) * g


def _log_sigmoid(x):
    return jnp.minimum(x, 0.0) - jnp.log1p(jnp.exp(-jnp.abs(x)))


def _dot_01(m01, f, pieces=3):
    out = None
    for _ in range(pieces):
        piece = f.astype(BF16)
        f = f - piece.astype(F32)
        out = _dot(m01, piece) if out is None else out + _dot(m01, piece)
    return out


def _head(h):
    return slice(h * HEAD_DIM, (h + 1) * HEAD_DIM)


def _head_scores(x, y):
    rows, width = x.shape
    tile = 2 * HEAD_DIM
    low = lax.broadcasted_iota(jnp.int32, (rows, tile), 1) < HEAD_DIM
    out = []
    for half in range(width // tile):
        lanes = slice(half * tile, (half + 1) * tile)
        xh = x[:, lanes]
        x2 = jnp.concatenate([jnp.where(low, xh, 0.0), jnp.where(low, 0.0, xh)], axis=0).astype(BF16)
        out.append(_dot_nt(x2, y[:, lanes].astype(BF16)))
    return jnp.concatenate(out, axis=0)


def _params(*sem):
    return pltpu.CompilerParams(dimension_semantics=sem, vmem_limit_bytes=VMEM_LIMIT)


def _inproj_body(x_ref, g_ref, w_ref, wg_lo_ref, *out_refs):
    hn = _rms(x_ref[...], g_ref[...])
    h = hn.astype(BF16)
    off = 0
    for o_ref in out_refs:
        n = o_ref.shape[-1]
        z = _dot(h, w_ref[:, off:off + n])
        if n == GATE_W:
            h_lo = (hn - h.astype(F32)).astype(BF16)
            z = z + _dot(h, wg_lo_ref[...]) + _dot(h_lo, w_ref[:, off:off + n])
        o_ref[...] = z
        off += n


def _inproj(x2d, g, w, wg_lo, tm):
    n = x2d.shape[0]
    return pl.pallas_call(
        _inproj_body,
        grid=(n // tm,),
        in_specs=[pl.BlockSpec((tm, D_MODEL), lambda i: (i, 0)),
                  pl.BlockSpec((1, D_MODEL), lambda i: (0, 0)),
                  pl.BlockSpec((D_MODEL, Z_TOTAL), lambda i: (0, 0)),
                  pl.BlockSpec((D_MODEL, GATE_W), lambda i: (0, 0))],
        out_specs=[pl.BlockSpec((tm, wd), lambda i: (i, 0)) for wd in Z_WIDTHS],
        out_shape=[jax.ShapeDtypeStruct((n, wd), F32) for wd in Z_WIDTHS],
        compiler_params=_params("parallel"),
        name="inproj",
    )(x2d, g, w, wg_lo)


ATTN_ROWS = MAX_WINDOW
LANES = 128
UNITS_PER_ITER = 4


def _attn_body(q0, q1, kp0, kp1, kc0, kc1, vp0, vp1, vc0, vc1, tab_ref, o_ref, os_ref, ls_ref):
    first = pl.program_id(1) == 0
    col = lax.broadcasted_iota(jnp.int32, (N_HEADS * CHUNK, 2 * CHUNK), 1)
    low_head = lax.broadcasted_iota(jnp.int32, (CHUNK, LANES), 1) < HEAD_DIM
    ones = jnp.ones((2 * CHUNK, LANES), BF16)
    q_refs, kp_refs, kc_refs, vp_refs, vc_refs = (q0, q1), (kp0, kp1), (kc0, kc1), (vp0, vp1), (vc0, vc1)

    def rows_of(start, d):
        if d == 1:
            return pl.ds(start if isinstance(start, int) else pl.multiple_of(start, CHUNK), CHUNK)
        return pl.ds(start, CHUNK, stride=d)

    def unit(p, d, r, c, prev_in_block):
        start = r + d * CHUNK * c
        cur = rows_of(start, d)
        prev = rows_of(start - d * CHUNK, d) if prev_in_block else rows_of(r + ATTN_ROWS - d * CHUNK, d)
        k_prev_ref, v_prev_ref = (kc_refs, vc_refs) if prev_in_block else (kp_refs, vp_refs)
        scores, vws = [], []
        for half in range(2):
            q = q_refs[half][0, cur, :] * ATTN_SCALE
            kw = jnp.concatenate([k_prev_ref[half][0, prev, :], kc_refs[half][0, cur, :]], axis=0).astype(BF16)
            vws.append(jnp.concatenate([v_prev_ref[half][0, prev, :], vc_refs[half][0, cur, :]], axis=0).astype(BF16))
            qm = jnp.concatenate([jnp.where(low_head, q, 0.0), jnp.where(low_head, 0.0, q)], axis=0).astype(BF16)
            scores.append(_dot_nt(qm, kw))
        s = jnp.concatenate(scores, axis=0) + tab_ref[p]
        if not prev_in_block:
            s = jnp.where(first & (col < CHUNK), NEG, s)
        m = jnp.max(jnp.maximum(s[:, :CHUNK], s[:, CHUNK:]), axis=1, keepdims=True)
        e = jnp.exp(s - m).astype(BF16)
        den = _dot(e, ones)
        lse = m + jnp.log(den)
        for half in range(2):
            rows = slice(2 * half * CHUNK, (2 * half + 1) * CHUNK)
            rows2 = slice((2 * half + 1) * CHUNK, (2 * half + 2) * CHUNK)
            o = _dot(e[2 * half * CHUNK:(2 * half + 2) * CHUNK], vws[half])
            o = o / den[2 * half * CHUNK:(2 * half + 2) * CHUNK]
            os_ref[p, half, cur, :] = jnp.where(low_head, o[:CHUNK], o[CHUNK:])
            ls_ref[p, half, cur, :] = jnp.where(low_head, lse[rows], lse[rows2])

    for p, (_, d) in enumerate(DILATIONS):
        nblk = ATTN_ROWS // d // CHUNK
        if nblk >= UNITS_PER_ITER:
            def per_subsequence(r, carry, p=p, d=d, nblk=nblk):
                for c in range(UNITS_PER_ITER):
                    unit(p, d, r, c, c > 0)

                def group(g, carry2):
                    for u in range(UNITS_PER_ITER):
                        unit(p, d, r, g * UNITS_PER_ITER + u, True)
                    return carry2
                return lax.fori_loop(1, nblk // UNITS_PER_ITER, group, carry)
            if d == 1:
                per_subsequence(0, 0)
            else:
                lax.fori_loop(0, d, per_subsequence, 0)
        else:
            def group(g, carry, p=p, d=d):
                for u in range(UNITS_PER_ITER):
                    unit(p, d, g * UNITS_PER_ITER + u, 0, False)
                return carry
            lax.fori_loop(0, d // UNITS_PER_ITER, group, 0)

    def merge(i, carry):
        rows = pl.ds(pl.multiple_of(i * CHUNK, CHUNK), CHUNK)
        for half in range(2):
            ls = [ls_ref[p, half, rows, :] for p in range(len(DILATIONS))]
            mx = jnp.maximum(jnp.maximum(ls[0], ls[1]), ls[2])
            es = [jnp.exp(l - mx) for l in ls]
            num = es[0] * os_ref[0, half, rows, :] + es[1] * os_ref[1, half, rows, :] + es[2] * os_ref[2, half, rows, :]
            o_ref[0, rows, half * LANES:(half + 1) * LANES] = num / (es[0] + es[1] + es[2])
        return carry
    lax.fori_loop(0, ATTN_ROWS // CHUNK, merge, 0)


def _attn_prompt(za, tabs):
    B, T, _ = za.shape
    blk = (1, ATTN_ROWS, LANES)
    cur = lambda j: pl.BlockSpec(blk, lambda b, i: (b, i, j))
    prv = lambda j: pl.BlockSpec(blk, lambda b, i: (b, jnp.maximum(i - 1, 0), j))
    npat = len(DILATIONS)
    return pl.pallas_call(
        _attn_body,
        grid=(B, T // ATTN_ROWS),
        in_specs=[cur(0), cur(1), prv(2), prv(3), cur(2), cur(3), prv(4), prv(5), cur(4), cur(5),
                  pl.BlockSpec((npat, N_HEADS * CHUNK, 2 * CHUNK), lambda b, i: (0, 0, 0))],
        out_specs=pl.BlockSpec((1, ATTN_ROWS, MIX_W), lambda b, i: (b, i, 0)),
        out_shape=jax.ShapeDtypeStruct((B, T, MIX_W), F32),
        scratch_shapes=[pltpu.VMEM((npat, 2, ATTN_ROWS, LANES), F32)] * 2,
        compiler_params=_params("parallel", "arbitrary"),
        name="attn_prompt",
    )(*([za] * 10), tabs.reshape(npat, N_HEADS * CHUNK, 2 * CHUNK))


SAMPLE_ROWS = 8


def _attn_sample_body(q_ref, kt_ref, vt_ref, knt_ref, vnt_ref, tpast_ref, tnew_ref, o_ref, kw_ref, vw_ref, *, n_new):
    npat = len(DILATIONS)
    q8 = q_ref[0] * ATTN_SCALE
    for h in range(N_HEADS):
        sl = _head(h)
        qh = q8[:, sl].astype(BF16)
        kt, vt = kt_ref[0, 0, h], vt_ref[0, 0, h]
        knt, vnt = knt_ref[0, h], vnt_ref[0, h]
        sp = _dot(qh, kt.astype(BF16))
        sn = _dot(qh, knt.astype(BF16))
        pps, pns, dens, lses = [], [], [], []
        for p in range(npat):
            lp = sp + tpast_ref[p, h]
            ln = sn + tnew_ref[p, h]
            m = jnp.maximum(jnp.max(lp, axis=1, keepdims=True), jnp.max(ln, axis=1, keepdims=True))
            pps.append(jnp.exp(lp - m))
            pns.append(jnp.exp(ln - m))
            dens.append(jnp.sum(pps[-1], axis=1, keepdims=True) + jnp.sum(pns[-1], axis=1, keepdims=True))
            lses.append(m + jnp.log(dens[-1]))
        pv = (_dot_nt(jnp.concatenate(pps, axis=0).astype(BF16), vt.astype(BF16))
              + _dot_nt(jnp.concatenate(pns, axis=0).astype(BF16), vnt.astype(BF16)))
        mx = jnp.maximum(jnp.maximum(lses[0], lses[1]), lses[2])
        es = [jnp.exp(l - mx) for l in lses]
        num = sum(es[p] * pv[p * SAMPLE_ROWS:(p + 1) * SAMPLE_ROWS] / dens[p] for p in range(npat))
        o_ref[0, :, sl] = num / (es[0] + es[1] + es[2])
        kw_ref[0, h] = jnp.concatenate([kt[:, n_new:], knt[:, :n_new]], axis=1)
        vw_ref[0, h] = jnp.concatenate([vt[:, n_new:], vnt[:, :n_new]], axis=1)


def _attn_sample(layer, q8, kt_all, vt_all, knt, vnt, tpast, tnew, n_new):
    _, B, _, _, P = kt_all.shape
    npat = len(DILATIONS)
    cache = pl.BlockSpec((1, 1, N_HEADS, HEAD_DIM, P), lambda b: (layer, b, 0, 0, 0))
    new = pl.BlockSpec((1, N_HEADS, HEAD_DIM, LANES), lambda b: (b, 0, 0, 0))
    win = pl.BlockSpec((1, N_HEADS, HEAD_DIM, P), lambda b: (b, 0, 0, 0))
    rows = pl.BlockSpec((1, SAMPLE_ROWS, MIX_W), lambda b: (b, 0, 0))
    return pl.pallas_call(
        functools.partial(_attn_sample_body, n_new=n_new),
        grid=(B,),
        in_specs=[rows, cache, cache, new, new,
                  pl.BlockSpec((npat, N_HEADS, SAMPLE_ROWS, P), lambda b: (0, 0, 0, 0)),
                  pl.BlockSpec((npat, N_HEADS, SAMPLE_ROWS, LANES), lambda b: (0, 0, 0, 0))],
        out_specs=[rows, win, win],
        out_shape=[jax.ShapeDtypeStruct((B, SAMPLE_ROWS, MIX_W), F32),
                   jax.ShapeDtypeStruct((B, N_HEADS, HEAD_DIM, P), F32),
                   jax.ShapeDtypeStruct((B, N_HEADS, HEAD_DIM, P), F32)],
        compiler_params=_params("parallel"),
        name="attn_sample",
    )(q8, kt_all, vt_all, knt, vnt, tpast, tnew)


def _mlstm_body(zq_ref, zk_ref, zv_ref, zo_ref, zg_ref, bias_ref, gn_ref, c0_ref, n0_ref, m0_ref,
                out_ref, c_out, n_out, m_out, c_s, n_s, m_s, *, n_valid):
    c = pl.program_id(1)

    nseq = zq_ref.shape[0]

    @pl.when(c == 0)
    def _():
        c_s[...] = jnp.zeros_like(c_s)
        for i in range(nseq):
            for h in range(N_HEADS):
                c_s[i, _head(h), _head(h)] = c0_ref[i, h]
        n_s[...] = n0_ref[...]
        m_s[...] = m0_ref[...]

    rows4 = N_HEADS * CHUNK
    t4 = lax.broadcasted_iota(jnp.int32, (rows4, CHUNK), 0) & (CHUNK - 1)
    causal4 = t4 >= lax.broadcasted_iota(jnp.int32, (rows4, CHUNK), 1)
    valid4 = t4[:, 0:1] < n_valid
    tril = (lax.broadcasted_iota(jnp.int32, (CHUNK, CHUNK), 0)
            >= lax.broadcasted_iota(jnp.int32, (CHUNK, CHUNK), 1)).astype(BF16)
    lane_head = lax.broadcasted_iota(jnp.int32, (CHUNK, MIX_W), 1) // HEAD_DIM
    same_head = (lax.broadcasted_iota(jnp.int32, (MIX_W, MIX_W), 0) // HEAD_DIM
                 == lax.broadcasted_iota(jnp.int32, (MIX_W, MIX_W), 1) // HEAD_DIM)
    head_ones = same_head.astype(BF16)
    ones_l = jnp.ones((CHUNK, MIX_W), BF16)
    ones_w = jnp.ones((MIX_W, MIX_W), BF16)
    last = n_valid - 1

    def stack(f):
        return jnp.concatenate([f(h) for h in range(N_HEADS)], axis=0)

    def on_head_lanes(f, lanes):
        out = f(N_HEADS - 1)
        for h in range(N_HEADS - 2, -1, -1):
            out = jnp.where(lanes == h, f(h), out)
        return out

    for i in range(nseq):
        gates = zg_ref[i]
        ipre = gates + bias_ref[0:1, :]
        logf = _log_sigmoid(gates + bias_ref[1:2, :])
        bcum = _dot_01(tril, logf)
        bcum_t = bcum.T
        ipre_t = ipre.T
        q = zq_ref[i]
        k = zk_ref[i] * ATTN_SCALE
        v = zv_ref[i]
        kb, vb = k.astype(BF16), v.astype(BF16)
        cbd = c_s[i]
        nrow = n_s[i]
        mprev = m_s[i]
        bcol = stack(lambda h: bcum[:, N_HEADS + h:N_HEADS + h + 1])
        icol = stack(lambda h: ipre[:, h:h + 1])
        brow = stack(lambda h: jnp.broadcast_to(bcum_t[N_HEADS + h:N_HEADS + h + 1, :], (CHUNK, CHUNK)))
        irow = stack(lambda h: jnp.broadcast_to(ipre_t[h:h + 1, :], (CHUNK, CHUNK)))
        mprev4 = stack(lambda h: jnp.broadcast_to(mprev[:, h:h + 1], (CHUNK, 1)))
        dmat = jnp.where(causal4, bcol - brow + irow, NEG)
        g = bcol + mprev4
        mt = jnp.maximum(g, jnp.max(dmat, axis=1, keepdims=True))
        dexp = jnp.exp(dmat - mt)
        gexp = jnp.exp(g - mt)
        qm_f = stack(lambda h: jnp.where(lane_head == h, q, 0.0))
        qm = qm_f.astype(BF16)
        sb = (_head_scores(q, k) * dexp).astype(BF16)
        num = _dot(sb, vb) + gexp * _dot_nt(qm, cbd.astype(BF16))
        nq = _dot(sb, ones_l) + gexp * _dot((qm_f * nrow).astype(BF16), ones_w)
        hh4 = num / jnp.maximum(jnp.abs(nq), jnp.exp(-mt))
        hh = on_head_lanes(lambda h: hh4[h * CHUNK:(h + 1) * CHUNK], lane_head)
        sq = hh * hh
        sq_hi = sq.astype(BF16)
        sq_lo = (sq - sq_hi.astype(F32)).astype(BF16)
        ms = (_dot(sq_hi, head_ones) + _dot(sq_lo, head_ones)) * (1.0 / HEAD_DIM)
        out_ref[i] = jax.nn.sigmoid(zo_ref[i]) * (hh * lax.rsqrt(ms + EPS) * gn_ref[...])

        mnew = [mt[h * CHUNK + last:h * CHUNK + last + 1, :] for h in range(N_HEADS)]
        blast = [bcol[h * CHUNK + last:h * CHUNK + last + 1, :] for h in range(N_HEADS)]
        wk = jnp.exp(stack(lambda h: jnp.broadcast_to(blast[h] - mnew[h], (CHUNK, 1))) - bcol + icol)
        if n_valid < CHUNK:
            wk = jnp.where(valid4, wk, 0.0)
        wk_w = on_head_lanes(lambda h: jnp.broadcast_to(wk[h * CHUNK:(h + 1) * CHUNK], (CHUNK, MIX_W)), lane_head)
        dc = on_head_lanes(lambda h: jnp.broadcast_to(jnp.exp(blast[h] + mprev[:, h:h + 1] - mnew[h]), (1, MIX_W)),
                           lane_head[0:1])
        c_s[i] = dc * cbd + jnp.where(same_head, _dot_tn((v * wk_w).astype(BF16), kb), 0.0)
        n_s[i] = dc * nrow + jnp.sum(wk_w * k, axis=0, keepdims=True)
        for h in range(N_HEADS):
            m_s[i, :, h:h + 1] = mnew[h]

    @pl.when(c == pl.num_programs(1) - 1)
    def _():
        for i in range(nseq):
            for h in range(N_HEADS):
                c_out[i, h] = c_s[i, _head(h), _head(h)]
        n_out[...] = n_s[...]
        m_out[...] = m_s[...]


def _mlstm(zb, zg, bias, gn, c0, n0, m0, n_valid, nseq):
    B, T, _ = zb.shape
    nc = T // CHUNK
    blk = (nseq, CHUNK, MIX_W)
    zspec = lambda j: pl.BlockSpec(blk, lambda b, c: (b, c, j))
    st4 = pl.BlockSpec((nseq, N_HEADS, HEAD_DIM, HEAD_DIM), lambda b, c: (b, 0, 0, 0))
    st3 = pl.BlockSpec((nseq, 1, MIX_W), lambda b, c: (b, 0, 0))
    st2 = pl.BlockSpec((nseq, 1, N_HEADS), lambda b, c: (b, 0, 0))
    return pl.pallas_call(
        functools.partial(_mlstm_body, n_valid=n_valid),
        grid=(B // nseq, nc),
        in_specs=[zspec(0), zspec(1), zspec(2), zspec(3),
                  pl.BlockSpec((nseq, CHUNK, GATE_W), lambda b, c: (b, c, 0)),
                  pl.BlockSpec((2, GATE_W), lambda b, c: (0, 0)),
                  pl.BlockSpec((1, MIX_W), lambda b, c: (0, 0)),
                  st4, st3, st2],
        out_specs=[pl.BlockSpec(blk, lambda b, c: (b, c, 0)), st4, st3, st2],
        out_shape=[jax.ShapeDtypeStruct((B, T, MIX_W), F32),
                   jax.ShapeDtypeStruct((B, N_HEADS, HEAD_DIM, HEAD_DIM), F32),
                   jax.ShapeDtypeStruct((B, 1, MIX_W), F32),
                   jax.ShapeDtypeStruct((B, 1, N_HEADS), F32)],
        scratch_shapes=[pltpu.VMEM((nseq, MIX_W, MIX_W), F32),
                        pltpu.VMEM((nseq, 1, MIX_W), F32),
                        pltpu.VMEM((nseq, 1, N_HEADS), F32)],
        compiler_params=_params("parallel", "arbitrary"),
        name="mlstm",
    )(zb, zb, zb, zb, zg, bias, gn, c0, n0, m0)


def _gate_body(zc_ref, w_ref, bs_ref, gcv_ref, out_ref, *maybe_vrow_ref):
    row = lax.broadcasted_iota(jnp.int32, (CHUNK, CHUNK), 0)
    col = lax.broadcasted_iota(jnp.int32, (CHUNK, CHUNK), 1)
    ws = [jnp.where(row >= col, w_ref[h], 0.0).astype(BF16) for h in range(N_HEADS)]
    for i in range(zc_ref.shape[0]):
        for j in range(zc_ref.shape[1] // CHUNK):
            rows = slice(j * CHUNK, (j + 1) * CHUNK)
            z = zc_ref[i, rows, :]
            u = jax.nn.gelu(z[:, :MIX_W])
            vn = _rms(jax.nn.gelu(z[:, MIX_W:]), gcv_ref[...])
            if maybe_vrow_ref:
                maybe_vrow_ref[0][i, rows, :] = vn
            vb = vn.astype(BF16)
            for h in range(N_HEADS):
                sl = _head(h)
                s = _dot(ws[h], vb[:, sl]) + bs_ref[:, h:h + 1]
                out_ref[i, rows, sl] = u[:, sl] * s


def _gate(zc, w_s, bs_t, gcv, nseq, rows, want_vrows):
    B, T, _ = zc.shape
    blk = (nseq, rows, MIX_W)
    n_out = 2 if want_vrows else 1
    return pl.pallas_call(
        _gate_body,
        grid=(B // nseq, T // rows),
        in_specs=[pl.BlockSpec((nseq, rows, 2 * MIX_W), lambda b, c: (b, c, 0)),
                  pl.BlockSpec((N_HEADS, CHUNK, CHUNK), lambda b, c: (0, 0, 0)),
                  pl.BlockSpec((CHUNK, N_HEADS), lambda b, c: (0, 0)),
                  pl.BlockSpec((1, MIX_W), lambda b, c: (0, 0))],
        out_specs=[pl.BlockSpec(blk, lambda b, c: (b, c, 0))] * n_out,
        out_shape=[jax.ShapeDtypeStruct((B, T, MIX_W), F32)] * n_out,
        compiler_params=_params("parallel", "parallel"),
        name="gate",
    )(zc, w_s, bs_t, gcv)


def _hgrn_tables(n_valid):
    p = np.arange(CHUNK)[:, None]
    u = np.arange(CHUNK)[None, :]
    mats = []
    for l in range(N_LEVELS):
        m = CHUNK >> (l + 1)
        start = (p // m) * m
        odd = ((p // m) % 2) == 1
        mats.append(np.where(odd, (u >= start) & (u <= p), (u > p) & (u <= start + m - 1)))
    mats.append(u <= p)
    mats.append((u > p) & (u <= n_valid - 1))
    mall = np.concatenate(mats, axis=0).astype(np.float32)
    t = np.arange(CHUNK)[:, None]
    s = np.arange(CHUNK)[None, :]
    x = t ^ s
    top = np.floor(np.log2(np.maximum(x, 1))).astype(np.int32)
    lvl = np.where(s < t, N_LEVELS - 1 - top, np.where(s == t, N_LEVELS, N_LEVELS + 1)).astype(np.int32)
    return jnp.asarray(mall, BF16), jnp.asarray(np.tile(lvl, (N_HEADS, 1)))


def _hgrn_body(zq_ref, zf_ref, zi_ref, zg_ref, lb_ref, gn_ref, mall_ref, lvl_ref, s0_ref,
               out_ref, s_out, s_s, *, n_valid):
    c = pl.program_id(1)

    nseq = zq_ref.shape[0]

    @pl.when(c == 0)
    def _():
        s_s[...] = jnp.zeros_like(s_s)
        for i in range(nseq):
            for h in range(N_HEADS):
                s_s[i, _head(h), _head(h)] = s0_ref[i, h]

    lb = lb_ref[...]
    a = jnp.log(jnp.maximum(lb, LB_FLOOR))
    lvl = lvl_ref[...]
    valid_rows = lax.broadcasted_iota(jnp.int32, (CHUNK, 1), 0) < n_valid
    lane_head = lax.broadcasted_iota(jnp.int32, (CHUNK, MIX_W), 1) // HEAD_DIM
    same_head = (lax.broadcasted_iota(jnp.int32, (MIX_W, MIX_W), 0) // HEAD_DIM
                 == lax.broadcasted_iota(jnp.int32, (MIX_W, MIX_W), 1) // HEAD_DIM)
    head_ones = same_head.astype(BF16)

    def by_head(x):
        return jnp.concatenate([jnp.where(lane_head == h, x, 0.0) for h in range(N_HEADS)], axis=0).astype(BF16)

    for i in range(nseq):
        q = zq_ref[i]
        fx = zf_ref[i]
        ct = jnp.log1p(-lb) + _log_sigmoid(fx)
        logf = jnp.maximum(a, ct) + jnp.log1p(jnp.exp(-jnp.abs(a - ct)))
        kd = (1.0 - lb) * jax.nn.sigmoid(-fx)
        gsum = _dot_01(mall_ref[...], logf, pieces=2)
        amat = None
        for l in range(N_LEVELS):
            e = jnp.exp(gsum[l * CHUNK:(l + 1) * CHUNK])
            amat = jnp.where(lvl == l, _head_scores(q * e, kd * e), 0.0 if l == 0 else amat)
        amat = jnp.where(lvl == N_LEVELS, _head_scores(q, kd), amat)
        vb = zi_ref[i].astype(BF16)
        bcum = gsum[N_LEVELS * CHUNK:(N_LEVELS + 1) * CHUNK]
        k_out = jnp.where(valid_rows, kd * jnp.exp(gsum[(N_LEVELS + 1) * CHUNK:]), 0.0).astype(BF16)
        e_last = jnp.exp(bcum[n_valid - 1:n_valid, :])
        sbd = s_s[i]
        o4 = _dot(amat.astype(BF16), vb) + _dot_nt(by_head(q * jnp.exp(bcum)), sbd.astype(BF16))
        o = o4[(N_HEADS - 1) * CHUNK:]
        for h in range(N_HEADS - 2, -1, -1):
            o = jnp.where(lane_head == h, o4[h * CHUNK:(h + 1) * CHUNK], o)
        sq = o * o
        sq_hi = sq.astype(BF16)
        sq_lo = (sq - sq_hi.astype(F32)).astype(BF16)
        ms = (_dot(sq_hi, head_ones) + _dot(sq_lo, head_ones)) * (1.0 / HEAD_DIM)
        gate = zg_ref[i]
        out_ref[i] = o * lax.rsqrt(ms + EPS) * gn_ref[...] * (gate * jax.nn.sigmoid(gate))
        s_s[i] = e_last * sbd + jnp.where(same_head, _dot_tn(vb, k_out), 0.0)

    @pl.when(c == pl.num_programs(1) - 1)
    def _():
        for i in range(nseq):
            for h in range(N_HEADS):
                s_out[i, h] = s_s[i, _head(h), _head(h)]


def _hgrn(zd, lb, gn, s0_t, n_valid, nseq):
    B, T, _ = zd.shape
    mall, lvl = _hgrn_tables(n_valid)
    blk = (nseq, CHUNK, MIX_W)
    zspec = lambda j: pl.BlockSpec(blk, lambda b, c: (b, c, j))
    st4 = pl.BlockSpec((nseq, N_HEADS, HEAD_DIM, HEAD_DIM), lambda b, c: (b, 0, 0, 0))
    vec = pl.BlockSpec((1, MIX_W), lambda b, c: (0, 0))
    return pl.pallas_call(
        functools.partial(_hgrn_body, n_valid=n_valid),
        grid=(B // nseq, T // CHUNK),
        in_specs=[zspec(0), zspec(1), zspec(2), zspec(3), vec, vec,
                  pl.BlockSpec(mall.shape, lambda b, c: (0, 0)),
                  pl.BlockSpec(lvl.shape, lambda b, c: (0, 0)),
                  st4],
        out_specs=[pl.BlockSpec(blk, lambda b, c: (b, c, 0)), st4],
        out_shape=[jax.ShapeDtypeStruct((B, T, MIX_W), F32),
                   jax.ShapeDtypeStruct((B, N_HEADS, HEAD_DIM, HEAD_DIM), F32)],
        scratch_shapes=[pltpu.VMEM((nseq, MIX_W, MIX_W), F32)],
        compiler_params=_params("parallel", "arbitrary"),
        name="hgrn",
    )(zd, zd, zd, zd, lb, gn, mall, lvl, s0_t)


def _post_body(x_ref, oa_ref, ob_ref, oc_ref, od_ref, wout_ref, gm_ref, wup_ref, wdn_ref, gf_ref, y_ref,
               x1_s, h_s, acc_s, *, final):
    j = pl.program_id(1)

    @pl.when(j == 0)
    def _():
        x1 = x_ref[...]
        for i, o_ref in enumerate((oa_ref, ob_ref, oc_ref, od_ref)):
            x1 = x1 + _dot(o_ref[...].astype(BF16), wout_ref[i * MIX_W:(i + 1) * MIX_W, :])
        x1_s[...] = x1
        h_s[...] = _rms(x1, gm_ref[...]).astype(BF16)
        acc_s[...] = jnp.zeros_like(acc_s)

    up = jnp.maximum(_dot(h_s[...], wup_ref[...]), 0.0)
    acc_s[...] += _dot((up * up).astype(BF16), wdn_ref[...])

    @pl.when(j == pl.num_programs(1) - 1)
    def _():
        x2 = x1_s[...] + acc_s[...]
        y_ref[...] = _rms(x2, gf_ref[...]) if final else x2


def _post(x2d, oa, ob, oc, od, wout, gm, wup, wdn, gf, tm, tf, final):
    n = x2d.shape[0]
    row = lambda wd: pl.BlockSpec((tm, wd), lambda i, j: (i, 0))
    vec = pl.BlockSpec((1, D_MODEL), lambda i, j: (0, 0))
    return pl.pallas_call(
        functools.partial(_post_body, final=final),
        grid=(n // tm, D_FF // tf),
        in_specs=[row(D_MODEL)] + [row(MIX_W)] * 4
                 + [pl.BlockSpec((D_MODEL, D_MODEL), lambda i, j: (0, 0)), vec,
                    pl.BlockSpec((D_MODEL, tf), lambda i, j: (0, j)),
                    pl.BlockSpec((tf, D_MODEL), lambda i, j: (j, 0)), vec],
        out_specs=row(D_MODEL),
        out_shape=jax.ShapeDtypeStruct((n, D_MODEL), F32),
        scratch_shapes=[pltpu.VMEM((tm, D_MODEL), F32), pltpu.VMEM((tm, D_MODEL), BF16),
                        pltpu.VMEM((tm, D_MODEL), F32)],
        compiler_params=_params("parallel", "arbitrary"),
        name="post",
    )(x2d, oa, ob, oc, od, wout, gm, wup, wdn, gf)


def _rel_bucket(dist):
    max_exact = N_BUCKETS // 2
    d = jnp.maximum(dist, 1).astype(F32)
    large = max_exact + (jnp.log(d / max_exact) / math.log(MAX_WINDOW / max_exact)
                         * (N_BUCKETS - max_exact)).astype(jnp.int32)
    large = jnp.clip(large, max_exact, N_BUCKETS - 1)
    return jnp.where(dist < max_exact, dist, large)


def _pattern_bias(rel_bias, w, d):
    offs = jnp.arange(w // d + 1, dtype=jnp.int32) * d
    return rel_bias[_rel_bucket(offs)].T.astype(F32)


def _prompt_table(bias):
    cols = 2 * CHUNK
    u = jnp.concatenate([bias[:, ::-1], jnp.full((N_HEADS, cols - CHUNK), NEG, F32)], axis=1)
    return jnp.tile(u, (1, CHUNK))[:, :CHUNK * cols].reshape(N_HEADS, CHUNK, cols)


def _sample_tables(bias, d, n_new, past):
    comb = jnp.concatenate([bias[:, :, None], jnp.full((N_HEADS, CHUNK + 1, d - 1), NEG, F32)], axis=2)
    comb = comb.reshape(N_HEADS, (CHUNK + 1) * d)
    length = past + SAMPLE_ROWS + 1
    comb = comb[:, :length]
    comb = jnp.pad(comb, ((0, 0), (0, length - comb.shape[1])), constant_values=NEG)
    rev = comb[:, ::-1]
    tpast, tnew = [], []
    for row in range(SAMPLE_ROWS):
        t = min(row, n_new - 1)
        start = length - 1 - past - t
        tpast.append(rev[:, start:start + past])
        start = length - 1 - t
        tnew.append(jnp.pad(rev[:, start:start + t + 1], ((0, 0), (0, CHUNK - t - 1)), constant_values=NEG))
    return jnp.stack(tpast, axis=1), jnp.stack(tnew, axis=1)


def _pad_rows(a, rows):
    return jnp.pad(a, ((0, 0), (0, rows - a.shape[1]), (0, 0)))


def kernel(x_prompt, x_sample, cache_k_win, cache_v_win, state_mlstm_C, state_mlstm_n, state_mlstm_m, state_hgrn_S, rel_bias, w_in, w_out, g_attn, g_mlp, w_up, w_down, b_i, b_f, g_mlstm, g_cv, w_s, b_s, hgrn_lb, g_hgrn, g_final):
    depth = w_in.shape[0]
    B, T, _ = x_prompt.shape
    Bs, Ts, _ = x_sample.shape
    past = cache_k_win.shape[2]
    assert past == MAX_WINDOW and T % (MAX_WINDOW) == 0 and Ts <= SAMPLE_ROWS
    H, dh = N_HEADS, HEAD_DIM
    keep_p = min(MAX_WINDOW, T)

    sm = jax.nn.softmax(hgrn_lb.astype(F32), axis=0)
    lb_all = jnp.cumsum(sm, axis=0) - sm[0:1]
    biases = [_pattern_bias(rel_bias, w, d) for w, d in DILATIONS]
    tabs_p = jnp.stack([_prompt_table(bb) for bb in biases])
    tabs_s = [_sample_tables(bb, d, Ts, past) for bb, (_, d) in zip(biases, DILATIONS)]
    tpast = jnp.stack([a for a, _ in tabs_s])
    tnew = jnp.stack([b for _, b in tabs_s])
    kt_all = jnp.transpose(cache_k_win, (0, 1, 3, 4, 2))
    vt_all = jnp.transpose(cache_v_win, (0, 1, 3, 4, 2))

    a_end = Z_WIDTHS[0] + Z_WIDTHS[1]
    g_end = a_end + N_GATE_COLS
    zeros_c = jnp.zeros((B, H, dh, dh), F32)
    zeros_n = jnp.zeros((B, 1, MIX_W), F32)
    zeros_m = jnp.zeros((B, 1, H), F32)
    vec = lambda a: a.reshape(1, -1).astype(F32)
    pseq = math.gcd(B, PROMPT_SEQS_PER_STEP)
    sseq = math.gcd(Bs, SAMPLE_SEQS_PER_STEP)

    xp = x_prompt.reshape(B * T, D_MODEL)
    xs = x_sample.reshape(Bs * Ts, D_MODEL)
    outs = [[] for _ in range(13)]
    for l in range(depth):
        wl = w_in[l]
        w_gate = jnp.pad(wl[:, a_end:g_end], ((0, 0), (0, GATE_W - N_GATE_COLS)))
        w_z = jnp.concatenate([wl[:, :a_end], w_gate, wl[:, g_end:]], axis=1).astype(BF16)
        wg_lo = (w_gate - w_gate.astype(BF16).astype(F32)).astype(BF16)
        wout_b, wup_b, wdn_b = w_out[l].astype(BF16), w_up[l].astype(BF16), w_down[l].astype(BF16)
        gate_bias = jnp.zeros((2, GATE_W), F32).at[0, :H].set(b_i[l]).at[1, H:2 * H].set(b_f[l])
        bs_t = b_s[l].T.astype(F32)
        final = l == depth - 1
        gf = vec(g_final)

        za, zb, zg, zc, zd = _inproj(xp, vec(g_attn[l]), w_z, wg_lo, 512)
        za3 = za.reshape(B, T, -1)
        oa = _attn_prompt(za3, tabs_p)
        ob, c1, n1, m1 = _mlstm(zb.reshape(B, T, -1), zg.reshape(B, T, -1), gate_bias, vec(g_mlstm[l]),
                                zeros_c, zeros_n, zeros_m, CHUNK, pseq)
        (oc,) = _gate(zc.reshape(B, T, -1), w_s[l], bs_t, vec(g_cv[l]), 1, GATE_ROWS, False)
        od, s1 = _hgrn(zd.reshape(B, T, -1), vec(lb_all[l]), vec(g_hgrn[l]), zeros_c, CHUNK, pseq)
        xp = _post(xp, oa.reshape(B * T, -1), ob.reshape(B * T, -1), oc.reshape(B * T, -1), od.reshape(B * T, -1),
                   wout_b, vec(g_mlp[l]), wup_b, wdn_b, gf, 512, 1024, final)
        outs[0].append(za3[:, T - keep_p:, MIX_W:2 * MIX_W].reshape(B, keep_p, H, dh))
        outs[1].append(za3[:, T - keep_p:, 2 * MIX_W:].reshape(B, keep_p, H, dh))
        outs[4].append(c1)
        outs[5].append(n1.reshape(B, H, dh))
        outs[6].append(m1.reshape(B, H))
        outs[10].append(jnp.swapaxes(s1, -1, -2))

        za, zb, zg, zc, zd = _inproj(xs, vec(g_attn[l]), w_z, wg_lo, Bs * Ts)
        za3 = za.reshape(Bs, Ts, -1)
        new_t = lambda a: jnp.pad(jnp.transpose(a.reshape(Bs, Ts, H, dh), (0, 2, 3, 1)),
                                  ((0, 0), (0, 0), (0, 0), (0, LANES - Ts)))
        oa, k_win, v_win = _attn_sample(l, _pad_rows(za3[:, :, :MIX_W], SAMPLE_ROWS), kt_all, vt_all,
                                        new_t(za3[:, :, MIX_W:2 * MIX_W]), new_t(za3[:, :, 2 * MIX_W:]),
                                        tpast, tnew, Ts)
        pad3 = lambda z: _pad_rows(z.reshape(Bs, Ts, -1), CHUNK)
        ob, c2, n2, m2 = _mlstm(pad3(zb), pad3(zg), gate_bias, vec(g_mlstm[l]),
                                state_mlstm_C[l], state_mlstm_n[l].reshape(Bs, 1, MIX_W),
                                state_mlstm_m[l].reshape(Bs, 1, H), Ts, sseq)
        oc, vrows = _gate(pad3(zc), w_s[l], bs_t, vec(g_cv[l]), sseq, CHUNK, True)
        od, s2 = _hgrn(pad3(zd), vec(lb_all[l]), vec(g_hgrn[l]), jnp.swapaxes(state_hgrn_S[l], -1, -2), Ts, sseq)
        flat = lambda o: o[:, :Ts].reshape(Bs * Ts, MIX_W)
        xs = _post(xs, flat(oa), flat(ob), flat(oc), flat(od),
                   wout_b, vec(g_mlp[l]), wup_b, wdn_b, gf, Bs * Ts, 1024, final)
        outs[2].append(jnp.transpose(k_win, (0, 3, 1, 2)))
        outs[3].append(jnp.transpose(v_win, (0, 3, 1, 2)))
        outs[7].append(c2)
        outs[8].append(n2.reshape(Bs, H, dh))
        outs[9].append(m2.reshape(Bs, H))
        outs[11].append(jnp.swapaxes(s2, -1, -2))
        outs[12].append(vrows[:, :Ts].reshape(Bs, Ts, H, dh))

    return (xp.reshape(B, T, D_MODEL), xs.reshape(Bs, Ts, D_MODEL)) + tuple(jnp.stack(o) for o in outs)
```

```python
import functools
import math

import numpy as np
import jax
import jax.numpy as jnp
from jax import lax
from jax.experimental import pallas as pl
from jax.experimental.pallas import tpu as pltpu

F32 = jnp.float32
BF16 = jnp.bfloat16

D_MODEL = 1024
N_HEADS = 4
HEAD_DIM = 64
MIX_W = N_HEADS * HEAD_DIM
DILATIONS = ((128, 1), (512, 4), (2048, 16))
MAX_WINDOW = 2048
N_BUCKETS = 32
D_FF = 4 * D_MODEL
EPS = 1e-6
NEG = -1e30
LB_FLOOR = 1e-30
CHUNK = 128
N_GATE_COLS = 2 * N_HEADS
GATE_W = 128
Z_WIDTHS = (3 * MIX_W, 4 * MIX_W, GATE_W, 2 * MIX_W, 4 * MIX_W)
Z_TOTAL = sum(Z_WIDTHS)
ATTN_SCALE = HEAD_DIM ** -0.5
N_LEVELS = 7
VMEM_LIMIT = 48 * 1024 * 1024
PROMPT_SEQS_PER_STEP = 2
SAMPLE_SEQS_PER_STEP = 4
GATE_ROWS = 4 * CHUNK


def _dot(a, b):
    return jnp.dot(a, b, preferred_element_type=F32)


def _dot_nt(a, b):
    return lax.dot_general(a, b, (((1,), (1,)), ((), ())), preferred_element_type=F32)


def _dot_tn(a, b):
    return lax.dot_general(a, b, (((0,), (0,)), ((), ())), preferred_element_type=F32)


def _rms(x, g):
    return x * lax.rsqrt(jnp.mean(x * x, axis=-1, keepdims=True) + EPS) * g


def _log_sigmoid(x):
    return jnp.minimum(x, 0.0) - jnp.log1p(jnp.exp(-jnp.abs(x)))


def _dot_01(m01, f, pieces=3):
    out = None
    for _ in range(pieces):
        piece = f.astype(BF16)
        f = f - piece.astype(F32)
        out = _dot(m01, piece) if out is None else out + _dot(m01, piece)
    return out


def _head(h):
    return slice(h * HEAD_DIM, (h + 1) * HEAD_DIM)


def _head_scores(x, y):
    rows, width = x.shape
    tile = 2 * HEAD_DIM
    low = lax.broadcasted_iota(jnp.int32, (rows, tile), 1) < HEAD_DIM
    out = []
    for half in range(width // tile):
        lanes = slice(half * tile, (half + 1) * tile)
        xh = x[:, lanes]
        x2 = jnp.concatenate([jnp.where(low, xh, 0.0), jnp.where(low, 0.0, xh)], axis=0).astype(BF16)
        out.append(_dot_nt(x2, y[:, lanes].astype(BF16)))
    return jnp.concatenate(out, axis=0)


def _params(*sem):
    return pltpu.CompilerParams(dimension_semantics=sem, vmem_limit_bytes=VMEM_LIMIT)


def _inproj_body(x_ref, g_ref, w_ref, wg_lo_ref, *out_refs):
    hn = _rms(x_ref[...], g_ref[...])
    h = hn.astype(BF16)
    off = 0
    for o_ref in out_refs:
        n = o_ref.shape[-1]
        z = _dot(h, w_ref[:, off:off + n])
        if n == GATE_W:
            h_lo = (hn - h.astype(F32)).astype(BF16)
            z = z + _dot(h, wg_lo_ref[...]) + _dot(h_lo, w_ref[:, off:off + n])
        o_ref[...] = z
        off += n


def _inproj(x2d, g, w, wg_lo, tm):
    n = x2d.shape[0]
    return pl.pallas_call(
        _inproj_body,
        grid=(n // tm,),
        in_specs=[pl.BlockSpec((tm, D_MODEL), lambda i: (i, 0)),
                  pl.BlockSpec((1, D_MODEL), lambda i: (0, 0)),
                  pl.BlockSpec((D_MODEL, Z_TOTAL), lambda i: (0, 0)),
                  pl.BlockSpec((D_MODEL, GATE_W), lambda i: (0, 0))],
        out_specs=[pl.BlockSpec((tm, wd), lambda i: (i, 0)) for wd in Z_WIDTHS],
        out_shape=[jax.ShapeDtypeStruct((n, wd), F32) for wd in Z_WIDTHS],
        compiler_params=_params("parallel"),
        name="inproj",
    )(x2d, g, w, wg_lo)


ATTN_ROWS = MAX_WINDOW
LANES = 128
UNITS_PER_ITER = 4


def _attn_body(q0, q1, kp0, kp1, kc0, kc1, vp0, vp1, vc0, vc1, tab_ref, o_ref, os_ref, ls_ref):
    first = pl.program_id(1) == 0
    col = lax.broadcasted_iota(jnp.int32, (N_HEADS * CHUNK, 2 * CHUNK), 1)
    low_head = lax.broadcasted_iota(jnp.int32, (CHUNK, LANES), 1) < HEAD_DIM
    ones = jnp.ones((2 * CHUNK, LANES), BF16)
    q_refs, kp_refs, kc_refs, vp_refs, vc_refs = (q0, q1), (kp0, kp1), (kc0, kc1), (vp0, vp1), (vc0, vc1)

    def rows_of(start, d):
        if d == 1:
            return pl.ds(start if isinstance(start, int) else pl.multiple_of(start, CHUNK), CHUNK)
        return pl.ds(start, CHUNK, stride=d)

    def unit(p, d, r, c, prev_in_block):
        start = r + d * CHUNK * c
        cur = rows_of(start, d)
        prev = rows_of(start - d * CHUNK, d) if prev_in_block else rows_of(r + ATTN_ROWS - d * CHUNK, d)
        k_prev_ref, v_prev_ref = (kc_refs, vc_refs) if prev_in_block else (kp_refs, vp_refs)
        scores, vws = [], []
        for half in range(2):
            q = q_refs[half][0, cur, :] * ATTN_SCALE
            kw = jnp.concatenate([k_prev_ref[half][0, prev, :], kc_refs[half][0, cur, :]], axis=0).astype(BF16)
            vws.append(jnp.concatenate([v_prev_ref[half][0, prev, :], vc_refs[half][0, cur, :]], axis=0).astype(BF16))
            qm = jnp.concatenate([jnp.where(low_head, q, 0.0), jnp.where(low_head, 0.0, q)], axis=0).astype(BF16)
            scores.append(_dot_nt(qm, kw))
        s = jnp.concatenate(scores, axis=0) + tab_ref[p]
        if not prev_in_block:
            s = jnp.where(first & (col < CHUNK), NEG, s)
        m = jnp.max(jnp.maximum(s[:, :CHUNK], s[:, CHUNK:]), axis=1, keepdims=True)
        e = jnp.exp(s - m).astype(BF16)
        den = _dot(e, ones)
        lse = m + jnp.log(den)
        for half in range(2):
            rows = slice(2 * half * CHUNK, (2 * half + 1) * CHUNK)
            rows2 = slice((2 * half + 1) * CHUNK, (2 * half + 2) * CHUNK)
            o = _dot(e[2 * half * CHUNK:(2 * half + 2) * CHUNK], vws[half])
            o = o / den[2 * half * CHUNK:(2 * half + 2) * CHUNK]
            os_ref[p, half, cur, :] = jnp.where(low_head, o[:CHUNK], o[CHUNK:])
            ls_ref[p, half, cur, :] = jnp.where(low_head, lse[rows], lse[rows2])

    for p, (_, d) in enumerate(DILATIONS):
        nblk = ATTN_ROWS // d // CHUNK
        if nblk >= UNITS_PER_ITER:
            def per_subsequence(r, carry, p=p, d=d, nblk=nblk):
                for c in range(UNITS_PER_ITER):
                    unit(p, d, r, c, c > 0)

                def group(g, carry2):
                    for u in range(UNITS_PER_ITER):
                        unit(p, d, r, g * UNITS_PER_ITER + u, True)
                    return carry2
                return lax.fori_loop(1, nblk // UNITS_PER_ITER, group, carry)
            if d == 1:
                per_subsequence(0, 0)
            else:
                lax.fori_loop(0, d, per_subsequence, 0)
        else:
            def group(g, carry, p=p, d=d):
                for u in range(UNITS_PER_ITER):
                    unit(p, d, g * UNITS_PER_ITER + u, 0, False)
                return carry
            lax.fori_loop(0, d // UNITS_PER_ITER, group, 0)

    def merge(i, carry):
        rows = pl.ds(pl.multiple_of(i * CHUNK, CHUNK), CHUNK)
        for half in range(2):
            ls = [ls_ref[p, half, rows, :] for p in range(len(DILATIONS))]
            mx = jnp.maximum(jnp.maximum(ls[0], ls[1]), ls[2])
            es = [jnp.exp(l - mx) for l in ls]
            num = es[0] * os_ref[0, half, rows, :] + es[1] * os_ref[1, half, rows, :] + es[2] * os_ref[2, half, rows, :]
            o_ref[0, rows, half * LANES:(half + 1) * LANES] = num / (es[0] + es[1] + es[2])
        return carry
    lax.fori_loop(0, ATTN_ROWS // CHUNK, merge, 0)


def _attn_prompt(za, tabs):
    B, T, _ = za.shape
    blk = (1, ATTN_ROWS, LANES)
    cur = lambda j: pl.BlockSpec(blk, lambda b, i: (b, i, j))
    prv = lambda j: pl.BlockSpec(blk, lambda b, i: (b, jnp.maximum(i - 1, 0), j))
    npat = len(DILATIONS)
    return pl.pallas_call(
        _attn_body,
        grid=(B, T // ATTN_ROWS),
        in_specs=[cur(0), cur(1), prv(2), prv(3), cur(2), cur(3), prv(4), prv(5), cur(4), cur(5),
                  pl.BlockSpec((npat, N_HEADS * CHUNK, 2 * CHUNK), lambda b, i: (0, 0, 0))],
        out_specs=pl.BlockSpec((1, ATTN_ROWS, MIX_W), lambda b, i: (b, i, 0)),
        out_shape=jax.ShapeDtypeStruct((B, T, MIX_W), F32),
        scratch_shapes=[pltpu.VMEM((npat, 2, ATTN_ROWS, LANES), F32)] * 2,
        compiler_params=_params("parallel", "arbitrary"),
        name="attn_prompt",
    )(*([za] * 10), tabs.reshape(npat, N_HEADS * CHUNK, 2 * CHUNK))


SAMPLE_ROWS = 8


def _attn_sample_body(q_ref, tpast_ref, tnew_ref, *rest, n_new, n_earlier, write_windows):
    kt_ref, vt_ref, knt_ref, vnt_ref = rest[:4]
    earlier = [rest[4 + 4 * e:8 + 4 * e] for e in range(n_earlier)]
    outs = rest[4 + 4 * n_earlier:]
    o_ref = outs[0]
    npat = len(DILATIONS)
    q8 = q_ref[0] * ATTN_SCALE
    for h in range(N_HEADS):
        sl = _head(h)
        qh = q8[:, sl].astype(BF16)
        kt, vt = kt_ref[0, 0, h], vt_ref[0, 0, h]
        knt, vnt = knt_ref[0, h], vnt_ref[0, h]
        sp = _dot(qh, kt.astype(BF16))
        sn = _dot(qh, knt.astype(BF16))
        pps, pns, dens, lses = [], [], [], []
        for p in range(npat):
            lp = sp + tpast_ref[p, h]
            ln = sn + tnew_ref[p, h]
            m = jnp.maximum(jnp.max(lp, axis=1, keepdims=True), jnp.max(ln, axis=1, keepdims=True))
            pps.append(jnp.exp(lp - m))
            pns.append(jnp.exp(ln - m))
            dens.append(jnp.sum(pps[-1], axis=1, keepdims=True) + jnp.sum(pns[-1], axis=1, keepdims=True))
            lses.append(m + jnp.log(dens[-1]))
        pv = (_dot_nt(jnp.concatenate(pps, axis=0).astype(BF16), vt.astype(BF16))
              + _dot_nt(jnp.concatenate(pns, axis=0).astype(BF16), vnt.astype(BF16)))
        mx = jnp.maximum(jnp.maximum(lses[0], lses[1]), lses[2])
        es = [jnp.exp(l - mx) for l in lses]
        num = sum(es[p] * pv[p * SAMPLE_ROWS:(p + 1) * SAMPLE_ROWS] / dens[p] for p in range(npat))
        o_ref[0, :, sl] = num / (es[0] + es[1] + es[2])
        if write_windows:
            slabs = earlier + [(kt_ref, vt_ref, knt_ref, vnt_ref)]
            for layer, (kt_l, vt_l, knt_l, vnt_l) in enumerate(slabs):
                outs[1][layer, 0, h] = jnp.concatenate([kt_l[0, 0, h][:, n_new:], knt_l[0, h][:, :n_new]], axis=1)
                outs[2][layer, 0, h] = jnp.concatenate([vt_l[0, 0, h][:, n_new:], vnt_l[0, h][:, :n_new]], axis=1)


def _attn_sample(layer, q8, kt_all, vt_all, new_rows, tpast, tnew, n_new):
    depth, B, _, _, P = kt_all.shape
    npat = len(DILATIONS)
    write_windows = layer == depth - 1
    n_earlier = layer if write_windows else 0
    cache = lambda l: pl.BlockSpec((1, 1, N_HEADS, HEAD_DIM, P), lambda b: (l, b, 0, 0, 0))
    new = pl.BlockSpec((1, N_HEADS, HEAD_DIM, LANES), lambda b: (b, 0, 0, 0))
    rows = pl.BlockSpec((1, SAMPLE_ROWS, MIX_W), lambda b: (b, 0, 0))
    in_specs = [rows,
                pl.BlockSpec((npat, N_HEADS, SAMPLE_ROWS, P), lambda b: (0, 0, 0, 0)),
                pl.BlockSpec((npat, N_HEADS, SAMPLE_ROWS, LANES), lambda b: (0, 0, 0, 0))]
    args = [q8, tpast, tnew]
    for l in [layer] + list(range(n_earlier)):
        in_specs += [cache(l), cache(l), new, new]
        args += [kt_all, vt_all, *new_rows[l]]
    out_specs = [rows]
    out_shape = [jax.ShapeDtypeStruct((B, SAMPLE_ROWS, MIX_W), F32)]
    if write_windows:
        out_specs += [pl.BlockSpec((depth, 1, N_HEADS, HEAD_DIM, P), lambda b: (0, b, 0, 0, 0))] * 2
        out_shape += [jax.ShapeDtypeStruct(kt_all.shape, F32)] * 2
    return pl.pallas_call(
        functools.partial(_attn_sample_body, n_new=n_new, n_earlier=n_earlier, write_windows=write_windows),
        grid=(B,),
        in_specs=in_specs,
        out_specs=out_specs,
        out_shape=out_shape,
        compiler_params=_params("parallel"),
        name="attn_sample",
    )(*args)


def _mlstm_body(zq_ref, zk_ref, zv_ref, zo_ref, zg_ref, bias_ref, gn_ref, c0_ref, n0_ref, m0_ref,
                out_ref, c_out, n_out, m_out, c_s, n_s, m_s, *, n_valid):
    c = pl.program_id(1)

    nseq = zq_ref.shape[0]

    @pl.when(c == 0)
    def _():
        c_s[...] = jnp.zeros_like(c_s)
        for i in range(nseq):
            for h in range(N_HEADS):
                c_s[i, _head(h), _head(h)] = c0_ref[i, h]
        n_s[...] = n0_ref[...]
        m_s[...] = m0_ref[...]

    rows4 = N_HEADS * CHUNK
    t4 = lax.broadcasted_iota(jnp.int32, (rows4, CHUNK), 0) & (CHUNK - 1)
    causal4 = t4 >= lax.broadcasted_iota(jnp.int32, (rows4, CHUNK), 1)
    valid4 = t4[:, 0:1] < n_valid
    tril = (lax.broadcasted_iota(jnp.int32, (CHUNK, CHUNK), 0)
            >= lax.broadcasted_iota(jnp.int32, (CHUNK, CHUNK), 1)).astype(BF16)
    lane_head = lax.broadcasted_iota(jnp.int32, (CHUNK, MIX_W), 1) // HEAD_DIM
    same_head = (lax.broadcasted_iota(jnp.int32, (MIX_W, MIX_W), 0) // HEAD_DIM
                 == lax.broadcasted_iota(jnp.int32, (MIX_W, MIX_W), 1) // HEAD_DIM)
    head_ones = same_head.astype(BF16)
    ones_l = jnp.ones((CHUNK, MIX_W), BF16)
    ones_w = jnp.ones((MIX_W, MIX_W), BF16)
    last = n_valid - 1

    def stack(f):
        return jnp.concatenate([f(h) for h in range(N_HEADS)], axis=0)

    def on_head_lanes(f, lanes):
        out = f(N_HEADS - 1)
        for h in range(N_HEADS - 2, -1, -1):
            out = jnp.where(lanes == h, f(h), out)
        return out

    for i in range(nseq):
        gates = zg_ref[i]
        ipre = gates + bias_ref[0:1, :]
        logf = _log_sigmoid(gates + bias_ref[1:2, :])
        bcum = _dot_01(tril, logf)
        bcum_t = bcum.T
        ipre_t = ipre.T
        q = zq_ref[i]
        k = zk_ref[i] * ATTN_SCALE
        v = zv_ref[i]
        kb, vb = k.astype(BF16), v.astype(BF16)
        cbd = c_s[i]
        nrow = n_s[i]
        mprev = m_s[i]
        bcol = stack(lambda h: bcum[:, N_HEADS + h:N_HEADS + h + 1])
        icol = stack(lambda h: ipre[:, h:h + 1])
        brow = stack(lambda h: jnp.broadcast_to(bcum_t[N_HEADS + h:N_HEADS + h + 1, :], (CHUNK, CHUNK)))
        irow = stack(lambda h: jnp.broadcast_to(ipre_t[h:h + 1, :], (CHUNK, CHUNK)))
        mprev4 = stack(lambda h: jnp.broadcast_to(mprev[:, h:h + 1], (CHUNK, 1)))
        dmat = jnp.where(causal4, bcol - brow + irow, NEG)
        g = bcol + mprev4
        mt = jnp.maximum(g, jnp.max(dmat, axis=1, keepdims=True))
        dexp = jnp.exp(dmat - mt)
        gexp = jnp.exp(g - mt)
        qm_f = stack(lambda h: jnp.where(lane_head == h, q, 0.0))
        qm = qm_f.astype(BF16)
        sb = (_head_scores(q, k) * dexp).astype(BF16)
        num = _dot(sb, vb) + gexp * _dot_nt(qm, cbd.astype(BF16))
        nq = _dot(sb, ones_l) + gexp * _dot((qm_f * nrow).astype(BF16), ones_w)
        hh4 = num / jnp.maximum(jnp.abs(nq), jnp.exp(-mt))
        hh = on_head_lanes(lambda h: hh4[h * CHUNK:(h + 1) * CHUNK], lane_head)
        sq = hh * hh
        sq_hi = sq.astype(BF16)
        sq_lo = (sq - sq_hi.astype(F32)).astype(BF16)
        ms = (_dot(sq_hi, head_ones) + _dot(sq_lo, head_ones)) * (1.0 / HEAD_DIM)
        out_ref[i] = jax.nn.sigmoid(zo_ref[i]) * (hh * lax.rsqrt(ms + EPS) * gn_ref[...])

        mnew = [mt[h * CHUNK + last:h * CHUNK + last + 1, :] for h in range(N_HEADS)]
        blast = [bcol[h * CHUNK + last:h * CHUNK + last + 1, :] for h in range(N_HEADS)]
        wk = jnp.exp(stack(lambda h: jnp.broadcast_to(blast[h] - mnew[h], (CHUNK, 1))) - bcol + icol)
        if n_valid < CHUNK:
            wk = jnp.where(valid4, wk, 0.0)
        wk_w = on_head_lanes(lambda h: jnp.broadcast_to(wk[h * CHUNK:(h + 1) * CHUNK], (CHUNK, MIX_W)), lane_head)
        dc = on_head_lanes(lambda h: jnp.broadcast_to(jnp.exp(blast[h] + mprev[:, h:h + 1] - mnew[h]), (1, MIX_W)),
                           lane_head[0:1])
        c_s[i] = dc * cbd + jnp.where(same_head, _dot_tn((v * wk_w).astype(BF16), kb), 0.0)
        n_s[i] = dc * nrow + jnp.sum(wk_w * k, axis=0, keepdims=True)
        for h in range(N_HEADS):
            m_s[i, :, h:h + 1] = mnew[h]

    @pl.when(c == pl.num_programs(1) - 1)
    def _():
        for i in range(nseq):
            for h in range(N_HEADS):
                c_out[i, h] = c_s[i, _head(h), _head(h)]
        n_out[...] = n_s[...]
        m_out[...] = m_s[...]


def _mlstm(zb, zg, bias, gn, c0, n0, m0, n_valid, nseq):
    B, T, _ = zb.shape
    nc = T // CHUNK
    blk = (nseq, CHUNK, MIX_W)
    zspec = lambda j: pl.BlockSpec(blk, lambda b, c: (b, c, j))
    st4 = pl.BlockSpec((nseq, N_HEADS, HEAD_DIM, HEAD_DIM), lambda b, c: (b, 0, 0, 0))
    st3 = pl.BlockSpec((nseq, 1, MIX_W), lambda b, c: (b, 0, 0))
    st2 = pl.BlockSpec((nseq, 1, N_HEADS), lambda b, c: (b, 0, 0))
    return pl.pallas_call(
        functools.partial(_mlstm_body, n_valid=n_valid),
        grid=(B // nseq, nc),
        in_specs=[zspec(0), zspec(1), zspec(2), zspec(3),
                  pl.BlockSpec((nseq, CHUNK, GATE_W), lambda b, c: (b, c, 0)),
                  pl.BlockSpec((2, GATE_W), lambda b, c: (0, 0)),
                  pl.BlockSpec((1, MIX_W), lambda b, c: (0, 0)),
                  st4, st3, st2],
        out_specs=[pl.BlockSpec(blk, lambda b, c: (b, c, 0)), st4, st3, st2],
        out_shape=[jax.ShapeDtypeStruct((B, T, MIX_W), F32),
                   jax.ShapeDtypeStruct((B, N_HEADS, HEAD_DIM, HEAD_DIM), F32),
                   jax.ShapeDtypeStruct((B, 1, MIX_W), F32),
                   jax.ShapeDtypeStruct((B, 1, N_HEADS), F32)],
        scratch_shapes=[pltpu.VMEM((nseq, MIX_W, MIX_W), F32),
                        pltpu.VMEM((nseq, 1, MIX_W), F32),
                        pltpu.VMEM((nseq, 1, N_HEADS), F32)],
        compiler_params=_params("parallel", "arbitrary"),
        name="mlstm",
    )(zb, zb, zb, zb, zg, bias, gn, c0, n0, m0)


def _gate_body(zc_ref, w_ref, bs_ref, gcv_ref, out_ref, *maybe_vrow_ref):
    row = lax.broadcasted_iota(jnp.int32, (CHUNK, CHUNK), 0)
    col = lax.broadcasted_iota(jnp.int32, (CHUNK, CHUNK), 1)
    ws = [jnp.where(row >= col, w_ref[h], 0.0).astype(BF16) for h in range(N_HEADS)]
    for i in range(zc_ref.shape[0]):
        for j in range(zc_ref.shape[1] // CHUNK):
            rows = slice(j * CHUNK, (j + 1) * CHUNK)
            z = zc_ref[i, rows, :]
            u = jax.nn.gelu(z[:, :MIX_W])
            vn = _rms(jax.nn.gelu(z[:, MIX_W:]), gcv_ref[...])
            if maybe_vrow_ref:
                maybe_vrow_ref[0][i, rows, :] = vn
            vb = vn.astype(BF16)
            for h in range(N_HEADS):
                sl = _head(h)
                s = _dot(ws[h], vb[:, sl]) + bs_ref[:, h:h + 1]
                out_ref[i, rows, sl] = u[:, sl] * s


def _gate(zc, w_s, bs_t, gcv, nseq, rows, want_vrows):
    B, T, _ = zc.shape
    blk = (nseq, rows, MIX_W)
    n_out = 2 if want_vrows else 1
    return pl.pallas_call(
        _gate_body,
        grid=(B // nseq, T // rows),
        in_specs=[pl.BlockSpec((nseq, rows, 2 * MIX_W), lambda b, c: (b, c, 0)),
                  pl.BlockSpec((N_HEADS, CHUNK, CHUNK), lambda b, c: (0, 0, 0)),
                  pl.BlockSpec((CHUNK, N_HEADS), lambda b, c: (0, 0)),
                  pl.BlockSpec((1, MIX_W), lambda b, c: (0, 0))],
        out_specs=[pl.BlockSpec(blk, lambda b, c: (b, c, 0))] * n_out,
        out_shape=[jax.ShapeDtypeStruct((B, T, MIX_W), F32)] * n_out,
        compiler_params=_params("parallel", "parallel"),
        name="gate",
    )(zc, w_s, bs_t, gcv)


def _hgrn_levels(n_valid):
    return [l for l in range(N_LEVELS) if (CHUNK >> (l + 1)) < max(n_valid, 2)]


def _hgrn_tables(n_valid):
    p = np.arange(CHUNK)[:, None]
    u = np.arange(CHUNK)[None, :]
    mats = []
    for l in _hgrn_levels(n_valid):
        m = CHUNK >> (l + 1)
        start = (p // m) * m
        odd = ((p // m) % 2) == 1
        mats.append(np.where(odd, (u >= start) & (u <= p), (u > p) & (u <= start + m - 1)))
    mats.append(u <= p)
    mats.append((u > p) & (u <= n_valid - 1))
    mall = np.concatenate(mats, axis=0).astype(np.float32)
    t = np.arange(CHUNK)[:, None]
    s = np.arange(CHUNK)[None, :]
    x = t ^ s
    top = np.floor(np.log2(np.maximum(x, 1))).astype(np.int32)
    lvl = np.where(s < t, N_LEVELS - 1 - top, np.where(s == t, N_LEVELS, N_LEVELS + 1)).astype(np.int32)
    return jnp.asarray(mall, BF16), jnp.asarray(np.tile(lvl, (N_HEADS, 1)))


def _hgrn_body(zq_ref, zf_ref, zi_ref, zg_ref, lb_ref, gn_ref, mall_ref, lvl_ref, s0_ref,
               out_ref, s_out, s_s, *, n_valid):
    c = pl.program_id(1)

    nseq = zq_ref.shape[0]

    @pl.when(c == 0)
    def _():
        s_s[...] = jnp.zeros_like(s_s)
        for i in range(nseq):
            for h in range(N_HEADS):
                s_s[i, _head(h), _head(h)] = s0_ref[i, h]

    lb = lb_ref[...]
    a = jnp.log(jnp.maximum(lb, LB_FLOOR))
    lvl = lvl_ref[...]
    valid_rows = lax.broadcasted_iota(jnp.int32, (CHUNK, 1), 0) < n_valid
    lane_head = lax.broadcasted_iota(jnp.int32, (CHUNK, MIX_W), 1) // HEAD_DIM
    same_head = (lax.broadcasted_iota(jnp.int32, (MIX_W, MIX_W), 0) // HEAD_DIM
                 == lax.broadcasted_iota(jnp.int32, (MIX_W, MIX_W), 1) // HEAD_DIM)
    head_ones = same_head.astype(BF16)
    levels = _hgrn_levels(n_valid)

    def by_head(x):
        return jnp.concatenate([jnp.where(lane_head == h, x, 0.0) for h in range(N_HEADS)], axis=0).astype(BF16)

    for i in range(nseq):
        q = zq_ref[i]
        fx = zf_ref[i]
        ct = jnp.log1p(-lb) + _log_sigmoid(fx)
        logf = jnp.maximum(a, ct) + jnp.log1p(jnp.exp(-jnp.abs(a - ct)))
        kd = (1.0 - lb) * jax.nn.sigmoid(-fx)
        gsum = _dot_01(mall_ref[...], logf, pieces=2)
        amat = None
        for j, l in enumerate(levels):
            e = jnp.exp(gsum[j * CHUNK:(j + 1) * CHUNK])
            amat = jnp.where(lvl == l, _head_scores(q * e, kd * e), 0.0 if j == 0 else amat)
        amat = jnp.where(lvl == N_LEVELS, _head_scores(q, kd), amat)
        vb = zi_ref[i].astype(BF16)
        bcum = gsum[len(levels) * CHUNK:(len(levels) + 1) * CHUNK]
        k_out = jnp.where(valid_rows, kd * jnp.exp(gsum[(len(levels) + 1) * CHUNK:]), 0.0).astype(BF16)
        e_last = jnp.exp(bcum[n_valid - 1:n_valid, :])
        sbd = s_s[i]
        o4 = _dot(amat.astype(BF16), vb) + _dot_nt(by_head(q * jnp.exp(bcum)), sbd.astype(BF16))
        o = o4[(N_HEADS - 1) * CHUNK:]
        for h in range(N_HEADS - 2, -1, -1):
            o = jnp.where(lane_head == h, o4[h * CHUNK:(h + 1) * CHUNK], o)
        sq = o * o
        sq_hi = sq.astype(BF16)
        sq_lo = (sq - sq_hi.astype(F32)).astype(BF16)
        ms = (_dot(sq_hi, head_ones) + _dot(sq_lo, head_ones)) * (1.0 / HEAD_DIM)
        gate = zg_ref[i]
        out_ref[i] = o * lax.rsqrt(ms + EPS) * gn_ref[...] * (gate * jax.nn.sigmoid(gate))
        s_s[i] = e_last * sbd + jnp.where(same_head, _dot_tn(vb, k_out), 0.0)

    @pl.when(c == pl.num_programs(1) - 1)
    def _():
        for i in range(nseq):
            for h in range(N_HEADS):
                s_out[i, h] = s_s[i, _head(h), _head(h)]


def _hgrn(zd, lb, gn, s0_t, n_valid, nseq):
    B, T, _ = zd.shape
    mall, lvl = _hgrn_tables(n_valid)
    blk = (nseq, CHUNK, MIX_W)
    zspec = lambda j: pl.BlockSpec(blk, lambda b, c: (b, c, j))
    st4 = pl.BlockSpec((nseq, N_HEADS, HEAD_DIM, HEAD_DIM), lambda b, c: (b, 0, 0, 0))
    vec = pl.BlockSpec((1, MIX_W), lambda b, c: (0, 0))
    return pl.pallas_call(
        functools.partial(_hgrn_body, n_valid=n_valid),
        grid=(B // nseq, T // CHUNK),
        in_specs=[zspec(0), zspec(1), zspec(2), zspec(3), vec, vec,
                  pl.BlockSpec(mall.shape, lambda b, c: (0, 0)),
                  pl.BlockSpec(lvl.shape, lambda b, c: (0, 0)),
                  st4],
        out_specs=[pl.BlockSpec(blk, lambda b, c: (b, c, 0)), st4],
        out_shape=[jax.ShapeDtypeStruct((B, T, MIX_W), F32),
                   jax.ShapeDtypeStruct((B, N_HEADS, HEAD_DIM, HEAD_DIM), F32)],
        scratch_shapes=[pltpu.VMEM((nseq, MIX_W, MIX_W), F32)],
        compiler_params=_params("parallel", "arbitrary"),
        name="hgrn",
    )(zd, zd, zd, zd, lb, gn, mall, lvl, s0_t)


def _post_body(x_ref, oa_ref, ob_ref, oc_ref, od_ref, wout_hbm, gm_ref, wup_hbm, wdn_hbm, gf_ref, y_ref,
               wout_ref, wup_ref, wdn_ref, h_s, acc_s, *, tf, final):
    @pl.when(pl.program_id(0) == 0)
    def _():
        pltpu.sync_copy(wout_hbm, wout_ref)
        pltpu.sync_copy(wup_hbm, wup_ref)
        pltpu.sync_copy(wdn_hbm, wdn_ref)

    mix = jnp.concatenate([o_ref[...].astype(BF16) for o_ref in (oa_ref, ob_ref, oc_ref, od_ref)], axis=1)
    x1 = x_ref[...] + _dot(mix, wout_ref[...])
    acc_s[...] = x1
    h_s[...] = _rms(x1, gm_ref[...]).astype(BF16)
    for j in range(D_FF // tf):
        up = jnp.maximum(_dot(h_s[...], wup_ref[:, j * tf:(j + 1) * tf]), 0.0)
        acc_s[...] += _dot((up * up).astype(BF16), wdn_ref[j * tf:(j + 1) * tf, :])
    y_ref[...] = _rms(acc_s[...], gf_ref[...]) if final else acc_s[...]


def _post(x2d, oa, ob, oc, od, wout, gm, wup, wdn, gf, tm, tf, final):
    n = x2d.shape[0]
    row = lambda wd: pl.BlockSpec((tm, wd), lambda i: (i, 0))
    vec = pl.BlockSpec((1, D_MODEL), lambda i: (0, 0))
    hbm = pl.BlockSpec(memory_space=pl.ANY)
    return pl.pallas_call(
        functools.partial(_post_body, tf=tf, final=final),
        grid=(n // tm,),
        in_specs=[row(D_MODEL)] + [row(MIX_W)] * 4 + [hbm, vec, hbm, hbm, vec],
        out_specs=row(D_MODEL),
        out_shape=jax.ShapeDtypeStruct((n, D_MODEL), F32),
        scratch_shapes=[pltpu.VMEM(wout.shape, BF16), pltpu.VMEM(wup.shape, BF16), pltpu.VMEM(wdn.shape, BF16),
                        pltpu.VMEM((tm, D_MODEL), BF16), pltpu.VMEM((tm, D_MODEL), F32)],
        compiler_params=_params("arbitrary"),
        name="post",
    )(x2d, oa, ob, oc, od, wout, gm, wup, wdn, gf)


def _rel_bucket(dist):
    max_exact = N_BUCKETS // 2
    d = jnp.maximum(dist, 1).astype(F32)
    large = max_exact + (jnp.log(d / max_exact) / math.log(MAX_WINDOW / max_exact)
                         * (N_BUCKETS - max_exact)).astype(jnp.int32)
    large = jnp.clip(large, max_exact, N_BUCKETS - 1)
    return jnp.where(dist < max_exact, dist, large)


def _pattern_bias(rel_bias, w, d):
    offs = jnp.arange(w // d + 1, dtype=jnp.int32) * d
    return rel_bias[_rel_bucket(offs)].T.astype(F32)


def _prompt_table(bias):
    cols = 2 * CHUNK
    u = jnp.concatenate([bias[:, ::-1], jnp.full((N_HEADS, cols - CHUNK), NEG, F32)], axis=1)
    return jnp.tile(u, (1, CHUNK))[:, :CHUNK * cols].reshape(N_HEADS, CHUNK, cols)


def _sample_tables(bias, d, n_new, past):
    comb = jnp.concatenate([bias[:, :, None], jnp.full((N_HEADS, CHUNK + 1, d - 1), NEG, F32)], axis=2)
    comb = comb.reshape(N_HEADS, (CHUNK + 1) * d)
    length = past + SAMPLE_ROWS + 1
    comb = comb[:, :length]
    comb = jnp.pad(comb, ((0, 0), (0, length - comb.shape[1])), constant_values=NEG)
    rev = comb[:, ::-1]
    tpast, tnew = [], []
    for row in range(SAMPLE_ROWS):
        t = min(row, n_new - 1)
        start = length - 1 - past - t
        tpast.append(rev[:, start:start + past])
        start = length - 1 - t
        tnew.append(jnp.pad(rev[:, start:start + t + 1], ((0, 0), (0, CHUNK - t - 1)), constant_values=NEG))
    return jnp.stack(tpast, axis=1), jnp.stack(tnew, axis=1)


def _pad_rows(a, rows):
    return jnp.pad(a, ((0, 0), (0, rows - a.shape[1]), (0, 0)))


def kernel(x_prompt, x_sample, cache_k_win, cache_v_win, state_mlstm_C, state_mlstm_n, state_mlstm_m, state_hgrn_S, rel_bias, w_in, w_out, g_attn, g_mlp, w_up, w_down, b_i, b_f, g_mlstm, g_cv, w_s, b_s, hgrn_lb, g_hgrn, g_final):
    depth = w_in.shape[0]
    B, T, _ = x_prompt.shape
    Bs, Ts, _ = x_sample.shape
    past = cache_k_win.shape[2]
    assert past == MAX_WINDOW and T % (MAX_WINDOW) == 0 and Ts <= SAMPLE_ROWS
    H, dh = N_HEADS, HEAD_DIM
    keep_p = min(MAX_WINDOW, T)

    sm = jax.nn.softmax(hgrn_lb.astype(F32), axis=0)
    lb_all = jnp.cumsum(sm, axis=0) - sm[0:1]
    biases = [_pattern_bias(rel_bias, w, d) for w, d in DILATIONS]
    tabs_p = jnp.stack([_prompt_table(bb) for bb in biases])
    tabs_s = [_sample_tables(bb, d, Ts, past) for bb, (_, d) in zip(biases, DILATIONS)]
    tpast = jnp.stack([a for a, _ in tabs_s])
    tnew = jnp.stack([b for _, b in tabs_s])
    kt_all = jnp.transpose(cache_k_win, (0, 1, 3, 4, 2))
    vt_all = jnp.transpose(cache_v_win, (0, 1, 3, 4, 2))

    a_end = Z_WIDTHS[0] + Z_WIDTHS[1]
    g_end = a_end + N_GATE_COLS
    zeros_c = jnp.zeros((B, H, dh, dh), F32)
    zeros_n = jnp.zeros((B, 1, MIX_W), F32)
    zeros_m = jnp.zeros((B, 1, H), F32)
    vec = lambda a: a.reshape(1, -1).astype(F32)
    pseq = math.gcd(B, PROMPT_SEQS_PER_STEP)
    sseq = math.gcd(Bs, SAMPLE_SEQS_PER_STEP)

    xp = x_prompt.reshape(B * T, D_MODEL)
    xs = x_sample.reshape(Bs * Ts, D_MODEL)
    outs = [[] for _ in range(13)]
    new_rows = []
    for l in range(depth):
        wl = w_in[l]
        w_gate = jnp.pad(wl[:, a_end:g_end], ((0, 0), (0, GATE_W - N_GATE_COLS)))
        w_z = jnp.concatenate([wl[:, :a_end], w_gate, wl[:, g_end:]], axis=1).astype(BF16)
        wg_lo = (w_gate - w_gate.astype(BF16).astype(F32)).astype(BF16)
        wout_b, wup_b, wdn_b = w_out[l].astype(BF16), w_up[l].astype(BF16), w_down[l].astype(BF16)
        gate_bias = jnp.zeros((2, GATE_W), F32).at[0, :H].set(b_i[l]).at[1, H:2 * H].set(b_f[l])
        bs_t = b_s[l].T.astype(F32)
        final = l == depth - 1
        gf = vec(g_final)

        za, zb, zg, zc, zd = _inproj(xp, vec(g_attn[l]), w_z, wg_lo, 512)
        za3 = za.reshape(B, T, -1)
        oa = _attn_prompt(za3, tabs_p)
        ob, c1, n1, m1 = _mlstm(zb.reshape(B, T, -1), zg.reshape(B, T, -1), gate_bias, vec(g_mlstm[l]),
                                zeros_c, zeros_n, zeros_m, CHUNK, pseq)
        (oc,) = _gate(zc.reshape(B, T, -1), w_s[l], bs_t, vec(g_cv[l]), 1, GATE_ROWS, False)
        od, s1 = _hgrn(zd.reshape(B, T, -1), vec(lb_all[l]), vec(g_hgrn[l]), zeros_c, CHUNK, pseq)
        xp = _post(xp, oa.reshape(B * T, -1), ob.reshape(B * T, -1), oc.reshape(B * T, -1), od.reshape(B * T, -1),
                   wout_b, vec(g_mlp[l]), wup_b, wdn_b, gf, 512, 1024, final)
        outs[0].append(za3[:, T - keep_p:, MIX_W:2 * MIX_W].reshape(B, keep_p, H, dh))
        outs[1].append(za3[:, T - keep_p:, 2 * MIX_W:].reshape(B, keep_p, H, dh))
        outs[4].append(c1)
        outs[5].append(n1.reshape(B, H, dh))
        outs[6].append(m1.reshape(B, H))
        outs[10].append(jnp.swapaxes(s1, -1, -2))

        za, zb, zg, zc, zd = _inproj(xs, vec(g_attn[l]), w_z, wg_lo, Bs * Ts)
        za3 = za.reshape(Bs, Ts, -1)
        new_t = lambda a: jnp.pad(jnp.transpose(a.reshape(Bs, Ts, H, dh), (0, 2, 3, 1)),
                                  ((0, 0), (0, 0), (0, 0), (0, LANES - Ts)))
        new_rows.append((new_t(za3[:, :, MIX_W:2 * MIX_W]), new_t(za3[:, :, 2 * MIX_W:])))
        oa, *windows = _attn_sample(l, _pad_rows(za3[:, :, :MIX_W], SAMPLE_ROWS), kt_all, vt_all,
                                    new_rows, tpast, tnew, Ts)
        pad3 = lambda z: _pad_rows(z.reshape(Bs, Ts, -1), CHUNK)
        ob, c2, n2, m2 = _mlstm(pad3(zb), pad3(zg), gate_bias, vec(g_mlstm[l]),
                                state_mlstm_C[l], state_mlstm_n[l].reshape(Bs, 1, MIX_W),
                                state_mlstm_m[l].reshape(Bs, 1, H), Ts, sseq)
        oc, vrows = _gate(pad3(zc), w_s[l], bs_t, vec(g_cv[l]), sseq, CHUNK, True)
        od, s2 = _hgrn(pad3(zd), vec(lb_all[l]), vec(g_hgrn[l]), jnp.swapaxes(state_hgrn_S[l], -1, -2), Ts, sseq)
        flat = lambda o: o[:, :Ts].reshape(Bs * Ts, MIX_W)
        xs = _post(xs, flat(oa), flat(ob), flat(oc), flat(od),
                   wout_b, vec(g_mlp[l]), wup_b, wdn_b, gf, Bs * Ts, 1024, final)
        outs[7].append(c2)
        outs[8].append(n2.reshape(Bs, H, dh))
        outs[9].append(m2.reshape(Bs, H))
        outs[11].append(jnp.swapaxes(s2, -1, -2))
        outs[12].append(vrows[:, :Ts].reshape(Bs, Ts, H, dh))

    stacked = [jnp.stack(o) if o else None for o in outs]
    stacked[2], stacked[3] = (jnp.transpose(w, (0, 1, 4, 2, 3)) for w in windows)
    return (xp.reshape(B, T, D_MODEL), xs.reshape(Bs, Ts, D_MODEL)) + tuple(stacked)
```

```python
import functools
import math

import numpy as np
import jax
import jax.numpy as jnp
from jax import lax
from jax.experimental import pallas as pl
from jax.experimental.pallas import tpu as pltpu

F32 = jnp.float32
BF16 = jnp.bfloat16

D_MODEL = 1024
N_HEADS = 4
HEAD_DIM = 64
MIX_W = N_HEADS * HEAD_DIM
DILATIONS = ((128, 1), (512, 4), (2048, 16))
MAX_WINDOW = 2048
N_BUCKETS = 32
D_FF = 4 * D_MODEL
EPS = 1e-6
NEG = -1e30
LB_FLOOR = 1e-30
CHUNK = 128
N_GATE_COLS = 2 * N_HEADS
GATE_W = 128
Z_WIDTHS = (3 * MIX_W, 4 * MIX_W, GATE_W, 2 * MIX_W, 4 * MIX_W)
Z_TOTAL = sum(Z_WIDTHS)
ATTN_SCALE = HEAD_DIM ** -0.5
N_LEVELS = 7
VMEM_LIMIT = 48 * 1024 * 1024
PROMPT_SEQS_PER_STEP = 2
SAMPLE_SEQS_PER_STEP = 4
GATE_ROWS = 4 * CHUNK


def _dot(a, b):
    return jnp.dot(a, b, preferred_element_type=F32)


def _dot_nt(a, b):
    return lax.dot_general(a, b, (((1,), (1,)), ((), ())), preferred_element_type=F32)


def _dot_tn(a, b):
    return lax.dot_general(a, b, (((0,), (0,)), ((), ())), preferred_element_type=F32)


def _rms(x, g):
    return x * lax.rsqrt(jnp.mean(x * x, axis=-1, keepdims=True) + EPS) * g


def _log_sigmoid(x):
    return jnp.minimum(x, 0.0) - jnp.log1p(jnp.exp(-jnp.abs(x)))


def _dot_01(m01, f, pieces=3):
    out = None
    for _ in range(pieces):
        piece = f.astype(BF16)
        f = f - piece.astype(F32)
        out = _dot(m01, piece) if out is None else out + _dot(m01, piece)
    return out


def _head(h):
    return slice(h * HEAD_DIM, (h + 1) * HEAD_DIM)


def _head_scores(x, y):
    rows, width = x.shape
    tile = 2 * HEAD_DIM
    low = lax.broadcasted_iota(jnp.int32, (rows, tile), 1) < HEAD_DIM
    out = []
    for half in range(width // tile):
        lanes = slice(half * tile, (half + 1) * tile)
        xh = x[:, lanes]
        x2 = jnp.concatenate([jnp.where(low, xh, 0.0), jnp.where(low, 0.0, xh)], axis=0).astype(BF16)
        out.append(_dot_nt(x2, y[:, lanes].astype(BF16)))
    return jnp.concatenate(out, axis=0)


def _chunk_tile(ref, i, pad_ref):
    if pad_ref is None:
        return ref[i]
    pad_ref[0:ref.shape[1], :] = ref[i]
    return pad_ref[...]


def _zero_pads(pads):
    for p in pads:
        p[...] = jnp.zeros_like(p)


def _params(*sem):
    return pltpu.CompilerParams(dimension_semantics=sem, vmem_limit_bytes=VMEM_LIMIT)


def _inproj_body(x_ref, g_ref, w_ref, wg_lo_ref, *out_refs):
    hn = _rms(x_ref[...], g_ref[...])
    h = hn.astype(BF16)
    off = 0
    for o_ref in out_refs:
        n = o_ref.shape[-1]
        z = _dot(h, w_ref[:, off:off + n])
        if n == GATE_W:
            h_lo = (hn - h.astype(F32)).astype(BF16)
            z = z + _dot(h, wg_lo_ref[...]) + _dot(h_lo, w_ref[:, off:off + n])
        o_ref[...] = z
        off += n


def _inproj(x2d, g, w, wg_lo, tm):
    n = x2d.shape[0]
    return pl.pallas_call(
        _inproj_body,
        grid=(n // tm,),
        in_specs=[pl.BlockSpec((tm, D_MODEL), lambda i: (i, 0)),
                  pl.BlockSpec((1, D_MODEL), lambda i: (0, 0)),
                  pl.BlockSpec((D_MODEL, Z_TOTAL), lambda i: (0, 0)),
                  pl.BlockSpec((D_MODEL, GATE_W), lambda i: (0, 0))],
        out_specs=[pl.BlockSpec((tm, wd), lambda i: (i, 0)) for wd in Z_WIDTHS],
        out_shape=[jax.ShapeDtypeStruct((n, wd), F32) for wd in Z_WIDTHS],
        compiler_params=_params("parallel"),
        name="inproj",
    )(x2d, g, w, wg_lo)


ATTN_ROWS = MAX_WINDOW
LANES = 128
UNITS_PER_ITER = 4


def _attn_body(q0, q1, kp0, kp1, kc0, kc1, vp0, vp1, vc0, vc1, tab_ref, o_ref, os_ref, ls_ref):
    first = pl.program_id(1) == 0
    low_head = lax.broadcasted_iota(jnp.int32, (CHUNK, LANES), 1) < HEAD_DIM
    ones = jnp.ones((2 * CHUNK, LANES), BF16)
    q_refs, kp_refs, kc_refs, vp_refs, vc_refs = (q0, q1), (kp0, kp1), (kc0, kc1), (vp0, vp1), (vc0, vc1)

    def rows_of(start, d):
        if d == 1:
            return pl.ds(start if isinstance(start, int) else pl.multiple_of(start, CHUNK), CHUNK)
        return pl.ds(start, CHUNK, stride=d)

    def unit(p, d, r, c, prev_in_block):
        start = r + d * CHUNK * c
        cur = rows_of(start, d)
        prev = rows_of(start - d * CHUNK, d) if prev_in_block else rows_of(r + ATTN_ROWS - d * CHUNK, d)
        k_prev_ref, v_prev_ref = (kc_refs, vc_refs) if prev_in_block else (kp_refs, vp_refs)
        scores, vws = [], []
        for half in range(2):
            q = q_refs[half][0, cur, :] * ATTN_SCALE
            kw = jnp.concatenate([k_prev_ref[half][0, prev, :], kc_refs[half][0, cur, :]], axis=0).astype(BF16)
            vws.append(jnp.concatenate([v_prev_ref[half][0, prev, :], vc_refs[half][0, cur, :]], axis=0).astype(BF16))
            qm = jnp.concatenate([jnp.where(low_head, q, 0.0), jnp.where(low_head, 0.0, q)], axis=0).astype(BF16)
            scores.append(_dot_nt(qm, kw))
        s = jnp.concatenate(scores, axis=0) + tab_ref[p]
        if not prev_in_block:
            s = jnp.concatenate([jnp.where(first, NEG, s[:, :CHUNK]), s[:, CHUNK:]], axis=1)
        m = jnp.max(jnp.maximum(s[:, :CHUNK], s[:, CHUNK:]), axis=1, keepdims=True)
        e = jnp.exp(s - m).astype(BF16)
        for half in range(2):
            rows = slice(2 * half * CHUNK, (2 * half + 2) * CHUNK)
            od = _dot(e[rows], jnp.concatenate([vws[half], ones], axis=1))
            den = od[:, LANES:]
            o = od[:, :LANES] / den
            lse = m[rows] + jnp.log(den)
            os_ref[p, half, cur, :] = jnp.where(low_head, o[:CHUNK], o[CHUNK:])
            ls_ref[p, half, cur, :] = jnp.where(low_head, lse[:CHUNK], lse[CHUNK:])

    for p, (_, d) in enumerate(DILATIONS):
        nblk = ATTN_ROWS // d // CHUNK
        if nblk >= UNITS_PER_ITER:
            def per_subsequence(r, carry, p=p, d=d, nblk=nblk):
                for c in range(UNITS_PER_ITER):
                    unit(p, d, r, c, c > 0)

                def group(g, carry2):
                    for u in range(UNITS_PER_ITER):
                        unit(p, d, r, g * UNITS_PER_ITER + u, True)
                    return carry2
                return lax.fori_loop(1, nblk // UNITS_PER_ITER, group, carry)
            if d == 1:
                per_subsequence(0, 0)
            else:
                lax.fori_loop(0, d, per_subsequence, 0)
        else:
            def group(g, carry, p=p, d=d):
                for u in range(UNITS_PER_ITER):
                    unit(p, d, g * UNITS_PER_ITER + u, 0, False)
                return carry
            lax.fori_loop(0, d // UNITS_PER_ITER, group, 0)

    def merge(i, carry):
        rows = pl.ds(pl.multiple_of(i * CHUNK, CHUNK), CHUNK)
        for half in range(2):
            ls = [ls_ref[p, half, rows, :] for p in range(len(DILATIONS))]
            mx = jnp.maximum(jnp.maximum(ls[0], ls[1]), ls[2])
            es = [jnp.exp(l - mx) for l in ls]
            num = es[0] * os_ref[0, half, rows, :] + es[1] * os_ref[1, half, rows, :] + es[2] * os_ref[2, half, rows, :]
            o_ref[0, rows, half * LANES:(half + 1) * LANES] = num / (es[0] + es[1] + es[2])
        return carry
    lax.fori_loop(0, ATTN_ROWS // CHUNK, merge, 0)


def _attn_prompt(za, tabs):
    B, T, _ = za.shape
    blk = (1, ATTN_ROWS, LANES)
    cur = lambda j: pl.BlockSpec(blk, lambda b, i: (b, i, j))
    prv = lambda j: pl.BlockSpec(blk, lambda b, i: (b, jnp.maximum(i - 1, 0), j))
    npat = len(DILATIONS)
    return pl.pallas_call(
        _attn_body,
        grid=(B, T // ATTN_ROWS),
        in_specs=[cur(0), cur(1), prv(2), prv(3), cur(2), cur(3), prv(4), prv(5), cur(4), cur(5),
                  pl.BlockSpec((npat, N_HEADS * CHUNK, 2 * CHUNK), lambda b, i: (0, 0, 0))],
        out_specs=pl.BlockSpec((1, ATTN_ROWS, MIX_W), lambda b, i: (b, i, 0)),
        out_shape=jax.ShapeDtypeStruct((B, T, MIX_W), F32),
        scratch_shapes=[pltpu.VMEM((npat, 2, ATTN_ROWS, LANES), F32)] * 2,
        compiler_params=_params("parallel", "arbitrary"),
        name="attn_prompt",
    )(*([za] * 10), tabs.reshape(npat, N_HEADS * CHUNK, 2 * CHUNK))


SAMPLE_ROWS = 8


def _attn_sample_body(q_ref, tpast_ref, tnew_ref, *rest, n_new, n_earlier, write_windows):
    kt_ref, vt_ref, knt_ref, vnt_ref = rest[:4]
    earlier = [rest[4 + 4 * e:8 + 4 * e] for e in range(n_earlier)]
    outs = rest[4 + 4 * n_earlier:]
    o_ref = outs[0]
    npat = len(DILATIONS)
    q8 = q_ref[0] * ATTN_SCALE
    for h in range(N_HEADS):
        sl = _head(h)
        qh = q8[:, sl].astype(BF16)
        kt, vt = kt_ref[0, 0, h], vt_ref[0, 0, h]
        knt, vnt = knt_ref[0, h], vnt_ref[0, h]
        sp = _dot(qh, kt.astype(BF16))
        sn = _dot(qh, knt.astype(BF16))
        pps, pns, dens, lses = [], [], [], []
        for p in range(npat):
            lp = sp + tpast_ref[p, h]
            ln = sn + tnew_ref[p, h]
            m = jnp.maximum(jnp.max(lp, axis=1, keepdims=True), jnp.max(ln, axis=1, keepdims=True))
            pps.append(jnp.exp(lp - m))
            pns.append(jnp.exp(ln - m))
            dens.append(jnp.sum(pps[-1], axis=1, keepdims=True) + jnp.sum(pns[-1], axis=1, keepdims=True))
            lses.append(m + jnp.log(dens[-1]))
        pv = (_dot_nt(jnp.concatenate(pps, axis=0).astype(BF16), vt.astype(BF16))
              + _dot_nt(jnp.concatenate(pns, axis=0).astype(BF16), vnt.astype(BF16)))
        mx = jnp.maximum(jnp.maximum(lses[0], lses[1]), lses[2])
        es = [jnp.exp(l - mx) for l in lses]
        num = sum(es[p] * pv[p * SAMPLE_ROWS:(p + 1) * SAMPLE_ROWS] / dens[p] for p in range(npat))
        o_ref[0, :, sl] = num / (es[0] + es[1] + es[2])
        if write_windows:
            slabs = earlier + [(kt_ref, vt_ref, knt_ref, vnt_ref)]
            for layer, (kt_l, vt_l, knt_l, vnt_l) in enumerate(slabs):
                outs[1][layer, 0, h] = jnp.concatenate([kt_l[0, 0, h][:, n_new:], knt_l[0, h][:, :n_new]], axis=1)
                outs[2][layer, 0, h] = jnp.concatenate([vt_l[0, 0, h][:, n_new:], vnt_l[0, h][:, :n_new]], axis=1)


def _attn_sample(layer, q8, kt_all, vt_all, new_rows, tpast, tnew, n_new):
    depth, B, _, _, P = kt_all.shape
    npat = len(DILATIONS)
    write_windows = layer == depth - 1
    n_earlier = layer if write_windows else 0
    cache = lambda l: pl.BlockSpec((1, 1, N_HEADS, HEAD_DIM, P), lambda b: (l, b, 0, 0, 0))
    new = pl.BlockSpec((1, N_HEADS, HEAD_DIM, LANES), lambda b: (b, 0, 0, 0))
    rows = pl.BlockSpec((1, SAMPLE_ROWS, MIX_W), lambda b: (b, 0, 0))
    in_specs = [rows,
                pl.BlockSpec((npat, N_HEADS, SAMPLE_ROWS, P), lambda b: (0, 0, 0, 0)),
                pl.BlockSpec((npat, N_HEADS, SAMPLE_ROWS, LANES), lambda b: (0, 0, 0, 0))]
    args = [q8, tpast, tnew]
    for l in [layer] + list(range(n_earlier)):
        in_specs += [cache(l), cache(l), new, new]
        args += [kt_all, vt_all, *new_rows[l]]
    out_specs = [rows]
    out_shape = [jax.ShapeDtypeStruct((B, SAMPLE_ROWS, MIX_W), F32)]
    if write_windows:
        out_specs += [pl.BlockSpec((depth, 1, N_HEADS, HEAD_DIM, P), lambda b: (0, b, 0, 0, 0))] * 2
        out_shape += [jax.ShapeDtypeStruct(kt_all.shape, F32)] * 2
    return pl.pallas_call(
        functools.partial(_attn_sample_body, n_new=n_new, n_earlier=n_earlier, write_windows=write_windows),
        grid=(B,),
        in_specs=in_specs,
        out_specs=out_specs,
        out_shape=out_shape,
        compiler_params=_params("parallel"),
        name="attn_sample",
    )(*args)


def _mlstm_body(zq_ref, zk_ref, zv_ref, zo_ref, zg_ref, bias_ref, gn_ref, c0_ref, n0_ref, m0_ref,
                out_ref, c_out, n_out, m_out, c_s, n_s, m_s, *pads, n_valid):
    c = pl.program_id(1)
    nseq = zq_ref.shape[0]
    pq, pk, pv, po, pg = pads if pads else (None,) * 5
    _zero_pads(pads)

    @pl.when(c == 0)
    def _():
        c_s[...] = jnp.zeros_like(c_s)
        for i in range(nseq):
            for h in range(N_HEADS):
                c_s[i, _head(h), _head(h)] = c0_ref[i, h]
        n_s[...] = n0_ref[...]
        m_s[...] = m0_ref[...]

    rows4 = N_HEADS * CHUNK
    t4 = lax.broadcasted_iota(jnp.int32, (rows4, CHUNK), 0) & (CHUNK - 1)
    causal4 = t4 >= lax.broadcasted_iota(jnp.int32, (rows4, CHUNK), 1)
    valid4 = t4[:, 0:1] < n_valid
    tril = (lax.broadcasted_iota(jnp.int32, (CHUNK, CHUNK), 0)
            >= lax.broadcasted_iota(jnp.int32, (CHUNK, CHUNK), 1)).astype(BF16)
    lane_head = lax.broadcasted_iota(jnp.int32, (CHUNK, MIX_W), 1) // HEAD_DIM
    same_head = (lax.broadcasted_iota(jnp.int32, (MIX_W, MIX_W), 0) // HEAD_DIM
                 == lax.broadcasted_iota(jnp.int32, (MIX_W, MIX_W), 1) // HEAD_DIM)
    head_ones = same_head.astype(BF16)
    ones_l = jnp.ones((CHUNK, MIX_W), BF16)
    ones_w = jnp.ones((MIX_W, MIX_W), BF16)
    last = n_valid - 1

    def stack(f):
        return jnp.concatenate([f(h) for h in range(N_HEADS)], axis=0)

    def on_head_lanes(f, lanes):
        out = f(N_HEADS - 1)
        for h in range(N_HEADS - 2, -1, -1):
            out = jnp.where(lanes == h, f(h), out)
        return out

    for i in range(nseq):
        gates = _chunk_tile(zg_ref, i, pg)
        ipre = gates + bias_ref[0:1, :]
        logf = _log_sigmoid(gates + bias_ref[1:2, :])
        bcum = _dot_01(tril, logf)
        bcum_t = bcum.T
        ipre_t = ipre.T
        q = _chunk_tile(zq_ref, i, pq)
        k = _chunk_tile(zk_ref, i, pk) * ATTN_SCALE
        v = _chunk_tile(zv_ref, i, pv)
        kb, vb = k.astype(BF16), v.astype(BF16)
        cbd = c_s[i]
        nrow = n_s[i]
        mprev = m_s[i]
        bcol = stack(lambda h: bcum[:, N_HEADS + h:N_HEADS + h + 1])
        icol = stack(lambda h: ipre[:, h:h + 1])
        brow = stack(lambda h: jnp.broadcast_to(bcum_t[N_HEADS + h:N_HEADS + h + 1, :], (CHUNK, CHUNK)))
        irow = stack(lambda h: jnp.broadcast_to(ipre_t[h:h + 1, :], (CHUNK, CHUNK)))
        mprev4 = stack(lambda h: jnp.broadcast_to(mprev[:, h:h + 1], (CHUNK, 1)))
        dmat = jnp.where(causal4, bcol - brow + irow, NEG)
        g = bcol + mprev4
        mt = jnp.maximum(g, jnp.max(dmat, axis=1, keepdims=True))
        dexp = jnp.exp(dmat - mt)
        gexp = jnp.exp(g - mt)
        qm_f = stack(lambda h: jnp.where(lane_head == h, q, 0.0))
        qm = qm_f.astype(BF16)
        sb = (_head_scores(q, k) * dexp).astype(BF16)
        num = _dot(sb, vb) + gexp * _dot_nt(qm, cbd.astype(BF16))
        nq = _dot(sb, ones_l) + gexp * _dot((qm_f * nrow).astype(BF16), ones_w)
        hh4 = num / jnp.maximum(jnp.abs(nq), jnp.exp(-mt))
        hh = on_head_lanes(lambda h: hh4[h * CHUNK:(h + 1) * CHUNK], lane_head)
        sq = hh * hh
        sq_hi = sq.astype(BF16)
        sq_lo = (sq - sq_hi.astype(F32)).astype(BF16)
        ms = (_dot(sq_hi, head_ones) + _dot(sq_lo, head_ones)) * (1.0 / HEAD_DIM)
        out = jax.nn.sigmoid(_chunk_tile(zo_ref, i, po)) * (hh * lax.rsqrt(ms + EPS) * gn_ref[...])
        out_ref[i] = out[0:out_ref.shape[1]]

        mnew = [mt[h * CHUNK + last:h * CHUNK + last + 1, :] for h in range(N_HEADS)]
        blast = [bcol[h * CHUNK + last:h * CHUNK + last + 1, :] for h in range(N_HEADS)]
        wk = jnp.exp(stack(lambda h: jnp.broadcast_to(blast[h] - mnew[h], (CHUNK, 1))) - bcol + icol)
        if n_valid < CHUNK:
            wk = jnp.where(valid4, wk, 0.0)
        wk_w = on_head_lanes(lambda h: jnp.broadcast_to(wk[h * CHUNK:(h + 1) * CHUNK], (CHUNK, MIX_W)), lane_head)
        dc = on_head_lanes(lambda h: jnp.broadcast_to(jnp.exp(blast[h] + mprev[:, h:h + 1] - mnew[h]), (1, MIX_W)),
                           lane_head[0:1])
        c_s[i] = dc * cbd + jnp.where(same_head, _dot_tn((v * wk_w).astype(BF16), kb), 0.0)
        n_s[i] = dc * nrow + jnp.sum(wk_w * k, axis=0, keepdims=True)
        for h in range(N_HEADS):
            m_s[i, :, h:h + 1] = mnew[h]

    @pl.when(c == pl.num_programs(1) - 1)
    def _():
        for i in range(nseq):
            for h in range(N_HEADS):
                c_out[i, h] = c_s[i, _head(h), _head(h)]
        n_out[...] = n_s[...]
        m_out[...] = m_s[...]


def _mlstm(zb, zg, bias, gn, c0, n0, m0, n_valid, nseq):
    B, T, _ = zb.shape
    rows = min(T, CHUNK)
    nc = T // rows
    blk = (nseq, rows, MIX_W)
    zspec = lambda j: pl.BlockSpec(blk, lambda b, c: (b, c, j))
    pads = [pltpu.VMEM((CHUNK, wd), F32) for wd in (MIX_W,) * 4 + (GATE_W,)] if rows < CHUNK else []
    st4 = pl.BlockSpec((nseq, N_HEADS, HEAD_DIM, HEAD_DIM), lambda b, c: (b, 0, 0, 0))
    st3 = pl.BlockSpec((nseq, 1, MIX_W), lambda b, c: (b, 0, 0))
    st2 = pl.BlockSpec((nseq, 1, N_HEADS), lambda b, c: (b, 0, 0))
    return pl.pallas_call(
        functools.partial(_mlstm_body, n_valid=n_valid),
        grid=(B // nseq, nc),
        in_specs=[zspec(0), zspec(1), zspec(2), zspec(3),
                  pl.BlockSpec((nseq, rows, GATE_W), lambda b, c: (b, c, 0)),
                  pl.BlockSpec((2, GATE_W), lambda b, c: (0, 0)),
                  pl.BlockSpec((1, MIX_W), lambda b, c: (0, 0)),
                  st4, st3, st2],
        out_specs=[pl.BlockSpec(blk, lambda b, c: (b, c, 0)), st4, st3, st2],
        out_shape=[jax.ShapeDtypeStruct((B, T, MIX_W), F32),
                   jax.ShapeDtypeStruct((B, N_HEADS, HEAD_DIM, HEAD_DIM), F32),
                   jax.ShapeDtypeStruct((B, 1, MIX_W), F32),
                   jax.ShapeDtypeStruct((B, 1, N_HEADS), F32)],
        scratch_shapes=[pltpu.VMEM((nseq, MIX_W, MIX_W), F32),
                        pltpu.VMEM((nseq, 1, MIX_W), F32),
                        pltpu.VMEM((nseq, 1, N_HEADS), F32)] + pads,
        compiler_params=_params("parallel", "arbitrary"),
        name="mlstm",
    )(zb, zb, zb, zb, zg, bias, gn, c0, n0, m0)


def _gate_body(zc_ref, w_ref, bs_ref, gcv_ref, *rest, want_vrows):
    out_ref = rest[0]
    vrow_ref = rest[1] if want_vrows else None
    pad = rest[-1] if zc_ref.shape[1] < CHUNK else None
    if pad is not None:
        _zero_pads([pad])
    row = lax.broadcasted_iota(jnp.int32, (CHUNK, CHUNK), 0)
    col = lax.broadcasted_iota(jnp.int32, (CHUNK, CHUNK), 1)
    lane_head = lax.broadcasted_iota(jnp.int32, (CHUNK, MIX_W), 1) // HEAD_DIM
    ws = [jnp.where(row >= col, w_ref[h], 0.0).astype(BF16) for h in range(N_HEADS)]
    rows_per_chunk = min(zc_ref.shape[1], CHUNK)
    for i in range(zc_ref.shape[0]):
        for j in range(zc_ref.shape[1] // rows_per_chunk):
            rows = slice(j * rows_per_chunk, (j + 1) * rows_per_chunk)
            z = zc_ref[i, rows, :] if pad is None else _chunk_tile(zc_ref, i, pad)
            u = jax.nn.gelu(z[:, :MIX_W])
            vn = _rms(jax.nn.gelu(z[:, MIX_W:]), gcv_ref[...])
            if want_vrows:
                vrow_ref[i, rows, :] = vn[0:rows_per_chunk]
            vb = vn.astype(BF16)
            s = _dot(ws[N_HEADS - 1], vb) + bs_ref[:, N_HEADS - 1:N_HEADS]
            for h in range(N_HEADS - 2, -1, -1):
                s = jnp.where(lane_head == h, _dot(ws[h], vb) + bs_ref[:, h:h + 1], s)
            out_ref[i, rows, :] = (u * s)[0:rows_per_chunk]


def _gate(zc, w_s, bs_t, gcv, nseq, rows, want_vrows):
    B, T, _ = zc.shape
    blk = (nseq, rows, MIX_W)
    n_out = 2 if want_vrows else 1
    return pl.pallas_call(
        functools.partial(_gate_body, want_vrows=want_vrows),
        grid=(B // nseq, T // rows),
        in_specs=[pl.BlockSpec((nseq, rows, 2 * MIX_W), lambda b, c: (b, c, 0)),
                  pl.BlockSpec((N_HEADS, CHUNK, CHUNK), lambda b, c: (0, 0, 0)),
                  pl.BlockSpec((CHUNK, N_HEADS), lambda b, c: (0, 0)),
                  pl.BlockSpec((1, MIX_W), lambda b, c: (0, 0))],
        out_specs=[pl.BlockSpec(blk, lambda b, c: (b, c, 0))] * n_out,
        out_shape=[jax.ShapeDtypeStruct((B, T, MIX_W), F32)] * n_out,
        scratch_shapes=[pltpu.VMEM((CHUNK, 2 * MIX_W), F32)] if rows < CHUNK else [],
        compiler_params=_params("parallel", "parallel"),
        name="gate",
    )(zc, w_s, bs_t, gcv)


def _hgrn_levels(n_valid):
    return [l for l in range(N_LEVELS) if (CHUNK >> (l + 1)) < max(n_valid, 2)]


def _hgrn_tables(n_valid):
    p = np.arange(CHUNK)[:, None]
    u = np.arange(CHUNK)[None, :]
    mats = []
    for l in _hgrn_levels(n_valid):
        m = CHUNK >> (l + 1)
        start = (p // m) * m
        odd = ((p // m) % 2) == 1
        mats.append(np.where(odd, (u >= start) & (u <= p), (u > p) & (u <= start + m - 1)))
    mats.append(u <= p)
    mats.append((u > p) & (u <= n_valid - 1))
    mall = np.concatenate(mats, axis=0).astype(np.float32)
    t = np.arange(CHUNK)[:, None]
    s = np.arange(CHUNK)[None, :]
    x = t ^ s
    top = np.floor(np.log2(np.maximum(x, 1))).astype(np.int32)
    lvl = np.where(s < t, N_LEVELS - 1 - top, np.where(s == t, N_LEVELS, N_LEVELS + 1)).astype(np.int32)
    return jnp.asarray(mall, BF16), jnp.asarray(np.tile(lvl, (N_HEADS, 1)))


def _hgrn_body(zq_ref, zf_ref, zi_ref, zg_ref, lb_ref, gn_ref, mall_ref, lvl_ref, s0_ref,
               out_ref, s_out, s_s, *pads, n_valid):
    c = pl.program_id(1)
    nseq = zq_ref.shape[0]
    pq, pf, pi, pg = pads if pads else (None,) * 4
    _zero_pads(pads)

    @pl.when(c == 0)
    def _():
        s_s[...] = jnp.zeros_like(s_s)
        for i in range(nseq):
            for h in range(N_HEADS):
                s_s[i, _head(h), _head(h)] = s0_ref[i, h]

    lb = lb_ref[...]
    a = jnp.log(jnp.maximum(lb, LB_FLOOR))
    lvl = lvl_ref[...]
    valid_rows = lax.broadcasted_iota(jnp.int32, (CHUNK, 1), 0) < n_valid
    lane_head = lax.broadcasted_iota(jnp.int32, (CHUNK, MIX_W), 1) // HEAD_DIM
    same_head = (lax.broadcasted_iota(jnp.int32, (MIX_W, MIX_W), 0) // HEAD_DIM
                 == lax.broadcasted_iota(jnp.int32, (MIX_W, MIX_W), 1) // HEAD_DIM)
    head_ones = same_head.astype(BF16)
    levels = _hgrn_levels(n_valid)

    def by_head(x):
        return jnp.concatenate([jnp.where(lane_head == h, x, 0.0) for h in range(N_HEADS)], axis=0).astype(BF16)

    for i in range(nseq):
        q = _chunk_tile(zq_ref, i, pq)
        fx = _chunk_tile(zf_ref, i, pf)
        ct = jnp.log1p(-lb) + _log_sigmoid(fx)
        logf = jnp.maximum(a, ct) + jnp.log1p(jnp.exp(-jnp.abs(a - ct)))
        kd = (1.0 - lb) * jax.nn.sigmoid(-fx)
        gsum = _dot_01(mall_ref[...], logf, pieces=2)
        amat = None
        for j, l in enumerate(levels):
            e = jnp.exp(gsum[j * CHUNK:(j + 1) * CHUNK])
            amat = jnp.where(lvl == l, _head_scores(q * e, kd * e), 0.0 if j == 0 else amat)
        amat = jnp.where(lvl == N_LEVELS, _head_scores(q, kd), amat)
        vb = _chunk_tile(zi_ref, i, pi).astype(BF16)
        bcum = gsum[len(levels) * CHUNK:(len(levels) + 1) * CHUNK]
        k_out = jnp.where(valid_rows, kd * jnp.exp(gsum[(len(levels) + 1) * CHUNK:]), 0.0).astype(BF16)
        e_last = jnp.exp(bcum[n_valid - 1:n_valid, :])
        sbd = s_s[i]
        o4 = _dot(amat.astype(BF16), vb) + _dot_nt(by_head(q * jnp.exp(bcum)), sbd.astype(BF16))
        o = o4[(N_HEADS - 1) * CHUNK:]
        for h in range(N_HEADS - 2, -1, -1):
            o = jnp.where(lane_head == h, o4[h * CHUNK:(h + 1) * CHUNK], o)
        sq = o * o
        sq_hi = sq.astype(BF16)
        sq_lo = (sq - sq_hi.astype(F32)).astype(BF16)
        ms = (_dot(sq_hi, head_ones) + _dot(sq_lo, head_ones)) * (1.0 / HEAD_DIM)
        gate = _chunk_tile(zg_ref, i, pg)
        out = o * lax.rsqrt(ms + EPS) * gn_ref[...] * (gate * jax.nn.sigmoid(gate))
        out_ref[i] = out[0:out_ref.shape[1]]
        s_s[i] = e_last * sbd + jnp.where(same_head, _dot_tn(vb, k_out), 0.0)

    @pl.when(c == pl.num_programs(1) - 1)
    def _():
        for i in range(nseq):
            for h in range(N_HEADS):
                s_out[i, h] = s_s[i, _head(h), _head(h)]


def _hgrn(zd, lb, gn, s0_t, n_valid, nseq):
    B, T, _ = zd.shape
    mall, lvl = _hgrn_tables(n_valid)
    rows = min(T, CHUNK)
    blk = (nseq, rows, MIX_W)
    zspec = lambda j: pl.BlockSpec(blk, lambda b, c: (b, c, j))
    pads = [pltpu.VMEM((CHUNK, MIX_W), F32)] * 4 if rows < CHUNK else []
    st4 = pl.BlockSpec((nseq, N_HEADS, HEAD_DIM, HEAD_DIM), lambda b, c: (b, 0, 0, 0))
    vec = pl.BlockSpec((1, MIX_W), lambda b, c: (0, 0))
    return pl.pallas_call(
        functools.partial(_hgrn_body, n_valid=n_valid),
        grid=(B // nseq, T // rows),
        in_specs=[zspec(0), zspec(1), zspec(2), zspec(3), vec, vec,
                  pl.BlockSpec(mall.shape, lambda b, c: (0, 0)),
                  pl.BlockSpec(lvl.shape, lambda b, c: (0, 0)),
                  st4],
        out_specs=[pl.BlockSpec(blk, lambda b, c: (b, c, 0)), st4],
        out_shape=[jax.ShapeDtypeStruct((B, T, MIX_W), F32),
                   jax.ShapeDtypeStruct((B, N_HEADS, HEAD_DIM, HEAD_DIM), F32)],
        scratch_shapes=[pltpu.VMEM((nseq, MIX_W, MIX_W), F32)] + pads,
        compiler_params=_params("parallel", "arbitrary"),
        name="hgrn",
    )(zd, zd, zd, zd, lb, gn, mall, lvl, s0_t)


def _post_body(x_ref, oa_ref, ob_ref, oc_ref, od_ref, wout_hbm, gm_ref, wup_hbm, wdn_hbm, gf_ref, y_ref,
               wout_ref, wup_ref, wdn_ref, h_s, acc_s, *, tf, final):
    @pl.when(pl.program_id(0) == 0)
    def _():
        pltpu.sync_copy(wout_hbm, wout_ref)
        pltpu.sync_copy(wup_hbm, wup_ref)
        pltpu.sync_copy(wdn_hbm, wdn_ref)

    mix = jnp.concatenate([o_ref[...].astype(BF16) for o_ref in (oa_ref, ob_ref, oc_ref, od_ref)], axis=1)
    x1 = x_ref[...] + _dot(mix, wout_ref[...])
    acc_s[...] = x1
    h_s[...] = _rms(x1, gm_ref[...]).astype(BF16)
    for j in range(D_FF // tf):
        up = jnp.maximum(_dot(h_s[...], wup_ref[:, j * tf:(j + 1) * tf]), 0.0)
        acc_s[...] += _dot((up * up).astype(BF16), wdn_ref[j * tf:(j + 1) * tf, :])
    y_ref[...] = _rms(acc_s[...], gf_ref[...]) if final else acc_s[...]


def _post(x2d, oa, ob, oc, od, wout, gm, wup, wdn, gf, tm, tf, final):
    n = x2d.shape[0]
    row = lambda wd: pl.BlockSpec((tm, wd), lambda i: (i, 0))
    vec = pl.BlockSpec((1, D_MODEL), lambda i: (0, 0))
    hbm = pl.BlockSpec(memory_space=pl.ANY)
    return pl.pallas_call(
        functools.partial(_post_body, tf=tf, final=final),
        grid=(n // tm,),
        in_specs=[row(D_MODEL)] + [row(MIX_W)] * 4 + [hbm, vec, hbm, hbm, vec],
        out_specs=row(D_MODEL),
        out_shape=jax.ShapeDtypeStruct((n, D_MODEL), F32),
        scratch_shapes=[pltpu.VMEM(wout.shape, BF16), pltpu.VMEM(wup.shape, BF16), pltpu.VMEM(wdn.shape, BF16),
                        pltpu.VMEM((tm, D_MODEL), BF16), pltpu.VMEM((tm, D_MODEL), F32)],
        compiler_params=_params("arbitrary"),
        name="post",
    )(x2d, oa, ob, oc, od, wout, gm, wup, wdn, gf)


def _rel_bucket(dist):
    max_exact = N_BUCKETS // 2
    d = jnp.maximum(dist, 1).astype(F32)
    large = max_exact + (jnp.log(d / max_exact) / math.log(MAX_WINDOW / max_exact)
                         * (N_BUCKETS - max_exact)).astype(jnp.int32)
    large = jnp.clip(large, max_exact, N_BUCKETS - 1)
    return jnp.where(dist < max_exact, dist, large)


def _pattern_bias(rel_bias, w, d):
    offs = jnp.arange(w // d + 1, dtype=jnp.int32) * d
    return rel_bias[_rel_bucket(offs)].T.astype(F32)


def _prompt_table(bias):
    cols = 2 * CHUNK
    u = jnp.concatenate([bias[:, ::-1], jnp.full((N_HEADS, cols - CHUNK), NEG, F32)], axis=1)
    return jnp.tile(u, (1, CHUNK))[:, :CHUNK * cols].reshape(N_HEADS, CHUNK, cols)


def _sample_tables(bias, d, n_new, past):
    comb = jnp.concatenate([bias[:, :, None], jnp.full((N_HEADS, CHUNK + 1, d - 1), NEG, F32)], axis=2)
    comb = comb.reshape(N_HEADS, (CHUNK + 1) * d)
    length = past + SAMPLE_ROWS + 1
    comb = comb[:, :length]
    comb = jnp.pad(comb, ((0, 0), (0, length - comb.shape[1])), constant_values=NEG)
    rev = comb[:, ::-1]
    tpast, tnew = [], []
    for row in range(SAMPLE_ROWS):
        t = min(row, n_new - 1)
        start = length - 1 - past - t
        tpast.append(rev[:, start:start + past])
        start = length - 1 - t
        tnew.append(jnp.pad(rev[:, start:start + t + 1], ((0, 0), (0, CHUNK - t - 1)), constant_values=NEG))
    return jnp.stack(tpast, axis=1), jnp.stack(tnew, axis=1)


def _pad_rows(a, rows):
    return jnp.pad(a, ((0, 0), (0, rows - a.shape[1]), (0, 0)))


def kernel(x_prompt, x_sample, cache_k_win, cache_v_win, state_mlstm_C, state_mlstm_n, state_mlstm_m, state_hgrn_S, rel_bias, w_in, w_out, g_attn, g_mlp, w_up, w_down, b_i, b_f, g_mlstm, g_cv, w_s, b_s, hgrn_lb, g_hgrn, g_final):
    depth = w_in.shape[0]
    B, T, _ = x_prompt.shape
    Bs, Ts, _ = x_sample.shape
    past = cache_k_win.shape[2]
    assert past == MAX_WINDOW and T % (MAX_WINDOW) == 0 and Ts <= SAMPLE_ROWS
    H, dh = N_HEADS, HEAD_DIM
    keep_p = min(MAX_WINDOW, T)

    sm = jax.nn.softmax(hgrn_lb.astype(F32), axis=0)
    lb_all = jnp.cumsum(sm, axis=0) - sm[0:1]
    biases = [_pattern_bias(rel_bias, w, d) for w, d in DILATIONS]
    tabs_p = jnp.stack([_prompt_table(bb) for bb in biases])
    tabs_s = [_sample_tables(bb, d, Ts, past) for bb, (_, d) in zip(biases, DILATIONS)]
    tpast = jnp.stack([a for a, _ in tabs_s])
    tnew = jnp.stack([b for _, b in tabs_s])
    kt_all = jnp.transpose(cache_k_win, (0, 1, 3, 4, 2))
    vt_all = jnp.transpose(cache_v_win, (0, 1, 3, 4, 2))

    a_end = Z_WIDTHS[0] + Z_WIDTHS[1]
    g_end = a_end + N_GATE_COLS
    zeros_c = jnp.zeros((B, H, dh, dh), F32)
    zeros_n = jnp.zeros((B, 1, MIX_W), F32)
    zeros_m = jnp.zeros((B, 1, H), F32)
    vec = lambda a: a.reshape(1, -1).astype(F32)
    pseq = math.gcd(B, PROMPT_SEQS_PER_STEP)
    sseq = math.gcd(Bs, SAMPLE_SEQS_PER_STEP)

    xp = x_prompt.reshape(B * T, D_MODEL)
    xs = x_sample.reshape(Bs * Ts, D_MODEL)
    outs = [[] for _ in range(13)]
    new_rows = []
    for l in range(depth):
        wl = w_in[l]
        w_gate = jnp.pad(wl[:, a_end:g_end], ((0, 0), (0, GATE_W - N_GATE_COLS)))
        w_z = jnp.concatenate([wl[:, :a_end], w_gate, wl[:, g_end:]], axis=1).astype(BF16)
        wg_lo = (w_gate - w_gate.astype(BF16).astype(F32)).astype(BF16)
        wout_b, wup_b, wdn_b = w_out[l].astype(BF16), w_up[l].astype(BF16), w_down[l].astype(BF16)
        gate_bias = jnp.zeros((2, GATE_W), F32).at[0, :H].set(b_i[l]).at[1, H:2 * H].set(b_f[l])
        bs_t = b_s[l].T.astype(F32)
        final = l == depth - 1
        gf = vec(g_final)

        za, zb, zg, zc, zd = _inproj(xp, vec(g_attn[l]), w_z, wg_lo, 512)
        za3 = za.reshape(B, T, -1)
        oa = _attn_prompt(za3, tabs_p)
        ob, c1, n1, m1 = _mlstm(zb.reshape(B, T, -1), zg.reshape(B, T, -1), gate_bias, vec(g_mlstm[l]),
                                zeros_c, zeros_n, zeros_m, CHUNK, pseq)
        (oc,) = _gate(zc.reshape(B, T, -1), w_s[l], bs_t, vec(g_cv[l]), 1, GATE_ROWS, False)
        od, s1 = _hgrn(zd.reshape(B, T, -1), vec(lb_all[l]), vec(g_hgrn[l]), zeros_c, CHUNK, pseq)
        xp = _post(xp, oa.reshape(B * T, -1), ob.reshape(B * T, -1), oc.reshape(B * T, -1), od.reshape(B * T, -1),
                   wout_b, vec(g_mlp[l]), wup_b, wdn_b, gf, 512, 1024, final)
        outs[0].append(za3[:, T - keep_p:, MIX_W:2 * MIX_W].reshape(B, keep_p, H, dh))
        outs[1].append(za3[:, T - keep_p:, 2 * MIX_W:].reshape(B, keep_p, H, dh))
        outs[4].append(c1)
        outs[5].append(n1.reshape(B, H, dh))
        outs[6].append(m1.reshape(B, H))
        outs[10].append(jnp.swapaxes(s1, -1, -2))

        za, zb, zg, zc, zd = _inproj(xs, vec(g_attn[l]), w_z, wg_lo, Bs * Ts)
        za3 = za.reshape(Bs, Ts, -1)
        new_t = lambda a: jnp.pad(jnp.transpose(a.reshape(Bs, Ts, H, dh), (0, 2, 3, 1)),
                                  ((0, 0), (0, 0), (0, 0), (0, LANES - Ts)))
        new_rows.append((new_t(za3[:, :, MIX_W:2 * MIX_W]), new_t(za3[:, :, 2 * MIX_W:])))
        oa, *windows = _attn_sample(l, _pad_rows(za3[:, :, :MIX_W], SAMPLE_ROWS), kt_all, vt_all,
                                    new_rows, tpast, tnew, Ts)
        seqs = lambda z: z.reshape(Bs, Ts, -1)
        ob, c2, n2, m2 = _mlstm(seqs(zb), seqs(zg), gate_bias, vec(g_mlstm[l]),
                                state_mlstm_C[l], state_mlstm_n[l].reshape(Bs, 1, MIX_W),
                                state_mlstm_m[l].reshape(Bs, 1, H), Ts, sseq)
        oc, vrows = _gate(seqs(zc), w_s[l], bs_t, vec(g_cv[l]), sseq, Ts, True)
        od, s2 = _hgrn(seqs(zd), vec(lb_all[l]), vec(g_hgrn[l]), jnp.swapaxes(state_hgrn_S[l], -1, -2), Ts, sseq)
        flat = lambda o: o.reshape(Bs * Ts, MIX_W)
        xs = _post(xs, flat(oa[:, :Ts]), flat(ob), flat(oc), flat(od),
                   wout_b, vec(g_mlp[l]), wup_b, wdn_b, gf, Bs * Ts, 1024, final)
        outs[7].append(c2)
        outs[8].append(n2.reshape(Bs, H, dh))
        outs[9].append(m2.reshape(Bs, H))
        outs[11].append(jnp.swapaxes(s2, -1, -2))
        outs[12].append(vrows.reshape(Bs, Ts, H, dh))

    stacked = [jnp.stack(o) if o else None for o in outs]
    stacked[2], stacked[3] = (jnp.transpose(w, (0, 1, 4, 2, 3)) for w in windows)
    return (xp.reshape(B, T, D_MODEL), xs.reshape(Bs, Ts, D_MODEL)) + tuple(stacked)
```

```python
import functools
import math

import numpy as np
import jax
import jax.numpy as jnp
from jax import lax
from jax.experimental import pallas as pl
from jax.experimental.pallas import tpu as pltpu

F32 = jnp.float32
BF16 = jnp.bfloat16

D_MODEL = 1024
N_HEADS = 4
HEAD_DIM = 64
MIX_W = N_HEADS * HEAD_DIM
DILATIONS = ((128, 1), (512, 4), (2048, 16))
MAX_WINDOW = 2048
N_BUCKETS = 32
D_FF = 4 * D_MODEL
EPS = 1e-6
NEG = -1e30
LB_FLOOR = 1e-30
CHUNK = 128
N_GATE_COLS = 2 * N_HEADS
GATE_W = 128
Z_WIDTHS = (3 * MIX_W, 4 * MIX_W, GATE_W, 2 * MIX_W, 4 * MIX_W)
Z_TOTAL = sum(Z_WIDTHS)
ATTN_SCALE = HEAD_DIM ** -0.5
N_LEVELS = 7
VMEM_LIMIT = 48 * 1024 * 1024
PROMPT_SEQS_PER_STEP = 2
SAMPLE_SEQS_PER_STEP = 4
GATE_ROWS = 4 * CHUNK


def _dot(a, b):
    return jnp.dot(a, b, preferred_element_type=F32)


def _dot_nt(a, b):
    return lax.dot_general(a, b, (((1,), (1,)), ((), ())), preferred_element_type=F32)


def _dot_tn(a, b):
    return lax.dot_general(a, b, (((0,), (0,)), ((), ())), preferred_element_type=F32)


def _rms(x, g):
    return x * lax.rsqrt(jnp.mean(x * x, axis=-1, keepdims=True) + EPS) * g


def _log_sigmoid(x):
    return jnp.minimum(x, 0.0) - jnp.log1p(jnp.exp(-jnp.abs(x)))


def _dot_01(m01, f, pieces=3):
    out = None
    for _ in range(pieces):
        piece = f.astype(BF16)
        f = f - piece.astype(F32)
        out = _dot(m01, piece) if out is None else out + _dot(m01, piece)
    return out


def _head(h):
    return slice(h * HEAD_DIM, (h + 1) * HEAD_DIM)


def _head_scores(x, y):
    rows, width = x.shape
    tile = 2 * HEAD_DIM
    low = lax.broadcasted_iota(jnp.int32, (rows, tile), 1) < HEAD_DIM
    out = []
    for half in range(width // tile):
        lanes = slice(half * tile, (half + 1) * tile)
        xh = x[:, lanes].astype(BF16)
        zero = jnp.zeros_like(xh)
        x2 = jnp.concatenate([jnp.where(low, xh, zero), jnp.where(low, zero, xh)], axis=0)
        out.append(_dot_nt(x2, y[:, lanes].astype(BF16)))
    return jnp.concatenate(out, axis=0)


def _chunk_tile(ref, i, pad_ref):
    if pad_ref is None:
        return ref[i]
    pad_ref[0:ref.shape[1], :] = ref[i]
    return pad_ref[...]


def _zero_pads(pads):
    for p in pads:
        p[...] = jnp.zeros_like(p)


def _params(*sem):
    return pltpu.CompilerParams(dimension_semantics=sem, vmem_limit_bytes=VMEM_LIMIT)


def _inproj_body(x_ref, g_ref, w_ref, wg_lo_ref, *out_refs):
    hn = _rms(x_ref[...], g_ref[...])
    h = hn.astype(BF16)
    off = 0
    for o_ref in out_refs:
        n = o_ref.shape[-1]
        z = _dot(h, w_ref[:, off:off + n])
        if n == GATE_W:
            h_lo = (hn - h.astype(F32)).astype(BF16)
            z = z + _dot(h, wg_lo_ref[...]) + _dot(h_lo, w_ref[:, off:off + n])
        o_ref[...] = z
        off += n


def _inproj(x2d, g, w, wg_lo, tm):
    n = x2d.shape[0]
    return pl.pallas_call(
        _inproj_body,
        grid=(n // tm,),
        in_specs=[pl.BlockSpec((tm, D_MODEL), lambda i: (i, 0)),
                  pl.BlockSpec((1, D_MODEL), lambda i: (0, 0)),
                  pl.BlockSpec((D_MODEL, Z_TOTAL), lambda i: (0, 0)),
                  pl.BlockSpec((D_MODEL, GATE_W), lambda i: (0, 0))],
        out_specs=[pl.BlockSpec((tm, wd), lambda i: (i, 0)) for wd in Z_WIDTHS],
        out_shape=[jax.ShapeDtypeStruct((n, wd), F32) for wd in Z_WIDTHS],
        compiler_params=_params("parallel"),
        name="inproj",
    )(x2d, g, w, wg_lo)


ATTN_ROWS = MAX_WINDOW
LANES = 128
UNITS_PER_ITER = 4


def _attn_body(q0, q1, kp0, kp1, kc0, kc1, vp0, vp1, vc0, vc1, tab_ref, o_ref, os_ref, ls_ref):
    first = pl.program_id(1) == 0
    low_head = lax.broadcasted_iota(jnp.int32, (CHUNK, LANES), 1) < HEAD_DIM
    ones = jnp.ones((2 * CHUNK, LANES), BF16)
    q_refs, kp_refs, kc_refs, vp_refs, vc_refs = (q0, q1), (kp0, kp1), (kc0, kc1), (vp0, vp1), (vc0, vc1)

    def rows_of(start, d):
        if d == 1:
            return pl.ds(start if isinstance(start, int) else pl.multiple_of(start, CHUNK), CHUNK)
        return pl.ds(start, CHUNK, stride=d)

    def unit(p, d, r, c, prev_in_block):
        start = r + d * CHUNK * c
        cur = rows_of(start, d)
        prev = rows_of(start - d * CHUNK, d) if prev_in_block else rows_of(r + ATTN_ROWS - d * CHUNK, d)
        k_prev_ref, v_prev_ref = (kc_refs, vc_refs) if prev_in_block else (kp_refs, vp_refs)
        scores, vws = [], []
        for half in range(2):
            q = (q_refs[half][0, cur, :] * ATTN_SCALE).astype(BF16)
            kw = jnp.concatenate([k_prev_ref[half][0, prev, :], kc_refs[half][0, cur, :]], axis=0).astype(BF16)
            vws.append(jnp.concatenate([v_prev_ref[half][0, prev, :], vc_refs[half][0, cur, :]], axis=0).astype(BF16))
            zero = jnp.zeros_like(q)
            qm = jnp.concatenate([jnp.where(low_head, q, zero), jnp.where(low_head, zero, q)], axis=0)
            scores.append(_dot_nt(qm, kw))
        s = jnp.concatenate(scores, axis=0) + tab_ref[p]
        if not prev_in_block:
            s = jnp.concatenate([jnp.where(first, NEG, s[:, :CHUNK]), s[:, CHUNK:]], axis=1)
        m = jnp.max(jnp.maximum(s[:, :CHUNK], s[:, CHUNK:]), axis=1, keepdims=True)
        e = jnp.exp(s - m).astype(BF16)
        for half in range(2):
            rows = slice(2 * half * CHUNK, (2 * half + 2) * CHUNK)
            od = _dot(e[rows], jnp.concatenate([vws[half], ones], axis=1))
            den = od[:, LANES:]
            o = od[:, :LANES] / den
            lse = m[rows] + jnp.log(den)
            os_ref[p, half, cur, :] = jnp.where(low_head, o[:CHUNK], o[CHUNK:])
            ls_ref[p, half, cur, :] = jnp.where(low_head, lse[:CHUNK], lse[CHUNK:])

    for p, (_, d) in enumerate(DILATIONS):
        nblk = ATTN_ROWS // d // CHUNK
        if nblk >= UNITS_PER_ITER:
            def per_subsequence(r, carry, p=p, d=d, nblk=nblk):
                for c in range(UNITS_PER_ITER):
                    unit(p, d, r, c, c > 0)

                def group(g, carry2):
                    for u in range(UNITS_PER_ITER):
                        unit(p, d, r, g * UNITS_PER_ITER + u, True)
                    return carry2
                return lax.fori_loop(1, nblk // UNITS_PER_ITER, group, carry)
            if d == 1:
                per_subsequence(0, 0)
            else:
                lax.fori_loop(0, d, per_subsequence, 0)
        else:
            def group(g, carry, p=p, d=d):
                for u in range(UNITS_PER_ITER):
                    unit(p, d, g * UNITS_PER_ITER + u, 0, False)
                return carry
            lax.fori_loop(0, d // UNITS_PER_ITER, group, 0)

    def merge(i, carry):
        rows = pl.ds(pl.multiple_of(i * CHUNK, CHUNK), CHUNK)
        for half in range(2):
            ls = [ls_ref[p, half, rows, :] for p in range(len(DILATIONS))]
            mx = jnp.maximum(jnp.maximum(ls[0], ls[1]), ls[2])
            es = [jnp.exp(l - mx) for l in ls]
            num = es[0] * os_ref[0, half, rows, :] + es[1] * os_ref[1, half, rows, :] + es[2] * os_ref[2, half, rows, :]
            o_ref[0, rows, half * LANES:(half + 1) * LANES] = num / (es[0] + es[1] + es[2])
        return carry
    lax.fori_loop(0, ATTN_ROWS // CHUNK, merge, 0)


def _attn_prompt(za, tabs):
    B, T, _ = za.shape
    blk = (1, ATTN_ROWS, LANES)
    cur = lambda j: pl.BlockSpec(blk, lambda b, i: (b, i, j))
    prv = lambda j: pl.BlockSpec(blk, lambda b, i: (b, jnp.maximum(i - 1, 0), j))
    npat = len(DILATIONS)
    return pl.pallas_call(
        _attn_body,
        grid=(B, T // ATTN_ROWS),
        in_specs=[cur(0), cur(1), prv(2), prv(3), cur(2), cur(3), prv(4), prv(5), cur(4), cur(5),
                  pl.BlockSpec((npat, N_HEADS * CHUNK, 2 * CHUNK), lambda b, i: (0, 0, 0))],
        out_specs=pl.BlockSpec((1, ATTN_ROWS, MIX_W), lambda b, i: (b, i, 0)),
        out_shape=jax.ShapeDtypeStruct((B, T, MIX_W), F32),
        scratch_shapes=[pltpu.VMEM((npat, 2, ATTN_ROWS, LANES), F32)] * 2,
        compiler_params=_params("parallel", "arbitrary"),
        name="attn_prompt",
    )(*([za] * 10), tabs.reshape(npat, N_HEADS * CHUNK, 2 * CHUNK))


SAMPLE_ROWS = 8


def _attn_sample_body(q_ref, tpast_ref, tnew_ref, *rest, n_new, n_earlier, write_windows):
    kt_ref, vt_ref, knt_ref, vnt_ref = rest[:4]
    earlier = [rest[4 + 4 * e:8 + 4 * e] for e in range(n_earlier)]
    outs = rest[4 + 4 * n_earlier:]
    o_ref = outs[0]
    npat = len(DILATIONS)
    q8 = q_ref[0] * ATTN_SCALE
    for h in range(N_HEADS):
        sl = _head(h)
        qh = q8[:, sl].astype(BF16)
        kt, vt = kt_ref[0, 0, h], vt_ref[0, 0, h]
        knt, vnt = knt_ref[0, h], vnt_ref[0, h]
        sp = _dot(qh, kt.astype(BF16))
        sn = _dot(qh, knt.astype(BF16))
        pps, pns, dens, lses = [], [], [], []
        for p in range(npat):
            lp = sp + tpast_ref[p, h]
            ln = sn + tnew_ref[p, h]
            m = jnp.maximum(jnp.max(lp, axis=1, keepdims=True), jnp.max(ln, axis=1, keepdims=True))
            pps.append(jnp.exp(lp - m))
            pns.append(jnp.exp(ln - m))
            dens.append(jnp.sum(pps[-1], axis=1, keepdims=True) + jnp.sum(pns[-1], axis=1, keepdims=True))
            lses.append(m + jnp.log(dens[-1]))
        pv = (_dot_nt(jnp.concatenate(pps, axis=0).astype(BF16), vt.astype(BF16))
              + _dot_nt(jnp.concatenate(pns, axis=0).astype(BF16), vnt.astype(BF16)))
        mx = jnp.maximum(jnp.maximum(lses[0], lses[1]), lses[2])
        es = [jnp.exp(l - mx) for l in lses]
        num = sum(es[p] * pv[p * SAMPLE_ROWS:(p + 1) * SAMPLE_ROWS] / dens[p] for p in range(npat))
        o_ref[0, :, sl] = num / (es[0] + es[1] + es[2])
        if write_windows:
            slabs = earlier + [(kt_ref, vt_ref, knt_ref, vnt_ref)]
            for layer, (kt_l, vt_l, knt_l, vnt_l) in enumerate(slabs):
                outs[1][layer, 0, h] = jnp.concatenate([kt_l[0, 0, h][:, n_new:], knt_l[0, h][:, :n_new]], axis=1)
                outs[2][layer, 0, h] = jnp.concatenate([vt_l[0, 0, h][:, n_new:], vnt_l[0, h][:, :n_new]], axis=1)


def _attn_sample(layer, q8, kt_all, vt_all, new_rows, tpast, tnew, n_new):
    depth, B, _, _, P = kt_all.shape
    npat = len(DILATIONS)
    write_windows = layer == depth - 1
    n_earlier = layer if write_windows else 0
    cache = lambda l: pl.BlockSpec((1, 1, N_HEADS, HEAD_DIM, P), lambda b: (l, b, 0, 0, 0))
    new = pl.BlockSpec((1, N_HEADS, HEAD_DIM, LANES), lambda b: (b, 0, 0, 0))
    rows = pl.BlockSpec((1, SAMPLE_ROWS, MIX_W), lambda b: (b, 0, 0))
    in_specs = [rows,
                pl.BlockSpec((npat, N_HEADS, SAMPLE_ROWS, P), lambda b: (0, 0, 0, 0)),
                pl.BlockSpec((npat, N_HEADS, SAMPLE_ROWS, LANES), lambda b: (0, 0, 0, 0))]
    args = [q8, tpast, tnew]
    for l in [layer] + list(range(n_earlier)):
        in_specs += [cache(l), cache(l), new, new]
        args += [kt_all, vt_all, *new_rows[l]]
    out_specs = [rows]
    out_shape = [jax.ShapeDtypeStruct((B, SAMPLE_ROWS, MIX_W), F32)]
    if write_windows:
        out_specs += [pl.BlockSpec((depth, 1, N_HEADS, HEAD_DIM, P), lambda b: (0, b, 0, 0, 0))] * 2
        out_shape += [jax.ShapeDtypeStruct(kt_all.shape, F32)] * 2
    return pl.pallas_call(
        functools.partial(_attn_sample_body, n_new=n_new, n_earlier=n_earlier, write_windows=write_windows),
        grid=(B,),
        in_specs=in_specs,
        out_specs=out_specs,
        out_shape=out_shape,
        compiler_params=_params("parallel"),
        name="attn_sample",
    )(*args)


def _mlstm_body(zq_ref, zk_ref, zv_ref, zo_ref, zg_ref, bias_ref, gn_ref, c0_ref, n0_ref, m0_ref,
                out_ref, c_out, n_out, m_out, c_s, n_s, m_s, *pads, n_valid):
    c = pl.program_id(1)
    nseq = zq_ref.shape[0]
    pq, pk, pv, po, pg = pads if pads else (None,) * 5
    _zero_pads(pads)

    @pl.when(c == 0)
    def _():
        c_s[...] = jnp.zeros_like(c_s)
        for i in range(nseq):
            for h in range(N_HEADS):
                c_s[i, _head(h), _head(h)] = c0_ref[i, h]
        n_s[...] = n0_ref[...]
        m_s[...] = m0_ref[...]

    rows4 = N_HEADS * CHUNK
    t4 = lax.broadcasted_iota(jnp.int32, (rows4, CHUNK), 0) & (CHUNK - 1)
    causal4 = t4 >= lax.broadcasted_iota(jnp.int32, (rows4, CHUNK), 1)
    valid4 = t4[:, 0:1] < n_valid
    tril = (lax.broadcasted_iota(jnp.int32, (CHUNK, CHUNK), 0)
            >= lax.broadcasted_iota(jnp.int32, (CHUNK, CHUNK), 1)).astype(BF16)
    lane_head = lax.broadcasted_iota(jnp.int32, (CHUNK, MIX_W), 1) // HEAD_DIM
    same_head = (lax.broadcasted_iota(jnp.int32, (MIX_W, MIX_W), 0) // HEAD_DIM
                 == lax.broadcasted_iota(jnp.int32, (MIX_W, MIX_W), 1) // HEAD_DIM)
    head_ones = same_head.astype(BF16)
    ones_l = jnp.ones((CHUNK, MIX_W), BF16)
    ones_w = jnp.ones((MIX_W, MIX_W), BF16)
    last = n_valid - 1

    def stack(f):
        return jnp.concatenate([f(h) for h in range(N_HEADS)], axis=0)

    def on_head_lanes(f, lanes):
        out = f(N_HEADS - 1)
        for h in range(N_HEADS - 2, -1, -1):
            out = jnp.where(lanes == h, f(h), out)
        return out

    for i in range(nseq):
        gates = _chunk_tile(zg_ref, i, pg)
        ipre = gates + bias_ref[0:1, :]
        logf = _log_sigmoid(gates + bias_ref[1:2, :])
        bcum = _dot_01(tril, logf)
        bcum_t = bcum.T
        ipre_t = ipre.T
        q = _chunk_tile(zq_ref, i, pq)
        k = _chunk_tile(zk_ref, i, pk) * ATTN_SCALE
        v = _chunk_tile(zv_ref, i, pv)
        kb, vb = k.astype(BF16), v.astype(BF16)
        cbd = c_s[i]
        nrow = n_s[i]
        mprev = m_s[i]
        bcol = stack(lambda h: bcum[:, N_HEADS + h:N_HEADS + h + 1])
        icol = stack(lambda h: ipre[:, h:h + 1])
        brow = stack(lambda h: jnp.broadcast_to(bcum_t[N_HEADS + h:N_HEADS + h + 1, :], (CHUNK, CHUNK)))
        irow = stack(lambda h: jnp.broadcast_to(ipre_t[h:h + 1, :], (CHUNK, CHUNK)))
        mprev4 = stack(lambda h: jnp.broadcast_to(mprev[:, h:h + 1], (CHUNK, 1)))
        dmat = jnp.where(causal4, bcol - brow + irow, NEG)
        g = bcol + mprev4
        mt = jnp.maximum(g, jnp.max(dmat, axis=1, keepdims=True))
        dexp = jnp.exp(dmat - mt)
        gexp = jnp.exp(g - mt)
        qm_f = stack(lambda h: jnp.where(lane_head == h, q, 0.0))
        qm = qm_f.astype(BF16)
        sb = (_head_scores(q, k) * dexp).astype(BF16)
        num = _dot(sb, vb) + gexp * _dot_nt(qm, cbd.astype(BF16))
        nq = _dot(sb, ones_l) + gexp * _dot((qm_f * nrow).astype(BF16), ones_w)
        hh4 = num / jnp.maximum(jnp.abs(nq), jnp.exp(-mt))
        hh = on_head_lanes(lambda h: hh4[h * CHUNK:(h + 1) * CHUNK], lane_head)
        sq = hh * hh
        sq_hi = sq.astype(BF16)
        sq_lo = (sq - sq_hi.astype(F32)).astype(BF16)
        ms = (_dot(sq_hi, head_ones) + _dot(sq_lo, head_ones)) * (1.0 / HEAD_DIM)
        out = jax.nn.sigmoid(_chunk_tile(zo_ref, i, po)) * (hh * lax.rsqrt(ms + EPS) * gn_ref[...])
        out_ref[i] = out[0:out_ref.shape[1]]

        mnew = [mt[h * CHUNK + last:h * CHUNK + last + 1, :] for h in range(N_HEADS)]
        blast = [bcol[h * CHUNK + last:h * CHUNK + last + 1, :] for h in range(N_HEADS)]
        wk = jnp.exp(stack(lambda h: jnp.broadcast_to(blast[h] - mnew[h], (CHUNK, 1))) - bcol + icol)
        if n_valid < CHUNK:
            wk = jnp.where(valid4, wk, 0.0)
        wk_w = on_head_lanes(lambda h: jnp.broadcast_to(wk[h * CHUNK:(h + 1) * CHUNK], (CHUNK, MIX_W)), lane_head)
        dc = on_head_lanes(lambda h: jnp.broadcast_to(jnp.exp(blast[h] + mprev[:, h:h + 1] - mnew[h]), (1, MIX_W)),
                           lane_head[0:1])
        c_s[i] = dc * cbd + jnp.where(same_head, _dot_tn((v * wk_w).astype(BF16), kb), 0.0)
        n_s[i] = dc * nrow + jnp.sum(wk_w * k, axis=0, keepdims=True)
        for h in range(N_HEADS):
            m_s[i, :, h:h + 1] = mnew[h]

    @pl.when(c == pl.num_programs(1) - 1)
    def _():
        for i in range(nseq):
            for h in range(N_HEADS):
                c_out[i, h] = c_s[i, _head(h), _head(h)]
        n_out[...] = n_s[...]
        m_out[...] = m_s[...]


def _mlstm(zb, zg, bias, gn, layer, c0, n0, m0, n_valid, nseq):
    B, T, _ = zb.shape
    rows = min(T, CHUNK)
    nc = T // rows
    blk = (nseq, rows, MIX_W)
    zspec = lambda j: pl.BlockSpec(blk, lambda b, c: (b, c, j))
    pads = [pltpu.VMEM((CHUNK, wd), F32) for wd in (MIX_W,) * 4 + (GATE_W,)] if rows < CHUNK else []
    st4 = pl.BlockSpec((nseq, N_HEADS, HEAD_DIM, HEAD_DIM), lambda b, c: (b, 0, 0, 0))
    st3 = pl.BlockSpec((nseq, 1, MIX_W), lambda b, c: (b, 0, 0))
    st2 = pl.BlockSpec((nseq, 1, N_HEADS), lambda b, c: (b, 0, 0))
    init = lambda spec: pl.BlockSpec((None,) + spec.block_shape,
                                     lambda b, c: (layer, b) + (0,) * (len(spec.block_shape) - 1))
    return pl.pallas_call(
        functools.partial(_mlstm_body, n_valid=n_valid),
        grid=(B // nseq, nc),
        in_specs=[zspec(0), zspec(1), zspec(2), zspec(3),
                  pl.BlockSpec((nseq, rows, GATE_W), lambda b, c: (b, c, 0)),
                  pl.BlockSpec((2, GATE_W), lambda b, c: (0, 0)),
                  pl.BlockSpec((1, MIX_W), lambda b, c: (0, 0)),
                  init(st4), init(st3), init(st2)],
        out_specs=[pl.BlockSpec(blk, lambda b, c: (b, c, 0)), st4, st3, st2],
        out_shape=[jax.ShapeDtypeStruct((B, T, MIX_W), F32),
                   jax.ShapeDtypeStruct((B, N_HEADS, HEAD_DIM, HEAD_DIM), F32),
                   jax.ShapeDtypeStruct((B, 1, MIX_W), F32),
                   jax.ShapeDtypeStruct((B, 1, N_HEADS), F32)],
        scratch_shapes=[pltpu.VMEM((nseq, MIX_W, MIX_W), F32),
                        pltpu.VMEM((nseq, 1, MIX_W), F32),
                        pltpu.VMEM((nseq, 1, N_HEADS), F32)] + pads,
        compiler_params=_params("parallel", "arbitrary"),
        name="mlstm",
    )(zb, zb, zb, zb, zg, bias, gn, c0, n0, m0)


def _gate_body(zc_ref, w_ref, bs_ref, gcv_ref, *rest, want_vrows):
    out_ref = rest[0]
    vrow_ref = rest[1] if want_vrows else None
    pad = rest[-1] if zc_ref.shape[1] < CHUNK else None
    if pad is not None:
        _zero_pads([pad])
    row = lax.broadcasted_iota(jnp.int32, (CHUNK, CHUNK), 0)
    col = lax.broadcasted_iota(jnp.int32, (CHUNK, CHUNK), 1)
    lane_head = lax.broadcasted_iota(jnp.int32, (CHUNK, MIX_W), 1) // HEAD_DIM
    ws = [jnp.where(row >= col, w_ref[h], 0.0).astype(BF16) for h in range(N_HEADS)]
    rows_per_chunk = min(zc_ref.shape[1], CHUNK)
    for i in range(zc_ref.shape[0]):
        for j in range(zc_ref.shape[1] // rows_per_chunk):
            rows = slice(j * rows_per_chunk, (j + 1) * rows_per_chunk)
            z = zc_ref[i, rows, :] if pad is None else _chunk_tile(zc_ref, i, pad)
            u = jax.nn.gelu(z[:, :MIX_W])
            vn = _rms(jax.nn.gelu(z[:, MIX_W:]), gcv_ref[...])
            if want_vrows:
                vrow_ref[i, rows, :] = vn[0:rows_per_chunk]
            vb = vn.astype(BF16)
            s = _dot(ws[N_HEADS - 1], vb) + bs_ref[:, N_HEADS - 1:N_HEADS]
            for h in range(N_HEADS - 2, -1, -1):
                s = jnp.where(lane_head == h, _dot(ws[h], vb) + bs_ref[:, h:h + 1], s)
            out_ref[i, rows, :] = (u * s)[0:rows_per_chunk]


def _gate(zc, w_s, bs_t, gcv, nseq, rows, want_vrows):
    B, T, _ = zc.shape
    blk = (nseq, rows, MIX_W)
    n_out = 2 if want_vrows else 1
    return pl.pallas_call(
        functools.partial(_gate_body, want_vrows=want_vrows),
        grid=(B // nseq, T // rows),
        in_specs=[pl.BlockSpec((nseq, rows, 2 * MIX_W), lambda b, c: (b, c, 0)),
                  pl.BlockSpec((N_HEADS, CHUNK, CHUNK), lambda b, c: (0, 0, 0)),
                  pl.BlockSpec((CHUNK, N_HEADS), lambda b, c: (0, 0)),
                  pl.BlockSpec((1, MIX_W), lambda b, c: (0, 0))],
        out_specs=[pl.BlockSpec(blk, lambda b, c: (b, c, 0))] * n_out,
        out_shape=[jax.ShapeDtypeStruct((B, T, MIX_W), F32)] * n_out,
        scratch_shapes=[pltpu.VMEM((CHUNK, 2 * MIX_W), F32)] if rows < CHUNK else [],
        compiler_params=_params("parallel", "parallel"),
        name="gate",
    )(zc, w_s, bs_t, gcv)


def _hgrn_levels(n_valid):
    return [l for l in range(N_LEVELS) if (CHUNK >> (l + 1)) < max(n_valid, 2)]


def _hgrn_tables(n_valid):
    p = np.arange(CHUNK)[:, None]
    u = np.arange(CHUNK)[None, :]
    mats = []
    for l in _hgrn_levels(n_valid):
        m = CHUNK >> (l + 1)
        start = (p // m) * m
        odd = ((p // m) % 2) == 1
        mats.append(np.where(odd, (u >= start) & (u <= p), (u > p) & (u <= start + m - 1)))
    mats.append(u <= p)
    mats.append((u > p) & (u <= n_valid - 1))
    mall = np.concatenate(mats, axis=0).astype(np.float32)
    t = np.arange(CHUNK)[:, None]
    s = np.arange(CHUNK)[None, :]
    x = t ^ s
    top = np.floor(np.log2(np.maximum(x, 1))).astype(np.int32)
    lvl = np.where(s < t, N_LEVELS - 1 - top, np.where(s == t, N_LEVELS, N_LEVELS + 1)).astype(np.int32)
    return jnp.asarray(mall, BF16), jnp.asarray(np.tile(lvl, (N_HEADS, 1)))


def _hgrn_body(zq_ref, zf_ref, zi_ref, zg_ref, lb_ref, gn_ref, mall_ref, lvl_ref, s0_ref,
               out_ref, s_out, s_s, *pads, n_valid):
    c = pl.program_id(1)
    nseq = zq_ref.shape[0]
    pq, pf, pi, pg = pads if pads else (None,) * 4
    _zero_pads(pads)

    @pl.when(c == 0)
    def _():
        s_s[...] = jnp.zeros_like(s_s)
        for i in range(nseq):
            for h in range(N_HEADS):
                s_s[i, _head(h), _head(h)] = s0_ref[i, h]

    lb = lb_ref[...]
    lb_floor = jnp.maximum(lb, LB_FLOOR)
    lvl = lvl_ref[...]
    valid_rows = lax.broadcasted_iota(jnp.int32, (CHUNK, 1), 0) < n_valid
    lane_head = lax.broadcasted_iota(jnp.int32, (CHUNK, MIX_W), 1) // HEAD_DIM
    same_head = (lax.broadcasted_iota(jnp.int32, (MIX_W, MIX_W), 0) // HEAD_DIM
                 == lax.broadcasted_iota(jnp.int32, (MIX_W, MIX_W), 1) // HEAD_DIM)
    head_ones = same_head.astype(BF16)
    levels = _hgrn_levels(n_valid)

    def by_head(x):
        return jnp.concatenate([jnp.where(lane_head == h, x, 0.0) for h in range(N_HEADS)], axis=0).astype(BF16)

    for i in range(nseq):
        q = _chunk_tile(zq_ref, i, pq)
        fx = _chunk_tile(zf_ref, i, pf)
        sig = jax.nn.sigmoid(fx)
        logf = jnp.log(lb_floor + (1.0 - lb) * sig)
        kd = (1.0 - lb) * (1.0 - sig)
        gsum = _dot_01(mall_ref[...], logf, pieces=2)
        amat = None
        for j, l in enumerate(levels):
            e = jnp.exp(gsum[j * CHUNK:(j + 1) * CHUNK])
            amat = jnp.where(lvl == l, _head_scores(q * e, kd * e), 0.0 if j == 0 else amat)
        amat = jnp.where(lvl == N_LEVELS, _head_scores(q, kd), amat)
        vb = _chunk_tile(zi_ref, i, pi).astype(BF16)
        bcum = gsum[len(levels) * CHUNK:(len(levels) + 1) * CHUNK]
        k_out = jnp.where(valid_rows, kd * jnp.exp(gsum[(len(levels) + 1) * CHUNK:]), 0.0).astype(BF16)
        e_last = jnp.exp(bcum[n_valid - 1:n_valid, :])
        sbd = s_s[i]
        o4 = _dot(amat.astype(BF16), vb) + _dot_nt(by_head(q * jnp.exp(bcum)), sbd.astype(BF16))
        o = o4[(N_HEADS - 1) * CHUNK:]
        for h in range(N_HEADS - 2, -1, -1):
            o = jnp.where(lane_head == h, o4[h * CHUNK:(h + 1) * CHUNK], o)
        sq = o * o
        sq_hi = sq.astype(BF16)
        sq_lo = (sq - sq_hi.astype(F32)).astype(BF16)
        ms = (_dot(sq_hi, head_ones) + _dot(sq_lo, head_ones)) * (1.0 / HEAD_DIM)
        gate = _chunk_tile(zg_ref, i, pg)
        out = o * lax.rsqrt(ms + EPS) * gn_ref[...] * (gate * jax.nn.sigmoid(gate))
        out_ref[i] = out[0:out_ref.shape[1]]
        s_s[i] = e_last * sbd + jnp.where(same_head, _dot_tn(vb, k_out), 0.0)

    @pl.when(c == pl.num_programs(1) - 1)
    def _():
        for i in range(nseq):
            for h in range(N_HEADS):
                s_out[i, h] = s_s[i, _head(h), _head(h)]


def _hgrn(zd, lb, gn, layer, s0_t, n_valid, nseq):
    B, T, _ = zd.shape
    mall, lvl = _hgrn_tables(n_valid)
    rows = min(T, CHUNK)
    blk = (nseq, rows, MIX_W)
    zspec = lambda j: pl.BlockSpec(blk, lambda b, c: (b, c, j))
    pads = [pltpu.VMEM((CHUNK, MIX_W), F32)] * 4 if rows < CHUNK else []
    st4 = pl.BlockSpec((nseq, N_HEADS, HEAD_DIM, HEAD_DIM), lambda b, c: (b, 0, 0, 0))
    vec = pl.BlockSpec((1, MIX_W), lambda b, c: (0, 0))
    return pl.pallas_call(
        functools.partial(_hgrn_body, n_valid=n_valid),
        grid=(B // nseq, T // rows),
        in_specs=[zspec(0), zspec(1), zspec(2), zspec(3), vec, vec,
                  pl.BlockSpec(mall.shape, lambda b, c: (0, 0)),
                  pl.BlockSpec(lvl.shape, lambda b, c: (0, 0)),
                  pl.BlockSpec((None,) + st4.block_shape, lambda b, c: (layer, b, 0, 0, 0))],
        out_specs=[pl.BlockSpec(blk, lambda b, c: (b, c, 0)), st4],
        out_shape=[jax.ShapeDtypeStruct((B, T, MIX_W), F32),
                   jax.ShapeDtypeStruct((B, N_HEADS, HEAD_DIM, HEAD_DIM), F32)],
        scratch_shapes=[pltpu.VMEM((nseq, MIX_W, MIX_W), F32)] + pads,
        compiler_params=_params("parallel", "arbitrary"),
        name="hgrn",
    )(zd, zd, zd, zd, lb, gn, mall, lvl, s0_t)


def _post_body(x_ref, oa_ref, ob_ref, oc_ref, od_ref, wout_hbm, gm_ref, wup_hbm, wdn_hbm, gf_ref, y_ref,
               wout_ref, wup_ref, wdn_ref, h_s, acc_s, *, layer, tf, final):
    @pl.when(pl.program_id(0) == 0)
    def _():
        pltpu.sync_copy(wout_hbm.at[layer], wout_ref)
        pltpu.sync_copy(wup_hbm.at[layer], wup_ref)
        pltpu.sync_copy(wdn_hbm.at[layer], wdn_ref)

    mix = jnp.concatenate([o_ref[...].astype(BF16) for o_ref in (oa_ref, ob_ref, oc_ref, od_ref)], axis=1)
    x1 = x_ref[...] + _dot(mix, wout_ref[...])
    acc_s[...] = x1
    h_s[...] = _rms(x1, gm_ref[...]).astype(BF16)
    for j in range(D_FF // tf):
        up = jnp.maximum(_dot(h_s[...], wup_ref[:, j * tf:(j + 1) * tf]), 0.0)
        acc_s[...] += _dot((up * up).astype(BF16), wdn_ref[j * tf:(j + 1) * tf, :])
    y_ref[...] = _rms(acc_s[...], gf_ref[...]) if final else acc_s[...]


def _post(layer, x2d, oa, ob, oc, od, wout, gm, wup, wdn, gf, tm, tf, final):
    n = x2d.shape[0]
    row = lambda wd: pl.BlockSpec((tm, wd), lambda i: (i, 0))
    vec = pl.BlockSpec((1, D_MODEL), lambda i: (0, 0))
    hbm = pl.BlockSpec(memory_space=pl.ANY)
    return pl.pallas_call(
        functools.partial(_post_body, layer=layer, tf=tf, final=final),
        grid=(n // tm,),
        in_specs=[row(D_MODEL)] + [row(MIX_W)] * 4 + [hbm, vec, hbm, hbm, vec],
        out_specs=row(D_MODEL),
        out_shape=jax.ShapeDtypeStruct((n, D_MODEL), F32),
        scratch_shapes=[pltpu.VMEM(wout.shape[1:], BF16), pltpu.VMEM(wup.shape[1:], BF16), pltpu.VMEM(wdn.shape[1:], BF16),
                        pltpu.VMEM((tm, D_MODEL), BF16), pltpu.VMEM((tm, D_MODEL), F32)],
        compiler_params=_params("arbitrary"),
        name="post",
    )(x2d, oa, ob, oc, od, wout, gm, wup, wdn, gf)


def _rel_bucket(dist):
    max_exact = N_BUCKETS // 2
    d = jnp.maximum(dist, 1).astype(F32)
    large = max_exact + (jnp.log(d / max_exact) / math.log(MAX_WINDOW / max_exact)
                         * (N_BUCKETS - max_exact)).astype(jnp.int32)
    large = jnp.clip(large, max_exact, N_BUCKETS - 1)
    return jnp.where(dist < max_exact, dist, large)


def _pattern_bias(rel_bias, w, d):
    offs = jnp.arange(w // d + 1, dtype=jnp.int32) * d
    return rel_bias[_rel_bucket(offs)].T.astype(F32)


def _prompt_table(bias):
    cols = 2 * CHUNK
    u = jnp.concatenate([bias[:, ::-1], jnp.full((N_HEADS, cols - CHUNK), NEG, F32)], axis=1)
    return jnp.tile(u, (1, CHUNK))[:, :CHUNK * cols].reshape(N_HEADS, CHUNK, cols)


def _sample_tables(bias, d, n_new, past):
    comb = jnp.concatenate([bias[:, :, None], jnp.full((N_HEADS, CHUNK + 1, d - 1), NEG, F32)], axis=2)
    comb = comb.reshape(N_HEADS, (CHUNK + 1) * d)
    length = past + SAMPLE_ROWS + 1
    comb = comb[:, :length]
    comb = jnp.pad(comb, ((0, 0), (0, length - comb.shape[1])), constant_values=NEG)
    rev = comb[:, ::-1]
    tpast, tnew = [], []
    for row in range(SAMPLE_ROWS):
        t = min(row, n_new - 1)
        start = length - 1 - past - t
        tpast.append(rev[:, start:start + past])
        start = length - 1 - t
        tnew.append(jnp.pad(rev[:, start:start + t + 1], ((0, 0), (0, CHUNK - t - 1)), constant_values=NEG))
    return jnp.stack(tpast, axis=1), jnp.stack(tnew, axis=1)


def _pad_rows(a, rows):
    return jnp.pad(a, ((0, 0), (0, rows - a.shape[1]), (0, 0)))


def kernel(x_prompt, x_sample, cache_k_win, cache_v_win, state_mlstm_C, state_mlstm_n, state_mlstm_m, state_hgrn_S, rel_bias, w_in, w_out, g_attn, g_mlp, w_up, w_down, b_i, b_f, g_mlstm, g_cv, w_s, b_s, hgrn_lb, g_hgrn, g_final):
    depth = w_in.shape[0]
    B, T, _ = x_prompt.shape
    Bs, Ts, _ = x_sample.shape
    past = cache_k_win.shape[2]
    assert past == MAX_WINDOW and T % (MAX_WINDOW) == 0 and Ts <= SAMPLE_ROWS
    H, dh = N_HEADS, HEAD_DIM
    keep_p = min(MAX_WINDOW, T)

    sm = jax.nn.softmax(hgrn_lb.astype(F32), axis=0)
    lb_all = jnp.cumsum(sm, axis=0) - sm[0:1]
    biases = [_pattern_bias(rel_bias, w, d) for w, d in DILATIONS]
    tabs_p = jnp.stack([_prompt_table(bb) for bb in biases])
    tabs_s = [_sample_tables(bb, d, Ts, past) for bb, (_, d) in zip(biases, DILATIONS)]
    tpast = jnp.stack([a for a, _ in tabs_s])
    tnew = jnp.stack([b for _, b in tabs_s])
    kt_all = jnp.transpose(cache_k_win, (0, 1, 3, 4, 2))
    vt_all = jnp.transpose(cache_v_win, (0, 1, 3, 4, 2))

    a_end = Z_WIDTHS[0] + Z_WIDTHS[1]
    g_end = a_end + N_GATE_COLS
    zeros_c = jnp.zeros((1, B, H, dh, dh), F32)
    zeros_n = jnp.zeros((1, B, 1, MIX_W), F32)
    zeros_m = jnp.zeros((1, B, 1, H), F32)
    n0_all = state_mlstm_n.reshape(depth, Bs, 1, MIX_W)
    m0_all = state_mlstm_m.reshape(depth, Bs, 1, H)
    s0_all = jnp.swapaxes(state_hgrn_S, -1, -2)
    vec = lambda a: a.reshape(1, -1).astype(F32)
    pseq = math.gcd(B, PROMPT_SEQS_PER_STEP)
    sseq = math.gcd(Bs, SAMPLE_SEQS_PER_STEP)

    wout_b, wup_b, wdn_b = w_out.astype(BF16), w_up.astype(BF16), w_down.astype(BF16)
    xp = x_prompt.reshape(B * T, D_MODEL)
    xs = x_sample.reshape(Bs * Ts, D_MODEL)
    outs = [[] for _ in range(13)]
    new_rows = []
    for l in range(depth):
        wl = w_in[l]
        w_gate = jnp.pad(wl[:, a_end:g_end], ((0, 0), (0, GATE_W - N_GATE_COLS)))
        w_z = jnp.concatenate([wl[:, :a_end], w_gate, wl[:, g_end:]], axis=1).astype(BF16)
        wg_lo = (w_gate - w_gate.astype(BF16).astype(F32)).astype(BF16)
        gate_bias = jnp.zeros((2, GATE_W), F32).at[0, :H].set(b_i[l]).at[1, H:2 * H].set(b_f[l])
        bs_t = b_s[l].T.astype(F32)
        final = l == depth - 1
        gf = vec(g_final)

        za, zb, zg, zc, zd = _inproj(xp, vec(g_attn[l]), w_z, wg_lo, 512)
        za3 = za.reshape(B, T, -1)
        oa = _attn_prompt(za3, tabs_p)
        ob, c1, n1, m1 = _mlstm(zb.reshape(B, T, -1), zg.reshape(B, T, -1), gate_bias, vec(g_mlstm[l]),
                                0, zeros_c, zeros_n, zeros_m, CHUNK, pseq)
        (oc,) = _gate(zc.reshape(B, T, -1), w_s[l], bs_t, vec(g_cv[l]), 1, GATE_ROWS, False)
        od, s1 = _hgrn(zd.reshape(B, T, -1), vec(lb_all[l]), vec(g_hgrn[l]), 0, zeros_c, CHUNK, pseq)
        xp = _post(l, xp, oa.reshape(B * T, -1), ob.reshape(B * T, -1), oc.reshape(B * T, -1), od.reshape(B * T, -1),
                   wout_b, vec(g_mlp[l]), wup_b, wdn_b, gf, 512, 1024, final)
        outs[0].append(za3[:, T - keep_p:, MIX_W:2 * MIX_W].reshape(B, keep_p, H, dh))
        outs[1].append(za3[:, T - keep_p:, 2 * MIX_W:].reshape(B, keep_p, H, dh))
        outs[4].append(c1)
        outs[5].append(n1.reshape(B, H, dh))
        outs[6].append(m1.reshape(B, H))
        outs[10].append(s1)

        za, zb, zg, zc, zd = _inproj(xs, vec(g_attn[l]), w_z, wg_lo, Bs * Ts)
        za3 = za.reshape(Bs, Ts, -1)
        new_t = lambda a: jnp.pad(jnp.transpose(a.reshape(Bs, Ts, H, dh), (0, 2, 3, 1)),
                                  ((0, 0), (0, 0), (0, 0), (0, LANES - Ts)))
        new_rows.append((new_t(za3[:, :, MIX_W:2 * MIX_W]), new_t(za3[:, :, 2 * MIX_W:])))
        oa, *windows = _attn_sample(l, _pad_rows(za3[:, :, :MIX_W], SAMPLE_ROWS), kt_all, vt_all,
                                    new_rows, tpast, tnew, Ts)
        seqs = lambda z: z.reshape(Bs, Ts, -1)
        ob, c2, n2, m2 = _mlstm(seqs(zb), seqs(zg), gate_bias, vec(g_mlstm[l]),
                                l, state_mlstm_C, n0_all, m0_all, Ts, sseq)
        oc, vrows = _gate(seqs(zc), w_s[l], bs_t, vec(g_cv[l]), sseq, Ts, True)
        od, s2 = _hgrn(seqs(zd), vec(lb_all[l]), vec(g_hgrn[l]), l, s0_all, Ts, sseq)
        flat = lambda o: o.reshape(Bs * Ts, MIX_W)
        xs = _post(l, xs, flat(oa[:, :Ts]), flat(ob), flat(oc), flat(od),
                   wout_b, vec(g_mlp[l]), wup_b, wdn_b, gf, Bs * Ts, 1024, final)
        outs[7].append(c2)
        outs[8].append(n2.reshape(Bs, H, dh))
        outs[9].append(m2.reshape(Bs, H))
        outs[11].append(s2)
        outs[12].append(vrows.reshape(Bs, Ts, H, dh))

    stacked = [jnp.stack(o) if o else None for o in outs]
    stacked[2], stacked[3] = (jnp.transpose(w, (0, 1, 4, 2, 3)) for w in windows)
    stacked[10], stacked[11] = (jnp.swapaxes(s, -1, -2) for s in stacked[10:12])
    return (xp.reshape(B, T, D_MODEL), xs.reshape(Bs, Ts, D_MODEL)) + tuple(stacked)
```

```python
import functools
import math

import numpy as np
import jax
import jax.numpy as jnp
from jax import lax
from jax.experimental import pallas as pl
from jax.experimental.pallas import tpu as pltpu

F32 = jnp.float32
BF16 = jnp.bfloat16

D_MODEL = 1024
N_HEADS = 4
HEAD_DIM = 64
MIX_W = N_HEADS * HEAD_DIM
DILATIONS = ((128, 1), (512, 4), (2048, 16))
MAX_WINDOW = 2048
N_BUCKETS = 32
D_FF = 4 * D_MODEL
EPS = 1e-6
NEG = -1e30
LB_FLOOR = 1e-30
CHUNK = 128
N_GATE_COLS = 2 * N_HEADS
GATE_W = 128
Z_WIDTHS = (3 * MIX_W, 4 * MIX_W, GATE_W, 2 * MIX_W, 4 * MIX_W)
Z_TOTAL = sum(Z_WIDTHS)
ATTN_SCALE = HEAD_DIM ** -0.5
N_LEVELS = 7
VMEM_LIMIT = 48 * 1024 * 1024
PROMPT_SEQS_PER_STEP = 2
SAMPLE_SEQS_PER_STEP = 4
GATE_ROWS = 4 * CHUNK


def _dot(a, b):
    return jnp.dot(a, b, preferred_element_type=F32)


def _dot_nt(a, b):
    return lax.dot_general(a, b, (((1,), (1,)), ((), ())), preferred_element_type=F32)


def _dot_tn(a, b):
    return lax.dot_general(a, b, (((0,), (0,)), ((), ())), preferred_element_type=F32)


def _rms(x, g):
    return x * lax.rsqrt(jnp.mean(x * x, axis=-1, keepdims=True) + EPS) * g


def _log_sigmoid(x):
    return jnp.minimum(x, 0.0) - jnp.log1p(jnp.exp(-jnp.abs(x)))


def _dot_01(m01, f, pieces=3):
    out = None
    for _ in range(pieces):
        piece = f.astype(BF16)
        f = f - piece.astype(F32)
        out = _dot(m01, piece) if out is None else out + _dot(m01, piece)
    return out


def _head(h):
    return slice(h * HEAD_DIM, (h + 1) * HEAD_DIM)


def _head_scores(x, y):
    rows, width = x.shape
    tile = 2 * HEAD_DIM
    low = lax.broadcasted_iota(jnp.int32, (rows, tile), 1) < HEAD_DIM
    out = []
    for half in range(width // tile):
        lanes = slice(half * tile, (half + 1) * tile)
        xh = x[:, lanes].astype(BF16)
        zero = jnp.zeros_like(xh)
        x2 = jnp.concatenate([jnp.where(low, xh, zero), jnp.where(low, zero, xh)], axis=0)
        out.append(_dot_nt(x2, y[:, lanes].astype(BF16)))
    return jnp.concatenate(out, axis=0)


def _chunk_tile(ref, i, pad_ref):
    if pad_ref is None:
        return ref[i]
    pad_ref[0:ref.shape[1], :] = ref[i]
    return pad_ref[...]


def _zero_pads(pads):
    for p in pads:
        p[...] = jnp.zeros_like(p)


def _params(*sem):
    return pltpu.CompilerParams(dimension_semantics=sem, vmem_limit_bytes=VMEM_LIMIT)


def _inproj_body(x_ref, g_ref, w_ref, *out_refs):
    hn = _rms(x_ref[...], g_ref[...])
    h = hn.astype(BF16)
    off = 0
    for o_ref in out_refs:
        n = o_ref.shape[-1]
        if n == GATE_W:
            h_lo = (hn - h.astype(F32)).astype(BF16)
            z2 = _dot(h, w_ref[:, off:off + 2 * n])
            o_ref[...] = z2[:, :n] + z2[:, n:] + _dot(h_lo, w_ref[:, off:off + n])
            off += 2 * n
        else:
            o_ref[...] = _dot(h, w_ref[:, off:off + n])
            off += n


def _inproj(x2d, g, w, tm):
    n = x2d.shape[0]
    return pl.pallas_call(
        _inproj_body,
        grid=(n // tm,),
        in_specs=[pl.BlockSpec((tm, D_MODEL), lambda i: (i, 0)),
                  pl.BlockSpec((1, D_MODEL), lambda i: (0, 0)),
                  pl.BlockSpec((D_MODEL, Z_TOTAL + GATE_W), lambda i: (0, 0))],
        out_specs=[pl.BlockSpec((tm, wd), lambda i: (i, 0)) for wd in Z_WIDTHS],
        out_shape=[jax.ShapeDtypeStruct((n, wd), F32) for wd in Z_WIDTHS],
        compiler_params=_params("parallel"),
        name="inproj",
    )(x2d, g, w)


ATTN_ROWS = MAX_WINDOW
LANES = 128
UNITS_PER_ITER = 4


def _attn_body(q0, q1, kp0, kp1, kc0, kc1, vp0, vp1, vc0, vc1, tab_ref, o_ref, os_ref, ls_ref):
    first = pl.program_id(1) == 0
    low_head = lax.broadcasted_iota(jnp.int32, (CHUNK, LANES), 1) < HEAD_DIM
    ones = jnp.ones((2 * CHUNK, LANES), BF16)
    q_refs, kp_refs, kc_refs, vp_refs, vc_refs = (q0, q1), (kp0, kp1), (kc0, kc1), (vp0, vp1), (vc0, vc1)

    def rows_of(start, d):
        if d == 1:
            return pl.ds(start if isinstance(start, int) else pl.multiple_of(start, CHUNK), CHUNK)
        return pl.ds(start, CHUNK, stride=d)

    def unit(p, d, r, c, prev_in_block):
        start = r + d * CHUNK * c
        cur = rows_of(start, d)
        prev = rows_of(start - d * CHUNK, d) if prev_in_block else rows_of(r + ATTN_ROWS - d * CHUNK, d)
        k_prev_ref, v_prev_ref = (kc_refs, vc_refs) if prev_in_block else (kp_refs, vp_refs)
        scores, vws = [], []
        for half in range(2):
            q = (q_refs[half][0, cur, :] * ATTN_SCALE).astype(BF16)
            kw = jnp.concatenate([k_prev_ref[half][0, prev, :], kc_refs[half][0, cur, :]], axis=0).astype(BF16)
            vws.append(jnp.concatenate([v_prev_ref[half][0, prev, :], vc_refs[half][0, cur, :]], axis=0).astype(BF16))
            zero = jnp.zeros_like(q)
            qm = jnp.concatenate([jnp.where(low_head, q, zero), jnp.where(low_head, zero, q)], axis=0)
            scores.append(_dot_nt(qm, kw))
        s = jnp.concatenate(scores, axis=0) + tab_ref[p]
        if not prev_in_block:
            s = jnp.concatenate([jnp.where(first, NEG, s[:, :CHUNK]), s[:, CHUNK:]], axis=1)
        m = jnp.max(jnp.maximum(s[:, :CHUNK], s[:, CHUNK:]), axis=1, keepdims=True)
        e = jnp.exp(s - m).astype(BF16)
        for half in range(2):
            rows = slice(2 * half * CHUNK, (2 * half + 2) * CHUNK)
            od = _dot(e[rows], jnp.concatenate([vws[half], ones], axis=1))
            den = od[:, LANES:]
            o = od[:, :LANES] / den
            lse = m[rows] + jnp.log(den)
            os_ref[p, half, cur, :] = jnp.where(low_head, o[:CHUNK], o[CHUNK:])
            ls_ref[p, half, cur, :] = jnp.where(low_head, lse[:CHUNK], lse[CHUNK:])

    for p, (_, d) in enumerate(DILATIONS):
        nblk = ATTN_ROWS // d // CHUNK
        if nblk >= UNITS_PER_ITER:
            def per_subsequence(r, carry, p=p, d=d, nblk=nblk):
                for c in range(UNITS_PER_ITER):
                    unit(p, d, r, c, c > 0)

                def group(g, carry2):
                    for u in range(UNITS_PER_ITER):
                        unit(p, d, r, g * UNITS_PER_ITER + u, True)
                    return carry2
                return lax.fori_loop(1, nblk // UNITS_PER_ITER, group, carry)
            if d == 1:
                per_subsequence(0, 0)
            else:
                lax.fori_loop(0, d, per_subsequence, 0)
        else:
            def group(g, carry, p=p, d=d):
                for u in range(UNITS_PER_ITER):
                    unit(p, d, g * UNITS_PER_ITER + u, 0, False)
                return carry
            lax.fori_loop(0, d // UNITS_PER_ITER, group, 0)

    def merge(i, carry):
        rows = pl.ds(pl.multiple_of(i * CHUNK, CHUNK), CHUNK)
        for half in range(2):
            ls = [ls_ref[p, half, rows, :] for p in range(len(DILATIONS))]
            mx = jnp.maximum(jnp.maximum(ls[0], ls[1]), ls[2])
            es = [jnp.exp(l - mx) for l in ls]
            num = es[0] * os_ref[0, half, rows, :] + es[1] * os_ref[1, half, rows, :] + es[2] * os_ref[2, half, rows, :]
            o_ref[0, rows, half * LANES:(half + 1) * LANES] = num / (es[0] + es[1] + es[2])
        return carry
    lax.fori_loop(0, ATTN_ROWS // CHUNK, merge, 0)


def _attn_prompt(za, tabs):
    B, T, _ = za.shape
    blk = (1, ATTN_ROWS, LANES)
    cur = lambda j: pl.BlockSpec(blk, lambda b, i: (b, i, j))
    prv = lambda j: pl.BlockSpec(blk, lambda b, i: (b, jnp.maximum(i - 1, 0), j))
    npat = len(DILATIONS)
    return pl.pallas_call(
        _attn_body,
        grid=(B, T // ATTN_ROWS),
        in_specs=[cur(0), cur(1), prv(2), prv(3), cur(2), cur(3), prv(4), prv(5), cur(4), cur(5),
                  pl.BlockSpec((npat, N_HEADS * CHUNK, 2 * CHUNK), lambda b, i: (0, 0, 0))],
        out_specs=pl.BlockSpec((1, ATTN_ROWS, MIX_W), lambda b, i: (b, i, 0)),
        out_shape=jax.ShapeDtypeStruct((B, T, MIX_W), F32),
        scratch_shapes=[pltpu.VMEM((npat, 2, ATTN_ROWS, LANES), F32)] * 2,
        compiler_params=_params("parallel", "arbitrary"),
        name="attn_prompt",
    )(*([za] * 10), tabs.reshape(npat, N_HEADS * CHUNK, 2 * CHUNK))


SAMPLE_ROWS = 8


def _attn_sample_body(q_ref, tpast_ref, tnew_ref, *rest, n_new, n_earlier, write_windows):
    kt_ref, vt_ref, knt_ref, vnt_ref = rest[:4]
    earlier = [rest[4 + 4 * e:8 + 4 * e] for e in range(n_earlier)]
    outs = rest[4 + 4 * n_earlier:]
    o_ref = outs[0]
    npat = len(DILATIONS)
    q8 = q_ref[0] * ATTN_SCALE
    for h in range(N_HEADS):
        sl = _head(h)
        qh = q8[:, sl].astype(BF16)
        kt, vt = kt_ref[0, 0, h], vt_ref[0, 0, h]
        knt, vnt = knt_ref[0, h], vnt_ref[0, h]
        sp = _dot(qh, kt.astype(BF16))
        sn = _dot(qh, knt.astype(BF16))
        pps, pns, dens, lses = [], [], [], []
        for p in range(npat):
            lp = sp + tpast_ref[p, h]
            ln = sn + tnew_ref[p, h]
            m = jnp.maximum(jnp.max(lp, axis=1, keepdims=True), jnp.max(ln, axis=1, keepdims=True))
            pps.append(jnp.exp(lp - m))
            pns.append(jnp.exp(ln - m))
            dens.append(jnp.sum(pps[-1], axis=1, keepdims=True) + jnp.sum(pns[-1], axis=1, keepdims=True))
            lses.append(m + jnp.log(dens[-1]))
        pv = (_dot_nt(jnp.concatenate(pps, axis=0).astype(BF16), vt.astype(BF16))
              + _dot_nt(jnp.concatenate(pns, axis=0).astype(BF16), vnt.astype(BF16)))
        mx = jnp.maximum(jnp.maximum(lses[0], lses[1]), lses[2])
        es = [jnp.exp(l - mx) for l in lses]
        num = sum(es[p] * pv[p * SAMPLE_ROWS:(p + 1) * SAMPLE_ROWS] / dens[p] for p in range(npat))
        o_ref[0, :, sl] = num / (es[0] + es[1] + es[2])
        if write_windows:
            slabs = earlier + [(kt_ref, vt_ref, knt_ref, vnt_ref)]
            for layer, (kt_l, vt_l, knt_l, vnt_l) in enumerate(slabs):
                outs[1][layer, 0, h] = jnp.concatenate([kt_l[0, 0, h][:, n_new:], knt_l[0, h][:, :n_new]], axis=1)
                outs[2][layer, 0, h] = jnp.concatenate([vt_l[0, 0, h][:, n_new:], vnt_l[0, h][:, :n_new]], axis=1)


def _attn_sample(layer, q8, kt_all, vt_all, new_rows, tpast, tnew, n_new):
    depth, B, _, _, P = kt_all.shape
    npat = len(DILATIONS)
    write_windows = layer == depth - 1
    n_earlier = layer if write_windows else 0
    cache = lambda l: pl.BlockSpec((1, 1, N_HEADS, HEAD_DIM, P), lambda b: (l, b, 0, 0, 0))
    new = pl.BlockSpec((1, N_HEADS, HEAD_DIM, LANES), lambda b: (b, 0, 0, 0))
    rows = pl.BlockSpec((1, SAMPLE_ROWS, MIX_W), lambda b: (b, 0, 0))
    in_specs = [rows,
                pl.BlockSpec((npat, N_HEADS, SAMPLE_ROWS, P), lambda b: (0, 0, 0, 0)),
                pl.BlockSpec((npat, N_HEADS, SAMPLE_ROWS, LANES), lambda b: (0, 0, 0, 0))]
    args = [q8, tpast, tnew]
    for l in [layer] + list(range(n_earlier)):
        in_specs += [cache(l), cache(l), new, new]
        args += [kt_all, vt_all, *new_rows[l]]
    out_specs = [rows]
    out_shape = [jax.ShapeDtypeStruct((B, SAMPLE_ROWS, MIX_W), F32)]
    if write_windows:
        out_specs += [pl.BlockSpec((depth, 1, N_HEADS, HEAD_DIM, P), lambda b: (0, b, 0, 0, 0))] * 2
        out_shape += [jax.ShapeDtypeStruct(kt_all.shape, F32)] * 2
    return pl.pallas_call(
        functools.partial(_attn_sample_body, n_new=n_new, n_earlier=n_earlier, write_windows=write_windows),
        grid=(B,),
        in_specs=in_specs,
        out_specs=out_specs,
        out_shape=out_shape,
        compiler_params=_params("parallel"),
        name="attn_sample",
    )(*args)


def _mlstm_body(zq_ref, zk_ref, zv_ref, zo_ref, zg_ref, bias_ref, gn_ref, c0_ref, n0_ref, m0_ref,
                out_ref, c_out, n_out, m_out, c_s, n_s, m_s, *pads, n_valid):
    c = pl.program_id(1)
    nseq = zq_ref.shape[0]
    pq, pk, pv, po, pg = pads if pads else (None,) * 5
    _zero_pads(pads)

    @pl.when(c == 0)
    def _():
        c_s[...] = jnp.zeros_like(c_s)
        for i in range(nseq):
            for h in range(N_HEADS):
                c_s[i, _head(h), _head(h)] = c0_ref[i, h]
        n_s[...] = n0_ref[...]
        m_s[...] = m0_ref[...]

    rows4 = N_HEADS * CHUNK
    t4 = lax.broadcasted_iota(jnp.int32, (rows4, CHUNK), 0) & (CHUNK - 1)
    causal4 = t4 >= lax.broadcasted_iota(jnp.int32, (rows4, CHUNK), 1)
    valid4 = t4[:, 0:1] < n_valid
    tril = (lax.broadcasted_iota(jnp.int32, (CHUNK, CHUNK), 0)
            >= lax.broadcasted_iota(jnp.int32, (CHUNK, CHUNK), 1)).astype(BF16)
    lane_head = lax.broadcasted_iota(jnp.int32, (CHUNK, MIX_W), 1) // HEAD_DIM
    same_head = (lax.broadcasted_iota(jnp.int32, (MIX_W, MIX_W), 0) // HEAD_DIM
                 == lax.broadcasted_iota(jnp.int32, (MIX_W, MIX_W), 1) // HEAD_DIM)
    head_ones = same_head.astype(BF16)
    ones_l = jnp.ones((CHUNK, MIX_W), BF16)
    last = n_valid - 1

    def stack(f):
        return jnp.concatenate([f(h) for h in range(N_HEADS)], axis=0)

    def on_head_lanes(f, lanes):
        out = f(N_HEADS - 1)
        for h in range(N_HEADS - 2, -1, -1):
            out = jnp.where(lanes == h, f(h), out)
        return out

    for i in range(nseq):
        gates = _chunk_tile(zg_ref, i, pg)
        ipre = gates + bias_ref[0:1, :]
        logf = _log_sigmoid(gates + bias_ref[1:2, :])
        bcum = _dot_01(tril, logf)
        bcum_t = bcum.T
        ipre_t = ipre.T
        q = _chunk_tile(zq_ref, i, pq)
        k = _chunk_tile(zk_ref, i, pk) * ATTN_SCALE
        v = _chunk_tile(zv_ref, i, pv)
        kb, vb = k.astype(BF16), v.astype(BF16)
        cbd = c_s[i]
        nrow = n_s[i]
        mprev = m_s[i]
        bcol = stack(lambda h: bcum[:, N_HEADS + h:N_HEADS + h + 1])
        icol = stack(lambda h: ipre[:, h:h + 1])
        brow = stack(lambda h: jnp.broadcast_to(bcum_t[N_HEADS + h:N_HEADS + h + 1, :], (CHUNK, CHUNK)))
        irow = stack(lambda h: jnp.broadcast_to(ipre_t[h:h + 1, :], (CHUNK, CHUNK)))
        mprev4 = stack(lambda h: jnp.broadcast_to(mprev[:, h:h + 1], (CHUNK, 1)))
        dmat = jnp.where(causal4, bcol - brow + irow, NEG)
        g = bcol + mprev4
        mt = jnp.maximum(g, jnp.max(dmat, axis=1, keepdims=True))
        dexp = jnp.exp(dmat - mt)
        gexp = jnp.exp(g - mt)
        sb = (_head_scores(q, k) * dexp).astype(BF16)
        q_c = _dot_nt(q.astype(BF16), cbd.astype(BF16))
        q_n = _dot((q * nrow).astype(BF16), head_ones)
        num = _dot(sb, vb) + gexp * jnp.concatenate([q_c] * N_HEADS, axis=0)
        nq = _dot(sb, ones_l) + gexp * jnp.concatenate([q_n] * N_HEADS, axis=0)
        hh4 = num / jnp.maximum(jnp.abs(nq), jnp.exp(-mt))
        hh = on_head_lanes(lambda h: hh4[h * CHUNK:(h + 1) * CHUNK], lane_head)
        sq = hh * hh
        sq_hi = sq.astype(BF16)
        sq_lo = (sq - sq_hi.astype(F32)).astype(BF16)
        ms = (_dot(sq_hi, head_ones) + _dot(sq_lo, head_ones)) * (1.0 / HEAD_DIM)
        out = jax.nn.sigmoid(_chunk_tile(zo_ref, i, po)) * (hh * lax.rsqrt(ms + EPS) * gn_ref[...])
        out_ref[i] = out[0:out_ref.shape[1]]

        mnew = [mt[h * CHUNK + last:h * CHUNK + last + 1, :] for h in range(N_HEADS)]
        blast = [bcol[h * CHUNK + last:h * CHUNK + last + 1, :] for h in range(N_HEADS)]
        wk = jnp.exp(stack(lambda h: jnp.broadcast_to(blast[h] - mnew[h], (CHUNK, 1))) - bcol + icol)
        if n_valid < CHUNK:
            wk = jnp.where(valid4, wk, 0.0)
        wk_w = on_head_lanes(lambda h: jnp.broadcast_to(wk[h * CHUNK:(h + 1) * CHUNK], (CHUNK, MIX_W)), lane_head)
        dc = on_head_lanes(lambda h: jnp.broadcast_to(jnp.exp(blast[h] + mprev[:, h:h + 1] - mnew[h]), (1, MIX_W)),
                           lane_head[0:1])
        c_s[i] = dc * cbd + jnp.where(same_head, _dot_tn((v * wk_w).astype(BF16), kb), 0.0)
        n_s[i] = dc * nrow + jnp.sum(wk_w * k, axis=0, keepdims=True)
        for h in range(N_HEADS):
            m_s[i, :, h:h + 1] = mnew[h]

    @pl.when(c == pl.num_programs(1) - 1)
    def _():
        for i in range(nseq):
            for h in range(N_HEADS):
                c_out[i, h] = c_s[i, _head(h), _head(h)]
        n_out[...] = n_s[...]
        m_out[...] = m_s[...]


def _mlstm(zb, zg, bias, gn, layer, c0, n0, m0, n_valid, nseq):
    B, T, _ = zb.shape
    rows = min(T, CHUNK)
    nc = T // rows
    blk = (nseq, rows, MIX_W)
    zspec = lambda j: pl.BlockSpec(blk, lambda b, c: (b, c, j))
    pads = [pltpu.VMEM((CHUNK, wd), F32) for wd in (MIX_W,) * 4 + (GATE_W,)] if rows < CHUNK else []
    st4 = pl.BlockSpec((nseq, N_HEADS, HEAD_DIM, HEAD_DIM), lambda b, c: (b, 0, 0, 0))
    st3 = pl.BlockSpec((nseq, 1, MIX_W), lambda b, c: (b, 0, 0))
    st2 = pl.BlockSpec((nseq, 1, N_HEADS), lambda b, c: (b, 0, 0))
    init = lambda spec: pl.BlockSpec((None,) + spec.block_shape,
                                     lambda b, c: (layer, b) + (0,) * (len(spec.block_shape) - 1))
    return pl.pallas_call(
        functools.partial(_mlstm_body, n_valid=n_valid),
        grid=(B // nseq, nc),
        in_specs=[zspec(0), zspec(1), zspec(2), zspec(3),
                  pl.BlockSpec((nseq, rows, GATE_W), lambda b, c: (b, c, 0)),
                  pl.BlockSpec((2, GATE_W), lambda b, c: (0, 0)),
                  pl.BlockSpec((1, MIX_W), lambda b, c: (0, 0)),
                  init(st4), init(st3), init(st2)],
        out_specs=[pl.BlockSpec(blk, lambda b, c: (b, c, 0)), st4, st3, st2],
        out_shape=[jax.ShapeDtypeStruct((B, T, MIX_W), F32),
                   jax.ShapeDtypeStruct((B, N_HEADS, HEAD_DIM, HEAD_DIM), F32),
                   jax.ShapeDtypeStruct((B, 1, MIX_W), F32),
                   jax.ShapeDtypeStruct((B, 1, N_HEADS), F32)],
        scratch_shapes=[pltpu.VMEM((nseq, MIX_W, MIX_W), F32),
                        pltpu.VMEM((nseq, 1, MIX_W), F32),
                        pltpu.VMEM((nseq, 1, N_HEADS), F32)] + pads,
        compiler_params=_params("parallel", "arbitrary"),
        name="mlstm",
    )(zb, zb, zb, zb, zg, bias, gn, c0, n0, m0)


def _gate_body(zc_ref, w_ref, bs_ref, gcv_ref, *rest, want_vrows):
    out_ref = rest[0]
    vrow_ref = rest[1] if want_vrows else None
    pad = rest[-1] if zc_ref.shape[1] < CHUNK else None
    if pad is not None:
        _zero_pads([pad])
    row = lax.broadcasted_iota(jnp.int32, (CHUNK, CHUNK), 0)
    col = lax.broadcasted_iota(jnp.int32, (CHUNK, CHUNK), 1)
    lane_head = lax.broadcasted_iota(jnp.int32, (CHUNK, MIX_W), 1) // HEAD_DIM
    ws = [jnp.where(row >= col, w_ref[h], 0.0).astype(BF16) for h in range(N_HEADS)]
    rows_per_chunk = min(zc_ref.shape[1], CHUNK)
    for i in range(zc_ref.shape[0]):
        for j in range(zc_ref.shape[1] // rows_per_chunk):
            rows = slice(j * rows_per_chunk, (j + 1) * rows_per_chunk)
            z = zc_ref[i, rows, :] if pad is None else _chunk_tile(zc_ref, i, pad)
            u = jax.nn.gelu(z[:, :MIX_W])
            vn = _rms(jax.nn.gelu(z[:, MIX_W:]), gcv_ref[...])
            if want_vrows:
                vrow_ref[i, rows, :] = vn[0:rows_per_chunk]
            vb = vn.astype(BF16)
            s = _dot(ws[N_HEADS - 1], vb) + bs_ref[:, N_HEADS - 1:N_HEADS]
            for h in range(N_HEADS - 2, -1, -1):
                s = jnp.where(lane_head == h, _dot(ws[h], vb) + bs_ref[:, h:h + 1], s)
            out_ref[i, rows, :] = (u * s)[0:rows_per_chunk]


def _gate(zc, w_s, bs_t, gcv, nseq, rows, want_vrows):
    B, T, _ = zc.shape
    blk = (nseq, rows, MIX_W)
    n_out = 2 if want_vrows else 1
    return pl.pallas_call(
        functools.partial(_gate_body, want_vrows=want_vrows),
        grid=(B // nseq, T // rows),
        in_specs=[pl.BlockSpec((nseq, rows, 2 * MIX_W), lambda b, c: (b, c, 0)),
                  pl.BlockSpec((N_HEADS, CHUNK, CHUNK), lambda b, c: (0, 0, 0)),
                  pl.BlockSpec((CHUNK, N_HEADS), lambda b, c: (0, 0)),
                  pl.BlockSpec((1, MIX_W), lambda b, c: (0, 0))],
        out_specs=[pl.BlockSpec(blk, lambda b, c: (b, c, 0))] * n_out,
        out_shape=[jax.ShapeDtypeStruct((B, T, MIX_W), F32)] * n_out,
        scratch_shapes=[pltpu.VMEM((CHUNK, 2 * MIX_W), F32)] if rows < CHUNK else [],
        compiler_params=_params("parallel", "parallel"),
        name="gate",
    )(zc, w_s, bs_t, gcv)


def _hgrn_levels(n_valid):
    return [l for l in range(N_LEVELS) if (CHUNK >> (l + 1)) < max(n_valid, 2)]


def _hgrn_tables(n_valid):
    p = np.arange(CHUNK)[:, None]
    u = np.arange(CHUNK)[None, :]
    mats = []
    for l in _hgrn_levels(n_valid):
        m = CHUNK >> (l + 1)
        start = (p // m) * m
        odd = ((p // m) % 2) == 1
        mats.append(np.where(odd, (u >= start) & (u <= p), (u > p) & (u <= start + m - 1)))
    mats.append(u <= p)
    mats.append((u > p) & (u <= n_valid - 1))
    mall = np.concatenate(mats, axis=0).astype(np.float32)
    t = np.arange(CHUNK)[:, None]
    s = np.arange(CHUNK)[None, :]
    x = t ^ s
    top = np.floor(np.log2(np.maximum(x, 1))).astype(np.int32)
    lvl = np.where(s < t, N_LEVELS - 1 - top, np.where(s == t, N_LEVELS, N_LEVELS + 1)).astype(np.int32)
    return jnp.asarray(mall, BF16), jnp.asarray(np.tile(lvl, (N_HEADS, 1)))


def _hgrn_body(zq_ref, zf_ref, zi_ref, zg_ref, lb_ref, gn_ref, mall_ref, lvl_ref, s0_ref,
               out_ref, s_out, s_s, *pads, n_valid):
    c = pl.program_id(1)
    nseq = zq_ref.shape[0]
    pq, pf, pi, pg = pads if pads else (None,) * 4
    _zero_pads(pads)

    @pl.when(c == 0)
    def _():
        s_s[...] = jnp.zeros_like(s_s)
        for i in range(nseq):
            for h in range(N_HEADS):
                s_s[i, _head(h), _head(h)] = s0_ref[i, h]

    lb = lb_ref[...]
    lb_floor = jnp.maximum(lb, LB_FLOOR)
    lvl = lvl_ref[...]
    valid_rows = lax.broadcasted_iota(jnp.int32, (CHUNK, 1), 0) < n_valid
    lane_head = lax.broadcasted_iota(jnp.int32, (CHUNK, MIX_W), 1) // HEAD_DIM
    same_head = (lax.broadcasted_iota(jnp.int32, (MIX_W, MIX_W), 0) // HEAD_DIM
                 == lax.broadcasted_iota(jnp.int32, (MIX_W, MIX_W), 1) // HEAD_DIM)
    head_ones = same_head.astype(BF16)
    levels = _hgrn_levels(n_valid)
    at_level = [lvl == l for l in levels]
    on_diagonal = lvl == N_LEVELS

    for i in range(nseq):
        q = _chunk_tile(zq_ref, i, pq)
        fx = _chunk_tile(zf_ref, i, pf)
        sig = jax.nn.sigmoid(fx)
        logf = jnp.log(lb_floor + (1.0 - lb) * sig)
        kd = (1.0 - lb) * (1.0 - sig)
        gsum = _dot_01(mall_ref[...], logf, pieces=2)
        amat = None
        for j, l in enumerate(levels):
            e = jnp.exp(gsum[j * CHUNK:(j + 1) * CHUNK])
            amat = jnp.where(at_level[j], _head_scores(q * e, kd * e), 0.0 if j == 0 else amat)
        amat = jnp.where(on_diagonal, _head_scores(q, kd), amat)
        vb = _chunk_tile(zi_ref, i, pi).astype(BF16)
        bcum = gsum[len(levels) * CHUNK:(len(levels) + 1) * CHUNK]
        k_out = jnp.where(valid_rows, kd * jnp.exp(gsum[(len(levels) + 1) * CHUNK:]), 0.0).astype(BF16)
        e_last = jnp.exp(bcum[n_valid - 1:n_valid, :])
        sbd = s_s[i]
        o4 = _dot(amat.astype(BF16), vb)
        o = o4[(N_HEADS - 1) * CHUNK:]
        for h in range(N_HEADS - 2, -1, -1):
            o = jnp.where(lane_head == h, o4[h * CHUNK:(h + 1) * CHUNK], o)
        o = o + _dot_nt((q * jnp.exp(bcum)).astype(BF16), sbd.astype(BF16))
        sq = o * o
        sq_hi = sq.astype(BF16)
        sq_lo = (sq - sq_hi.astype(F32)).astype(BF16)
        ms = (_dot(sq_hi, head_ones) + _dot(sq_lo, head_ones)) * (1.0 / HEAD_DIM)
        gate = _chunk_tile(zg_ref, i, pg)
        out = o * lax.rsqrt(ms + EPS) * gn_ref[...] * (gate * jax.nn.sigmoid(gate))
        out_ref[i] = out[0:out_ref.shape[1]]
        s_s[i] = e_last * sbd + jnp.where(same_head, _dot_tn(vb, k_out), 0.0)

    @pl.when(c == pl.num_programs(1) - 1)
    def _():
        for i in range(nseq):
            for h in range(N_HEADS):
                s_out[i, h] = s_s[i, _head(h), _head(h)]


def _hgrn(zd, lb, gn, layer, s0_t, n_valid, nseq):
    B, T, _ = zd.shape
    mall, lvl = _hgrn_tables(n_valid)
    rows = min(T, CHUNK)
    blk = (nseq, rows, MIX_W)
    zspec = lambda j: pl.BlockSpec(blk, lambda b, c: (b, c, j))
    pads = [pltpu.VMEM((CHUNK, MIX_W), F32)] * 4 if rows < CHUNK else []
    st4 = pl.BlockSpec((nseq, N_HEADS, HEAD_DIM, HEAD_DIM), lambda b, c: (b, 0, 0, 0))
    vec = pl.BlockSpec((1, MIX_W), lambda b, c: (0, 0))
    return pl.pallas_call(
        functools.partial(_hgrn_body, n_valid=n_valid),
        grid=(B // nseq, T // rows),
        in_specs=[zspec(0), zspec(1), zspec(2), zspec(3), vec, vec,
                  pl.BlockSpec(mall.shape, lambda b, c: (0, 0)),
                  pl.BlockSpec(lvl.shape, lambda b, c: (0, 0)),
                  pl.BlockSpec((None,) + st4.block_shape, lambda b, c: (layer, b, 0, 0, 0))],
        out_specs=[pl.BlockSpec(blk, lambda b, c: (b, c, 0)), st4],
        out_shape=[jax.ShapeDtypeStruct((B, T, MIX_W), F32),
                   jax.ShapeDtypeStruct((B, N_HEADS, HEAD_DIM, HEAD_DIM), F32)],
        scratch_shapes=[pltpu.VMEM((nseq, MIX_W, MIX_W), F32)] + pads,
        compiler_params=_params("parallel", "arbitrary"),
        name="hgrn",
    )(zd, zd, zd, zd, lb, gn, mall, lvl, s0_t)


def _post_body(x_ref, oa_ref, ob_ref, oc_ref, od_ref, wout_hbm, gm_ref, wup_hbm, wdn_hbm, gf_ref, y_ref,
               wout_ref, wup_ref, wdn_ref, h_s, acc_s, *, layer, tf, final):
    @pl.when(pl.program_id(0) == 0)
    def _():
        pltpu.sync_copy(wout_hbm.at[layer], wout_ref)
        pltpu.sync_copy(wup_hbm.at[layer], wup_ref)
        pltpu.sync_copy(wdn_hbm.at[layer], wdn_ref)

    mix = jnp.concatenate([o_ref[...].astype(BF16) for o_ref in (oa_ref, ob_ref, oc_ref, od_ref)], axis=1)
    x1 = x_ref[...] + _dot(mix, wout_ref[...])
    acc_s[...] = x1
    h_s[...] = _rms(x1, gm_ref[...]).astype(BF16)
    for j in range(D_FF // tf):
        up = jnp.maximum(_dot(h_s[...], wup_ref[:, j * tf:(j + 1) * tf]), 0.0)
        acc_s[...] += _dot((up * up).astype(BF16), wdn_ref[j * tf:(j + 1) * tf, :])
    y_ref[...] = _rms(acc_s[...], gf_ref[...]) if final else acc_s[...]


def _post(layer, x2d, oa, ob, oc, od, wout, gm, wup, wdn, gf, tm, tf, final):
    n = x2d.shape[0]
    row = lambda wd: pl.BlockSpec((tm, wd), lambda i: (i, 0))
    vec = pl.BlockSpec((1, D_MODEL), lambda i: (0, 0))
    hbm = pl.BlockSpec(memory_space=pl.ANY)
    return pl.pallas_call(
        functools.partial(_post_body, layer=layer, tf=tf, final=final),
        grid=(n // tm,),
        in_specs=[row(D_MODEL)] + [row(MIX_W)] * 4 + [hbm, vec, hbm, hbm, vec],
        out_specs=row(D_MODEL),
        out_shape=jax.ShapeDtypeStruct((n, D_MODEL), F32),
        scratch_shapes=[pltpu.VMEM(wout.shape[1:], BF16), pltpu.VMEM(wup.shape[1:], BF16), pltpu.VMEM(wdn.shape[1:], BF16),
                        pltpu.VMEM((tm, D_MODEL), BF16), pltpu.VMEM((tm, D_MODEL), F32)],
        compiler_params=_params("arbitrary"),
        name="post",
    )(x2d, oa, ob, oc, od, wout, gm, wup, wdn, gf)


def _rel_bucket(dist):
    max_exact = N_BUCKETS // 2
    d = jnp.maximum(dist, 1).astype(F32)
    large = max_exact + (jnp.log(d / max_exact) / math.log(MAX_WINDOW / max_exact)
                         * (N_BUCKETS - max_exact)).astype(jnp.int32)
    large = jnp.clip(large, max_exact, N_BUCKETS - 1)
    return jnp.where(dist < max_exact, dist, large)


def _pattern_bias(rel_bias, w, d):
    offs = jnp.arange(w // d + 1, dtype=jnp.int32) * d
    return rel_bias[_rel_bucket(offs)].T.astype(F32)


def _prompt_table(bias):
    cols = 2 * CHUNK
    u = jnp.concatenate([bias[:, ::-1], jnp.full((N_HEADS, cols - CHUNK), NEG, F32)], axis=1)
    return jnp.tile(u, (1, CHUNK))[:, :CHUNK * cols].reshape(N_HEADS, CHUNK, cols)


def _sample_tables(bias, d, n_new, past):
    comb = jnp.concatenate([bias[:, :, None], jnp.full((N_HEADS, CHUNK + 1, d - 1), NEG, F32)], axis=2)
    comb = comb.reshape(N_HEADS, (CHUNK + 1) * d)
    length = past + SAMPLE_ROWS + 1
    comb = comb[:, :length]
    comb = jnp.pad(comb, ((0, 0), (0, length - comb.shape[1])), constant_values=NEG)
    rev = comb[:, ::-1]
    tpast, tnew = [], []
    for row in range(SAMPLE_ROWS):
        t = min(row, n_new - 1)
        start = length - 1 - past - t
        tpast.append(rev[:, start:start + past])
        start = length - 1 - t
        tnew.append(jnp.pad(rev[:, start:start + t + 1], ((0, 0), (0, CHUNK - t - 1)), constant_values=NEG))
    return jnp.stack(tpast, axis=1), jnp.stack(tnew, axis=1)


def _pad_rows(a, rows):
    return jnp.pad(a, ((0, 0), (0, rows - a.shape[1]), (0, 0)))


def kernel(x_prompt, x_sample, cache_k_win, cache_v_win, state_mlstm_C, state_mlstm_n, state_mlstm_m, state_hgrn_S, rel_bias, w_in, w_out, g_attn, g_mlp, w_up, w_down, b_i, b_f, g_mlstm, g_cv, w_s, b_s, hgrn_lb, g_hgrn, g_final):
    depth = w_in.shape[0]
    B, T, _ = x_prompt.shape
    Bs, Ts, _ = x_sample.shape
    past = cache_k_win.shape[2]
    assert past == MAX_WINDOW and T % (MAX_WINDOW) == 0 and Ts <= SAMPLE_ROWS
    H, dh = N_HEADS, HEAD_DIM
    keep_p = min(MAX_WINDOW, T)

    sm = jax.nn.softmax(hgrn_lb.astype(F32), axis=0)
    lb_all = jnp.cumsum(sm, axis=0) - sm[0:1]
    biases = [_pattern_bias(rel_bias, w, d) for w, d in DILATIONS]
    tabs_p = jnp.stack([_prompt_table(bb) for bb in biases])
    tabs_s = [_sample_tables(bb, d, Ts, past) for bb, (_, d) in zip(biases, DILATIONS)]
    tpast = jnp.stack([a for a, _ in tabs_s])
    tnew = jnp.stack([b for _, b in tabs_s])
    kt_all = jnp.transpose(cache_k_win, (0, 1, 3, 4, 2))
    vt_all = jnp.transpose(cache_v_win, (0, 1, 3, 4, 2))

    a_end = Z_WIDTHS[0] + Z_WIDTHS[1]
    g_end = a_end + N_GATE_COLS
    zeros_c = jnp.zeros((1, B, H, dh, dh), F32)
    zeros_n = jnp.zeros((1, B, 1, MIX_W), F32)
    zeros_m = jnp.zeros((1, B, 1, H), F32)
    n0_all = state_mlstm_n.reshape(depth, Bs, 1, MIX_W)
    m0_all = state_mlstm_m.reshape(depth, Bs, 1, H)
    s0_all = jnp.swapaxes(state_hgrn_S, -1, -2)
    vec = lambda a: a.reshape(1, -1).astype(F32)
    pseq = math.gcd(B, PROMPT_SEQS_PER_STEP)
    sseq = math.gcd(Bs, SAMPLE_SEQS_PER_STEP)

    wout_b, wup_b, wdn_b = w_out.astype(BF16), w_up.astype(BF16), w_down.astype(BF16)
    xp = x_prompt.reshape(B * T, D_MODEL)
    xs = x_sample.reshape(Bs * Ts, D_MODEL)
    outs = [[] for _ in range(13)]
    new_rows = []
    for l in range(depth):
        wl = w_in[l]
        w_gate = jnp.pad(wl[:, a_end:g_end], ((0, 0), (0, GATE_W - N_GATE_COLS)))
        w_gate_lo = w_gate - w_gate.astype(BF16).astype(F32)
        w_z = jnp.concatenate([wl[:, :a_end], w_gate, w_gate_lo, wl[:, g_end:]], axis=1).astype(BF16)
        gate_bias = jnp.zeros((2, GATE_W), F32).at[0, :H].set(b_i[l]).at[1, H:2 * H].set(b_f[l])
        bs_t = b_s[l].T.astype(F32)
        final = l == depth - 1
        gf = vec(g_final)

        za, zb, zg, zc, zd = _inproj(xp, vec(g_attn[l]), w_z, 512)
        za3 = za.reshape(B, T, -1)
        oa = _attn_prompt(za3, tabs_p)
        ob, c1, n1, m1 = _mlstm(zb.reshape(B, T, -1), zg.reshape(B, T, -1), gate_bias, vec(g_mlstm[l]),
                                0, zeros_c, zeros_n, zeros_m, CHUNK, pseq)
        (oc,) = _gate(zc.reshape(B, T, -1), w_s[l], bs_t, vec(g_cv[l]), 1, GATE_ROWS, False)
        od, s1 = _hgrn(zd.reshape(B, T, -1), vec(lb_all[l]), vec(g_hgrn[l]), 0, zeros_c, CHUNK, pseq)
        xp = _post(l, xp, oa.reshape(B * T, -1), ob.reshape(B * T, -1), oc.reshape(B * T, -1), od.reshape(B * T, -1),
                   wout_b, vec(g_mlp[l]), wup_b, wdn_b, gf, 512, 1024, final)
        outs[0].append(za3[:, T - keep_p:, MIX_W:2 * MIX_W].reshape(B, keep_p, H, dh))
        outs[1].append(za3[:, T - keep_p:, 2 * MIX_W:].reshape(B, keep_p, H, dh))
        outs[4].append(c1)
        outs[5].append(n1.reshape(B, H, dh))
        outs[6].append(m1.reshape(B, H))
        outs[10].append(s1)

        za, zb, zg, zc, zd = _inproj(xs, vec(g_attn[l]), w_z, Bs * Ts)
        za3 = za.reshape(Bs, Ts, -1)
        new_t = lambda a: jnp.pad(jnp.transpose(a.reshape(Bs, Ts, H, dh), (0, 2, 3, 1)),
                                  ((0, 0), (0, 0), (0, 0), (0, LANES - Ts)))
        new_rows.append((new_t(za3[:, :, MIX_W:2 * MIX_W]), new_t(za3[:, :, 2 * MIX_W:])))
        oa, *windows = _attn_sample(l, _pad_rows(za3[:, :, :MIX_W], SAMPLE_ROWS), kt_all, vt_all,
                                    new_rows, tpast, tnew, Ts)
        seqs = lambda z: z.reshape(Bs, Ts, -1)
        ob, c2, n2, m2 = _mlstm(seqs(zb), seqs(zg), gate_bias, vec(g_mlstm[l]),
                                l, state_mlstm_C, n0_all, m0_all, Ts, sseq)
        oc, vrows = _gate(seqs(zc), w_s[l], bs_t, vec(g_cv[l]), sseq, Ts, True)
        od, s2 = _hgrn(seqs(zd), vec(lb_all[l]), vec(g_hgrn[l]), l, s0_all, Ts, sseq)
        flat = lambda o: o.reshape(Bs * Ts, MIX_W)
        xs = _post(l, xs, flat(oa[:, :Ts]), flat(ob), flat(oc), flat(od),
                   wout_b, vec(g_mlp[l]), wup_b, wdn_b, gf, Bs * Ts, 1024, final)
        outs[7].append(c2)
        outs[8].append(n2.reshape(Bs, H, dh))
        outs[9].append(m2.reshape(Bs, H))
        outs[11].append(s2)
        outs[12].append(vrows.reshape(Bs, Ts, H, dh))

    stacked = [jnp.stack(o) if o else None for o in outs]
    stacked[2], stacked[3] = (jnp.transpose(w, (0, 1, 4, 2, 3)) for w in windows)
    stacked[10], stacked[11] = (jnp.swapaxes(s, -1, -2) for s in stacked[10:12])
    return (xp.reshape(B, T, D_MODEL), xs.reshape(Bs, Ts, D_MODEL)) + tuple(stacked)
```

```python
import functools
import math

import numpy as np
import jax
import jax.numpy as jnp
from jax import lax
from jax.experimental import pallas as pl
from jax.experimental.pallas import tpu as pltpu

F32 = jnp.float32
BF16 = jnp.bfloat16

D_MODEL = 1024
N_HEADS = 4
HEAD_DIM = 64
MIX_W = N_HEADS * HEAD_DIM
DILATIONS = ((128, 1), (512, 4), (2048, 16))
MAX_WINDOW = 2048
N_BUCKETS = 32
D_FF = 4 * D_MODEL
EPS = 1e-6
NEG = -1e30
LB_FLOOR = 1e-30
CHUNK = 128
N_GATE_COLS = 2 * N_HEADS
GATE_W = 128
Z_WIDTHS = (3 * MIX_W, 4 * MIX_W, GATE_W, 2 * MIX_W, 4 * MIX_W)
Z_TOTAL = sum(Z_WIDTHS)
ATTN_SCALE = HEAD_DIM ** -0.5
N_LEVELS = 7
VMEM_LIMIT = 48 * 1024 * 1024
PROMPT_SEQS_PER_STEP = 2
SAMPLE_SEQS_PER_STEP = 4
GATE_ROWS = 4 * CHUNK


def _dot(a, b):
    return jnp.dot(a, b, preferred_element_type=F32)


def _dot_nt(a, b):
    return lax.dot_general(a, b, (((1,), (1,)), ((), ())), preferred_element_type=F32)


def _dot_tn(a, b):
    return lax.dot_general(a, b, (((0,), (0,)), ((), ())), preferred_element_type=F32)


def _rms(x, g):
    return x * lax.rsqrt(jnp.mean(x * x, axis=-1, keepdims=True) + EPS) * g


def _log_sigmoid(x):
    return jnp.minimum(x, 0.0) - jnp.log1p(jnp.exp(-jnp.abs(x)))


def _dot_01(m01, f, pieces=3):
    out = None
    for _ in range(pieces):
        piece = f.astype(BF16)
        f = f - piece.astype(F32)
        out = _dot(m01, piece) if out is None else out + _dot(m01, piece)
    return out


def _head(h):
    return slice(h * HEAD_DIM, (h + 1) * HEAD_DIM)


def _head_scores(x, y):
    rows, width = x.shape
    tile = 2 * HEAD_DIM
    low = lax.broadcasted_iota(jnp.int32, (rows, tile), 1) < HEAD_DIM
    out = []
    for half in range(width // tile):
        lanes = slice(half * tile, (half + 1) * tile)
        xh = x[:, lanes].astype(BF16)
        zero = jnp.zeros_like(xh)
        x2 = jnp.concatenate([jnp.where(low, xh, zero), jnp.where(low, zero, xh)], axis=0)
        out.append(_dot_nt(x2, y[:, lanes].astype(BF16)))
    return jnp.concatenate(out, axis=0)


def _chunk_tile(ref, i, pad_ref):
    if pad_ref is None:
        return ref[i]
    pad_ref[0:ref.shape[1], :] = ref[i]
    return pad_ref[...]


def _zero_pads(pads):
    for p in pads:
        p[...] = jnp.zeros_like(p)


def _params(*sem):
    return pltpu.CompilerParams(dimension_semantics=sem, vmem_limit_bytes=VMEM_LIMIT)


def _inproj_body(x_ref, g_ref, w_ref, *out_refs):
    hn = _rms(x_ref[...], g_ref[...])
    h = hn.astype(BF16)
    off = 0
    for o_ref in out_refs:
        n = o_ref.shape[-1]
        if n == GATE_W:
            h_lo = (hn - h.astype(F32)).astype(BF16)
            z2 = _dot(h, w_ref[:, off:off + 2 * n])
            o_ref[...] = z2[:, :n] + z2[:, n:] + _dot(h_lo, w_ref[:, off:off + n])
            off += 2 * n
        else:
            o_ref[...] = _dot(h, w_ref[:, off:off + n])
            off += n


def _inproj(x2d, g, w, tm):
    n = x2d.shape[0]
    return pl.pallas_call(
        _inproj_body,
        grid=(n // tm,),
        in_specs=[pl.BlockSpec((tm, D_MODEL), lambda i: (i, 0)),
                  pl.BlockSpec((1, D_MODEL), lambda i: (0, 0)),
                  pl.BlockSpec((D_MODEL, Z_TOTAL + GATE_W), lambda i: (0, 0))],
        out_specs=[pl.BlockSpec((tm, wd), lambda i: (i, 0)) for wd in Z_WIDTHS],
        out_shape=[jax.ShapeDtypeStruct((n, wd), F32) for wd in Z_WIDTHS],
        compiler_params=_params("parallel"),
        name="inproj",
    )(x2d, g, w)


ATTN_ROWS = MAX_WINDOW
LANES = 128
UNITS_PER_ITER = 4
UNITS_PER_CHAIN = 2


def _attn_body(q0, q1, kp0, kp1, kc0, kc1, vp0, vp1, vc0, vc1, tab_ref, o_ref, os_ref, ls_ref):
    first = pl.program_id(1) == 0
    low_head = lax.broadcasted_iota(jnp.int32, (CHUNK, LANES), 1) < HEAD_DIM
    ones = jnp.ones((2 * CHUNK, LANES), BF16)
    q_refs, kp_refs, kc_refs, vp_refs, vc_refs = (q0, q1), (kp0, kp1), (kc0, kc1), (vp0, vp1), (vc0, vc1)

    def rows_of(start, d):
        if d == 1:
            return pl.ds(start if isinstance(start, int) else pl.multiple_of(start, CHUNK), CHUNK)
        return pl.ds(start, CHUNK, stride=d)

    def units(p, d, specs):
        loaded, scores = [], []
        for r, c, prev_in_block in specs:
            start = r + d * CHUNK * c
            cur = rows_of(start, d)
            prev = rows_of(start - d * CHUNK, d) if prev_in_block else rows_of(r + ATTN_ROWS - d * CHUNK, d)
            k_prev_ref, v_prev_ref = (kc_refs, vc_refs) if prev_in_block else (kp_refs, vp_refs)
            halves, vws = [], []
            for half in range(2):
                q = (q_refs[half][0, cur, :] * ATTN_SCALE).astype(BF16)
                kw = jnp.concatenate([k_prev_ref[half][0, prev, :], kc_refs[half][0, cur, :]], axis=0).astype(BF16)
                vws.append(jnp.concatenate([v_prev_ref[half][0, prev, :], vc_refs[half][0, cur, :]],
                                           axis=0).astype(BF16))
                zero = jnp.zeros_like(q)
                qm = jnp.concatenate([jnp.where(low_head, q, zero), jnp.where(low_head, zero, q)], axis=0)
                halves.append(_dot_nt(qm, kw))
            s = jnp.concatenate(halves, axis=0) + tab_ref[p]
            if not prev_in_block:
                s = jnp.concatenate([jnp.where(first, NEG, s[:, :CHUNK]), s[:, CHUNK:]], axis=1)
            loaded.append((cur, vws))
            scores.append(s)
        s = jnp.concatenate(scores, axis=0)
        m = jnp.max(jnp.maximum(s[:, :CHUNK], s[:, CHUNK:]), axis=1, keepdims=True)
        e = jnp.exp(s - m).astype(BF16)
        for u, (cur, vws) in enumerate(loaded):
            for half in range(2):
                base = (u * N_HEADS + 2 * half) * CHUNK
                rows = slice(base, base + 2 * CHUNK)
                od = _dot(e[rows], jnp.concatenate([vws[half], ones], axis=1))
                den = od[:, LANES:]
                o = od[:, :LANES] / den
                lse = m[rows] + jnp.log(den)
                os_ref[p, half, cur, :] = jnp.where(low_head, o[:CHUNK], o[CHUNK:])
                ls_ref[p, half, cur, :] = jnp.where(low_head, lse[:CHUNK], lse[CHUNK:])

    def run(p, d, specs):
        for j in range(0, len(specs), UNITS_PER_CHAIN):
            units(p, d, specs[j:j + UNITS_PER_CHAIN])

    for p, (_, d) in enumerate(DILATIONS):
        nblk = ATTN_ROWS // d // CHUNK
        if nblk >= UNITS_PER_ITER:
            def per_subsequence(r, carry, p=p, d=d, nblk=nblk):
                run(p, d, [(r, c, c > 0) for c in range(UNITS_PER_ITER)])

                def group(g, carry2):
                    run(p, d, [(r, g * UNITS_PER_ITER + u, True) for u in range(UNITS_PER_ITER)])
                    return carry2
                return lax.fori_loop(1, nblk // UNITS_PER_ITER, group, carry)
            if d == 1:
                per_subsequence(0, 0)
            else:
                lax.fori_loop(0, d, per_subsequence, 0)
        else:
            def group(g, carry, p=p, d=d):
                run(p, d, [(g * UNITS_PER_ITER + u, 0, False) for u in range(UNITS_PER_ITER)])
                return carry
            lax.fori_loop(0, d // UNITS_PER_ITER, group, 0)

    def merge(i, carry):
        rows = pl.ds(pl.multiple_of(i * CHUNK, CHUNK), CHUNK)
        for half in range(2):
            ls = [ls_ref[p, half, rows, :] for p in range(len(DILATIONS))]
            mx = jnp.maximum(jnp.maximum(ls[0], ls[1]), ls[2])
            es = [jnp.exp(l - mx) for l in ls]
            num = es[0] * os_ref[0, half, rows, :] + es[1] * os_ref[1, half, rows, :] + es[2] * os_ref[2, half, rows, :]
            o_ref[0, rows, half * LANES:(half + 1) * LANES] = num / (es[0] + es[1] + es[2])
        return carry
    lax.fori_loop(0, ATTN_ROWS // CHUNK, merge, 0)


def _attn_prompt(za, tabs):
    B, T, _ = za.shape
    blk = (1, ATTN_ROWS, LANES)
    cur = lambda j: pl.BlockSpec(blk, lambda b, i: (b, i, j))
    prv = lambda j: pl.BlockSpec(blk, lambda b, i: (b, jnp.maximum(i - 1, 0), j))
    npat = len(DILATIONS)
    return pl.pallas_call(
        _attn_body,
        grid=(B, T // ATTN_ROWS),
        in_specs=[cur(0), cur(1), prv(2), prv(3), cur(2), cur(3), prv(4), prv(5), cur(4), cur(5),
                  pl.BlockSpec((npat, N_HEADS * CHUNK, 2 * CHUNK), lambda b, i: (0, 0, 0))],
        out_specs=pl.BlockSpec((1, ATTN_ROWS, MIX_W), lambda b, i: (b, i, 0)),
        out_shape=jax.ShapeDtypeStruct((B, T, MIX_W), F32),
        scratch_shapes=[pltpu.VMEM((npat, 2, ATTN_ROWS, LANES), F32)] * 2,
        compiler_params=_params("parallel", "arbitrary"),
        name="attn_prompt",
    )(*([za] * 10), tabs.reshape(npat, N_HEADS * CHUNK, 2 * CHUNK))


SAMPLE_ROWS = 8


def _attn_sample_body(q_ref, tpast_ref, tnew_ref, *rest, n_new, n_earlier, write_windows):
    kt_ref, vt_ref, knt_ref, vnt_ref = rest[:4]
    earlier = [rest[4 + 4 * e:8 + 4 * e] for e in range(n_earlier)]
    outs = rest[4 + 4 * n_earlier:]
    o_ref = outs[0]
    npat = len(DILATIONS)
    q8 = q_ref[0] * ATTN_SCALE
    for h in range(N_HEADS):
        sl = _head(h)
        qh = q8[:, sl].astype(BF16)
        kt, vt = kt_ref[0, 0, h], vt_ref[0, 0, h]
        knt, vnt = knt_ref[0, h], vnt_ref[0, h]
        sp = _dot(qh, kt.astype(BF16))
        sn = _dot(qh, knt.astype(BF16))
        pps, pns, dens, lses = [], [], [], []
        for p in range(npat):
            lp = sp + tpast_ref[p, h]
            ln = sn + tnew_ref[p, h]
            m = jnp.maximum(jnp.max(lp, axis=1, keepdims=True), jnp.max(ln, axis=1, keepdims=True))
            pps.append(jnp.exp(lp - m))
            pns.append(jnp.exp(ln - m))
            dens.append(jnp.sum(pps[-1], axis=1, keepdims=True) + jnp.sum(pns[-1], axis=1, keepdims=True))
            lses.append(m + jnp.log(dens[-1]))
        pv = (_dot_nt(jnp.concatenate(pps, axis=0).astype(BF16), vt.astype(BF16))
              + _dot_nt(jnp.concatenate(pns, axis=0).astype(BF16), vnt.astype(BF16)))
        mx = jnp.maximum(jnp.maximum(lses[0], lses[1]), lses[2])
        es = [jnp.exp(l - mx) for l in lses]
        num = sum(es[p] * pv[p * SAMPLE_ROWS:(p + 1) * SAMPLE_ROWS] / dens[p] for p in range(npat))
        o_ref[0, :, sl] = num / (es[0] + es[1] + es[2])
        if write_windows:
            slabs = earlier + [(kt_ref, vt_ref, knt_ref, vnt_ref)]
            for layer, (kt_l, vt_l, knt_l, vnt_l) in enumerate(slabs):
                outs[1][layer, 0, h] = jnp.concatenate([kt_l[0, 0, h][:, n_new:], knt_l[0, h][:, :n_new]], axis=1)
                outs[2][layer, 0, h] = jnp.concatenate([vt_l[0, 0, h][:, n_new:], vnt_l[0, h][:, :n_new]], axis=1)


def _attn_sample(layer, q8, kt_all, vt_all, new_rows, tpast, tnew, n_new):
    depth, B, _, _, P = kt_all.shape
    npat = len(DILATIONS)
    write_windows = layer == depth - 1
    n_earlier = layer if write_windows else 0
    cache = lambda l: pl.BlockSpec((1, 1, N_HEADS, HEAD_DIM, P), lambda b: (l, b, 0, 0, 0))
    new = pl.BlockSpec((1, N_HEADS, HEAD_DIM, LANES), lambda b: (b, 0, 0, 0))
    rows = pl.BlockSpec((1, SAMPLE_ROWS, MIX_W), lambda b: (b, 0, 0))
    in_specs = [rows,
                pl.BlockSpec((npat, N_HEADS, SAMPLE_ROWS, P), lambda b: (0, 0, 0, 0)),
                pl.BlockSpec((npat, N_HEADS, SAMPLE_ROWS, LANES), lambda b: (0, 0, 0, 0))]
    args = [q8, tpast, tnew]
    for l in [layer] + list(range(n_earlier)):
        in_specs += [cache(l), cache(l), new, new]
        args += [kt_all, vt_all, *new_rows[l]]
    out_specs = [rows]
    out_shape = [jax.ShapeDtypeStruct((B, SAMPLE_ROWS, MIX_W), F32)]
    if write_windows:
        out_specs += [pl.BlockSpec((depth, 1, N_HEADS, HEAD_DIM, P), lambda b: (0, b, 0, 0, 0))] * 2
        out_shape += [jax.ShapeDtypeStruct(kt_all.shape, F32)] * 2
    return pl.pallas_call(
        functools.partial(_attn_sample_body, n_new=n_new, n_earlier=n_earlier, write_windows=write_windows),
        grid=(B,),
        in_specs=in_specs,
        out_specs=out_specs,
        out_shape=out_shape,
        compiler_params=_params("parallel"),
        name="attn_sample",
    )(*args)


def _mlstm_body(zq_ref, zk_ref, zv_ref, zo_ref, zg_ref, bias_ref, gn_ref, c0_ref, n0_ref, m0_ref,
                out_ref, c_out, n_out, m_out, c_s, n_s, m_s, *pads, n_valid):
    c = pl.program_id(1)
    nseq = zq_ref.shape[0]
    pq, pk, pv, po, pg = pads if pads else (None,) * 5
    _zero_pads(pads)

    @pl.when(c == 0)
    def _():
        c_s[...] = jnp.zeros_like(c_s)
        for i in range(nseq):
            for h in range(N_HEADS):
                c_s[i, _head(h), _head(h)] = c0_ref[i, h]
        n_s[...] = n0_ref[...]
        m_s[...] = m0_ref[...]

    rows4 = N_HEADS * CHUNK
    t4 = lax.broadcasted_iota(jnp.int32, (rows4, CHUNK), 0) & (CHUNK - 1)
    causal4 = t4 >= lax.broadcasted_iota(jnp.int32, (rows4, CHUNK), 1)
    valid4 = t4[:, 0:1] < n_valid
    tril = (lax.broadcasted_iota(jnp.int32, (CHUNK, CHUNK), 0)
            >= lax.broadcasted_iota(jnp.int32, (CHUNK, CHUNK), 1)).astype(BF16)
    lane_head = lax.broadcasted_iota(jnp.int32, (CHUNK, MIX_W), 1) // HEAD_DIM
    same_head = (lax.broadcasted_iota(jnp.int32, (MIX_W, MIX_W), 0) // HEAD_DIM
                 == lax.broadcasted_iota(jnp.int32, (MIX_W, MIX_W), 1) // HEAD_DIM)
    head_ones = same_head.astype(BF16)
    ones_l = jnp.ones((CHUNK, MIX_W), BF16)
    last = n_valid - 1

    def stack(f):
        return jnp.concatenate([f(h) for h in range(N_HEADS)], axis=0)

    def on_head_lanes(f, lanes):
        out = f(N_HEADS - 1)
        for h in range(N_HEADS - 2, -1, -1):
            out = jnp.where(lanes == h, f(h), out)
        return out

    for i in range(nseq):
        gates = _chunk_tile(zg_ref, i, pg)
        ipre = gates + bias_ref[0:1, :]
        logf = _log_sigmoid(gates + bias_ref[1:2, :])
        bcum = _dot_01(tril, logf)
        bcum_t = bcum.T
        ipre_t = ipre.T
        q = _chunk_tile(zq_ref, i, pq)
        k = _chunk_tile(zk_ref, i, pk) * ATTN_SCALE
        v = _chunk_tile(zv_ref, i, pv)
        kb, vb = k.astype(BF16), v.astype(BF16)
        cbd = c_s[i]
        nrow = n_s[i]
        mprev = m_s[i]
        bcol = stack(lambda h: bcum[:, N_HEADS + h:N_HEADS + h + 1])
        icol = stack(lambda h: ipre[:, h:h + 1])
        brow = stack(lambda h: jnp.broadcast_to(bcum_t[N_HEADS + h:N_HEADS + h + 1, :], (CHUNK, CHUNK)))
        irow = stack(lambda h: jnp.broadcast_to(ipre_t[h:h + 1, :], (CHUNK, CHUNK)))
        mprev4 = stack(lambda h: jnp.broadcast_to(mprev[:, h:h + 1], (CHUNK, 1)))
        dmat = jnp.where(causal4, bcol - brow + irow, NEG)
        g = bcol + mprev4
        mt = jnp.maximum(g, jnp.max(dmat, axis=1, keepdims=True))
        dexp = jnp.exp(dmat - mt)
        gexp = jnp.exp(g - mt)
        sb = (_head_scores(q, k) * dexp).astype(BF16)
        q_c = _dot_nt(q.astype(BF16), cbd.astype(BF16))
        q_n = _dot((q * nrow).astype(BF16), head_ones)
        num = _dot(sb, vb) + gexp * jnp.concatenate([q_c] * N_HEADS, axis=0)
        nq = _dot(sb, ones_l) + gexp * jnp.concatenate([q_n] * N_HEADS, axis=0)
        hh4 = num / jnp.maximum(jnp.abs(nq), jnp.exp(-mt))
        hh = on_head_lanes(lambda h: hh4[h * CHUNK:(h + 1) * CHUNK], lane_head)
        sq = hh * hh
        sq_hi = sq.astype(BF16)
        sq_lo = (sq - sq_hi.astype(F32)).astype(BF16)
        ms = (_dot(sq_hi, head_ones) + _dot(sq_lo, head_ones)) * (1.0 / HEAD_DIM)
        out = jax.nn.sigmoid(_chunk_tile(zo_ref, i, po)) * (hh * lax.rsqrt(ms + EPS) * gn_ref[...])
        out_ref[i] = out[0:out_ref.shape[1]]

        mnew = [mt[h * CHUNK + last:h * CHUNK + last + 1, :] for h in range(N_HEADS)]
        blast = [bcol[h * CHUNK + last:h * CHUNK + last + 1, :] for h in range(N_HEADS)]
        wk = jnp.exp(stack(lambda h: jnp.broadcast_to(blast[h] - mnew[h], (CHUNK, 1))) - bcol + icol)
        if n_valid < CHUNK:
            wk = jnp.where(valid4, wk, 0.0)
        wk_w = on_head_lanes(lambda h: jnp.broadcast_to(wk[h * CHUNK:(h + 1) * CHUNK], (CHUNK, MIX_W)), lane_head)
        dc = on_head_lanes(lambda h: jnp.broadcast_to(jnp.exp(blast[h] + mprev[:, h:h + 1] - mnew[h]), (1, MIX_W)),
                           lane_head[0:1])
        c_s[i] = dc * cbd + jnp.where(same_head, _dot_tn((v * wk_w).astype(BF16), kb), 0.0)
        n_s[i] = dc * nrow + jnp.sum(wk_w * k, axis=0, keepdims=True)
        for h in range(N_HEADS):
            m_s[i, :, h:h + 1] = mnew[h]

    @pl.when(c == pl.num_programs(1) - 1)
    def _():
        for i in range(nseq):
            for h in range(N_HEADS):
                c_out[i, h] = c_s[i, _head(h), _head(h)]
        n_out[...] = n_s[...]
        m_out[...] = m_s[...]


def _mlstm(zb, zg, bias, gn, layer, c0, n0, m0, n_valid, nseq):
    B, T, _ = zb.shape
    rows = min(T, CHUNK)
    nc = T // rows
    blk = (nseq, rows, MIX_W)
    zspec = lambda j: pl.BlockSpec(blk, lambda b, c: (b, c, j))
    pads = [pltpu.VMEM((CHUNK, wd), F32) for wd in (MIX_W,) * 4 + (GATE_W,)] if rows < CHUNK else []
    st4 = pl.BlockSpec((nseq, N_HEADS, HEAD_DIM, HEAD_DIM), lambda b, c: (b, 0, 0, 0))
    st3 = pl.BlockSpec((nseq, 1, MIX_W), lambda b, c: (b, 0, 0))
    st2 = pl.BlockSpec((nseq, 1, N_HEADS), lambda b, c: (b, 0, 0))
    init = lambda spec: pl.BlockSpec((None,) + spec.block_shape,
                                     lambda b, c: (layer, b) + (0,) * (len(spec.block_shape) - 1))
    return pl.pallas_call(
        functools.partial(_mlstm_body, n_valid=n_valid),
        grid=(B // nseq, nc),
        in_specs=[zspec(0), zspec(1), zspec(2), zspec(3),
                  pl.BlockSpec((nseq, rows, GATE_W), lambda b, c: (b, c, 0)),
                  pl.BlockSpec((2, GATE_W), lambda b, c: (0, 0)),
                  pl.BlockSpec((1, MIX_W), lambda b, c: (0, 0)),
                  init(st4), init(st3), init(st2)],
        out_specs=[pl.BlockSpec(blk, lambda b, c: (b, c, 0)), st4, st3, st2],
        out_shape=[jax.ShapeDtypeStruct((B, T, MIX_W), F32),
                   jax.ShapeDtypeStruct((B, N_HEADS, HEAD_DIM, HEAD_DIM), F32),
                   jax.ShapeDtypeStruct((B, 1, MIX_W), F32),
                   jax.ShapeDtypeStruct((B, 1, N_HEADS), F32)],
        scratch_shapes=[pltpu.VMEM((nseq, MIX_W, MIX_W), F32),
                        pltpu.VMEM((nseq, 1, MIX_W), F32),
                        pltpu.VMEM((nseq, 1, N_HEADS), F32)] + pads,
        compiler_params=_params("parallel", "arbitrary"),
        name="mlstm",
    )(zb, zb, zb, zb, zg, bias, gn, c0, n0, m0)


def _gate_body(zc_ref, w_ref, bs_ref, gcv_ref, *rest, want_vrows):
    out_ref = rest[0]
    vrow_ref = rest[1] if want_vrows else None
    pad = rest[-1] if zc_ref.shape[1] < CHUNK else None
    if pad is not None:
        _zero_pads([pad])
    row = lax.broadcasted_iota(jnp.int32, (CHUNK, CHUNK), 0)
    col = lax.broadcasted_iota(jnp.int32, (CHUNK, CHUNK), 1)
    lane_head = lax.broadcasted_iota(jnp.int32, (CHUNK, MIX_W), 1) // HEAD_DIM
    ws = [jnp.where(row >= col, w_ref[h], 0.0).astype(BF16) for h in range(N_HEADS)]
    rows_per_chunk = min(zc_ref.shape[1], CHUNK)
    for i in range(zc_ref.shape[0]):
        for j in range(zc_ref.shape[1] // rows_per_chunk):
            rows = slice(j * rows_per_chunk, (j + 1) * rows_per_chunk)
            z = zc_ref[i, rows, :] if pad is None else _chunk_tile(zc_ref, i, pad)
            u = jax.nn.gelu(z[:, :MIX_W])
            vn = _rms(jax.nn.gelu(z[:, MIX_W:]), gcv_ref[...])
            if want_vrows:
                vrow_ref[i, rows, :] = vn[0:rows_per_chunk]
            vb = vn.astype(BF16)
            s = _dot(ws[N_HEADS - 1], vb) + bs_ref[:, N_HEADS - 1:N_HEADS]
            for h in range(N_HEADS - 2, -1, -1):
                s = jnp.where(lane_head == h, _dot(ws[h], vb) + bs_ref[:, h:h + 1], s)
            out_ref[i, rows, :] = (u * s)[0:rows_per_chunk]


def _gate(zc, w_s, bs_t, gcv, nseq, rows, want_vrows):
    B, T, _ = zc.shape
    blk = (nseq, rows, MIX_W)
    n_out = 2 if want_vrows else 1
    return pl.pallas_call(
        functools.partial(_gate_body, want_vrows=want_vrows),
        grid=(B // nseq, T // rows),
        in_specs=[pl.BlockSpec((nseq, rows, 2 * MIX_W), lambda b, c: (b, c, 0)),
                  pl.BlockSpec((N_HEADS, CHUNK, CHUNK), lambda b, c: (0, 0, 0)),
                  pl.BlockSpec((CHUNK, N_HEADS), lambda b, c: (0, 0)),
                  pl.BlockSpec((1, MIX_W), lambda b, c: (0, 0))],
        out_specs=[pl.BlockSpec(blk, lambda b, c: (b, c, 0))] * n_out,
        out_shape=[jax.ShapeDtypeStruct((B, T, MIX_W), F32)] * n_out,
        scratch_shapes=[pltpu.VMEM((CHUNK, 2 * MIX_W), F32)] if rows < CHUNK else [],
        compiler_params=_params("parallel", "parallel"),
        name="gate",
    )(zc, w_s, bs_t, gcv)


def _hgrn_levels(n_valid):
    return [l for l in range(N_LEVELS) if (CHUNK >> (l + 1)) < max(n_valid, 2)]


def _hgrn_tables(n_valid):
    p = np.arange(CHUNK)[:, None]
    u = np.arange(CHUNK)[None, :]
    mats = []
    for l in _hgrn_levels(n_valid):
        m = CHUNK >> (l + 1)
        start = (p // m) * m
        odd = ((p // m) % 2) == 1
        mats.append(np.where(odd, (u >= start) & (u <= p), (u > p) & (u <= start + m - 1)))
    mats.append(u <= p)
    mats.append((u > p) & (u <= n_valid - 1))
    mall = np.concatenate(mats, axis=0).astype(np.float32)
    t = np.arange(CHUNK)[:, None]
    s = np.arange(CHUNK)[None, :]
    x = t ^ s
    top = np.floor(np.log2(np.maximum(x, 1))).astype(np.int32)
    lvl = np.where(s < t, N_LEVELS - 1 - top, np.where(s == t, N_LEVELS, N_LEVELS + 1)).astype(np.int32)
    return jnp.asarray(mall, BF16), jnp.asarray(np.tile(lvl, (N_HEADS, 1)))


def _hgrn_body(zq_ref, zf_ref, zi_ref, zg_ref, lb_ref, gn_ref, mall_ref, lvl_ref, s0_ref,
               out_ref, s_out, s_s, *pads, n_valid):
    c = pl.program_id(1)
    nseq = zq_ref.shape[0]
    pq, pf, pi, pg = pads if pads else (None,) * 4
    _zero_pads(pads)

    @pl.when(c == 0)
    def _():
        s_s[...] = jnp.zeros_like(s_s)
        for i in range(nseq):
            for h in range(N_HEADS):
                s_s[i, _head(h), _head(h)] = s0_ref[i, h]

    lb = lb_ref[...]
    lb_floor = jnp.maximum(lb, LB_FLOOR)
    lvl = lvl_ref[...]
    valid_rows = lax.broadcasted_iota(jnp.int32, (CHUNK, 1), 0) < n_valid
    lane_head = lax.broadcasted_iota(jnp.int32, (CHUNK, MIX_W), 1) // HEAD_DIM
    same_head = (lax.broadcasted_iota(jnp.int32, (MIX_W, MIX_W), 0) // HEAD_DIM
                 == lax.broadcasted_iota(jnp.int32, (MIX_W, MIX_W), 1) // HEAD_DIM)
    head_ones = same_head.astype(BF16)
    levels = _hgrn_levels(n_valid)
    at_level = [lvl == l for l in levels]
    on_diagonal = lvl == N_LEVELS

    for i in range(nseq):
        q = _chunk_tile(zq_ref, i, pq)
        fx = _chunk_tile(zf_ref, i, pf)
        sig = jax.nn.sigmoid(fx)
        logf = jnp.log(lb_floor + (1.0 - lb) * sig)
        kd = (1.0 - lb) * (1.0 - sig)
        gsum = _dot_01(mall_ref[...], logf, pieces=2)
        amat = None
        for j, l in enumerate(levels):
            e = jnp.exp(gsum[j * CHUNK:(j + 1) * CHUNK])
            amat = jnp.where(at_level[j], _head_scores(q * e, kd * e), 0.0 if j == 0 else amat)
        amat = jnp.where(on_diagonal, _head_scores(q, kd), amat)
        vb = _chunk_tile(zi_ref, i, pi).astype(BF16)
        bcum = gsum[len(levels) * CHUNK:(len(levels) + 1) * CHUNK]
        k_out = jnp.where(valid_rows, kd * jnp.exp(gsum[(len(levels) + 1) * CHUNK:]), 0.0).astype(BF16)
        e_last = jnp.exp(bcum[n_valid - 1:n_valid, :])
        sbd = s_s[i]
        o4 = _dot(amat.astype(BF16), vb)
        o = o4[(N_HEADS - 1) * CHUNK:]
        for h in range(N_HEADS - 2, -1, -1):
            o = jnp.where(lane_head == h, o4[h * CHUNK:(h + 1) * CHUNK], o)
        o = o + _dot_nt((q * jnp.exp(bcum)).astype(BF16), sbd.astype(BF16))
        sq = o * o
        sq_hi = sq.astype(BF16)
        sq_lo = (sq - sq_hi.astype(F32)).astype(BF16)
        ms = (_dot(sq_hi, head_ones) + _dot(sq_lo, head_ones)) * (1.0 / HEAD_DIM)
        gate = _chunk_tile(zg_ref, i, pg)
        out = o * lax.rsqrt(ms + EPS) * gn_ref[...] * (gate * jax.nn.sigmoid(gate))
        out_ref[i] = out[0:out_ref.shape[1]]
        s_s[i] = e_last * sbd + jnp.where(same_head, _dot_tn(vb, k_out), 0.0)

    @pl.when(c == pl.num_programs(1) - 1)
    def _():
        for i in range(nseq):
            for h in range(N_HEADS):
                s_out[i, h] = s_s[i, _head(h), _head(h)]


def _hgrn(zd, lb, gn, layer, s0_t, n_valid, nseq):
    B, T, _ = zd.shape
    mall, lvl = _hgrn_tables(n_valid)
    rows = min(T, CHUNK)
    blk = (nseq, rows, MIX_W)
    zspec = lambda j: pl.BlockSpec(blk, lambda b, c: (b, c, j))
    pads = [pltpu.VMEM((CHUNK, MIX_W), F32)] * 4 if rows < CHUNK else []
    st4 = pl.BlockSpec((nseq, N_HEADS, HEAD_DIM, HEAD_DIM), lambda b, c: (b, 0, 0, 0))
    vec = pl.BlockSpec((1, MIX_W), lambda b, c: (0, 0))
    return pl.pallas_call(
        functools.partial(_hgrn_body, n_valid=n_valid),
        grid=(B // nseq, T // rows),
        in_specs=[zspec(0), zspec(1), zspec(2), zspec(3), vec, vec,
                  pl.BlockSpec(mall.shape, lambda b, c: (0, 0)),
                  pl.BlockSpec(lvl.shape, lambda b, c: (0, 0)),
                  pl.BlockSpec((None,) + st4.block_shape, lambda b, c: (layer, b, 0, 0, 0))],
        out_specs=[pl.BlockSpec(blk, lambda b, c: (b, c, 0)), st4],
        out_shape=[jax.ShapeDtypeStruct((B, T, MIX_W), F32),
                   jax.ShapeDtypeStruct((B, N_HEADS, HEAD_DIM, HEAD_DIM), F32)],
        scratch_shapes=[pltpu.VMEM((nseq, MIX_W, MIX_W), F32)] + pads,
        compiler_params=_params("parallel", "arbitrary"),
        name="hgrn",
    )(zd, zd, zd, zd, lb, gn, mall, lvl, s0_t)


def _post_body(x_ref, oa_ref, ob_ref, oc_ref, od_ref, wout_hbm, gm_ref, wup_hbm, wdn_hbm, gf_ref, y_ref,
               wout_ref, wup_ref, wdn_ref, h_s, acc_s, *, layer, tf, final):
    @pl.when(pl.program_id(0) == 0)
    def _():
        pltpu.sync_copy(wout_hbm.at[layer], wout_ref)
        pltpu.sync_copy(wup_hbm.at[layer], wup_ref)
        pltpu.sync_copy(wdn_hbm.at[layer], wdn_ref)

    mix = jnp.concatenate([o_ref[...].astype(BF16) for o_ref in (oa_ref, ob_ref, oc_ref, od_ref)], axis=1)
    x1 = x_ref[...] + _dot(mix, wout_ref[...])
    acc_s[...] = x1
    h_s[...] = _rms(x1, gm_ref[...]).astype(BF16)
    for j in range(D_FF // tf):
        up = jnp.maximum(_dot(h_s[...], wup_ref[:, j * tf:(j + 1) * tf]), 0.0)
        acc_s[...] += _dot((up * up).astype(BF16), wdn_ref[j * tf:(j + 1) * tf, :])
    y_ref[...] = _rms(acc_s[...], gf_ref[...]) if final else acc_s[...]


def _post(layer, x2d, oa, ob, oc, od, wout, gm, wup, wdn, gf, tm, tf, final):
    n = x2d.shape[0]
    row = lambda wd: pl.BlockSpec((tm, wd), lambda i: (i, 0))
    vec = pl.BlockSpec((1, D_MODEL), lambda i: (0, 0))
    hbm = pl.BlockSpec(memory_space=pl.ANY)
    return pl.pallas_call(
        functools.partial(_post_body, layer=layer, tf=tf, final=final),
        grid=(n // tm,),
        in_specs=[row(D_MODEL)] + [row(MIX_W)] * 4 + [hbm, vec, hbm, hbm, vec],
        out_specs=row(D_MODEL),
        out_shape=jax.ShapeDtypeStruct((n, D_MODEL), F32),
        scratch_shapes=[pltpu.VMEM(wout.shape[1:], BF16), pltpu.VMEM(wup.shape[1:], BF16), pltpu.VMEM(wdn.shape[1:], BF16),
                        pltpu.VMEM((tm, D_MODEL), BF16), pltpu.VMEM((tm, D_MODEL), F32)],
        compiler_params=_params("arbitrary"),
        name="post",
    )(x2d, oa, ob, oc, od, wout, gm, wup, wdn, gf)


def _rel_bucket(dist):
    max_exact = N_BUCKETS // 2
    d = jnp.maximum(dist, 1).astype(F32)
    large = max_exact + (jnp.log(d / max_exact) / math.log(MAX_WINDOW / max_exact)
                         * (N_BUCKETS - max_exact)).astype(jnp.int32)
    large = jnp.clip(large, max_exact, N_BUCKETS - 1)
    return jnp.where(dist < max_exact, dist, large)


def _pattern_bias(rel_bias, w, d):
    offs = jnp.arange(w // d + 1, dtype=jnp.int32) * d
    return rel_bias[_rel_bucket(offs)].T.astype(F32)


def _prompt_table(bias):
    cols = 2 * CHUNK
    u = jnp.concatenate([bias[:, ::-1], jnp.full((N_HEADS, cols - CHUNK), NEG, F32)], axis=1)
    return jnp.tile(u, (1, CHUNK))[:, :CHUNK * cols].reshape(N_HEADS, CHUNK, cols)


def _sample_tables(bias, d, n_new, past):
    comb = jnp.concatenate([bias[:, :, None], jnp.full((N_HEADS, CHUNK + 1, d - 1), NEG, F32)], axis=2)
    comb = comb.reshape(N_HEADS, (CHUNK + 1) * d)
    length = past + SAMPLE_ROWS + 1
    comb = comb[:, :length]
    comb = jnp.pad(comb, ((0, 0), (0, length - comb.shape[1])), constant_values=NEG)
    rev = comb[:, ::-1]
    tpast, tnew = [], []
    for row in range(SAMPLE_ROWS):
        t = min(row, n_new - 1)
        start = length - 1 - past - t
        tpast.append(rev[:, start:start + past])
        start = length - 1 - t
        tnew.append(jnp.pad(rev[:, start:start + t + 1], ((0, 0), (0, CHUNK - t - 1)), constant_values=NEG))
    return jnp.stack(tpast, axis=1), jnp.stack(tnew, axis=1)


def _pad_rows(a, rows):
    return jnp.pad(a, ((0, 0), (0, rows - a.shape[1]), (0, 0)))


def kernel(x_prompt, x_sample, cache_k_win, cache_v_win, state_mlstm_C, state_mlstm_n, state_mlstm_m, state_hgrn_S, rel_bias, w_in, w_out, g_attn, g_mlp, w_up, w_down, b_i, b_f, g_mlstm, g_cv, w_s, b_s, hgrn_lb, g_hgrn, g_final):
    depth = w_in.shape[0]
    B, T, _ = x_prompt.shape
    Bs, Ts, _ = x_sample.shape
    past = cache_k_win.shape[2]
    assert past == MAX_WINDOW and T % (MAX_WINDOW) == 0 and Ts <= SAMPLE_ROWS
    H, dh = N_HEADS, HEAD_DIM
    keep_p = min(MAX_WINDOW, T)

    sm = jax.nn.softmax(hgrn_lb.astype(F32), axis=0)
    lb_all = jnp.cumsum(sm, axis=0) - sm[0:1]
    biases = [_pattern_bias(rel_bias, w, d) for w, d in DILATIONS]
    tabs_p = jnp.stack([_prompt_table(bb) for bb in biases])
    tabs_s = [_sample_tables(bb, d, Ts, past) for bb, (_, d) in zip(biases, DILATIONS)]
    tpast = jnp.stack([a for a, _ in tabs_s])
    tnew = jnp.stack([b for _, b in tabs_s])
    kt_all = jnp.transpose(cache_k_win, (0, 1, 3, 4, 2))
    vt_all = jnp.transpose(cache_v_win, (0, 1, 3, 4, 2))

    a_end = Z_WIDTHS[0] + Z_WIDTHS[1]
    g_end = a_end + N_GATE_COLS
    zeros_c = jnp.zeros((1, B, H, dh, dh), F32)
    zeros_n = jnp.zeros((1, B, 1, MIX_W), F32)
    zeros_m = jnp.zeros((1, B, 1, H), F32)
    n0_all = state_mlstm_n.reshape(depth, Bs, 1, MIX_W)
    m0_all = state_mlstm_m.reshape(depth, Bs, 1, H)
    s0_all = jnp.swapaxes(state_hgrn_S, -1, -2)
    vec = lambda a: a.reshape(1, -1).astype(F32)
    pseq = math.gcd(B, PROMPT_SEQS_PER_STEP)
    sseq = math.gcd(Bs, SAMPLE_SEQS_PER_STEP)

    wout_b, wup_b, wdn_b = w_out.astype(BF16), w_up.astype(BF16), w_down.astype(BF16)
    xp = x_prompt.reshape(B * T, D_MODEL)
    xs = x_sample.reshape(Bs * Ts, D_MODEL)
    outs = [[] for _ in range(13)]
    new_rows = []
    for l in range(depth):
        wl = w_in[l]
        w_gate = jnp.pad(wl[:, a_end:g_end], ((0, 0), (0, GATE_W - N_GATE_COLS)))
        w_gate_lo = w_gate - w_gate.astype(BF16).astype(F32)
        w_z = jnp.concatenate([wl[:, :a_end], w_gate, w_gate_lo, wl[:, g_end:]], axis=1).astype(BF16)
        gate_bias = jnp.zeros((2, GATE_W), F32).at[0, :H].set(b_i[l]).at[1, H:2 * H].set(b_f[l])
        bs_t = b_s[l].T.astype(F32)
        final = l == depth - 1
        gf = vec(g_final)

        za, zb, zg, zc, zd = _inproj(xp, vec(g_attn[l]), w_z, 512)
        za3 = za.reshape(B, T, -1)
        oa = _attn_prompt(za3, tabs_p)
        ob, c1, n1, m1 = _mlstm(zb.reshape(B, T, -1), zg.reshape(B, T, -1), gate_bias, vec(g_mlstm[l]),
                                0, zeros_c, zeros_n, zeros_m, CHUNK, pseq)
        (oc,) = _gate(zc.reshape(B, T, -1), w_s[l], bs_t, vec(g_cv[l]), 1, GATE_ROWS, False)
        od, s1 = _hgrn(zd.reshape(B, T, -1), vec(lb_all[l]), vec(g_hgrn[l]), 0, zeros_c, CHUNK, pseq)
        xp = _post(l, xp, oa.reshape(B * T, -1), ob.reshape(B * T, -1), oc.reshape(B * T, -1), od.reshape(B * T, -1),
                   wout_b, vec(g_mlp[l]), wup_b, wdn_b, gf, 512, 1024, final)
        outs[0].append(za3[:, T - keep_p:, MIX_W:2 * MIX_W].reshape(B, keep_p, H, dh))
        outs[1].append(za3[:, T - keep_p:, 2 * MIX_W:].reshape(B, keep_p, H, dh))
        outs[4].append(c1)
        outs[5].append(n1.reshape(B, H, dh))
        outs[6].append(m1.reshape(B, H))
        outs[10].append(s1)

        za, zb, zg, zc, zd = _inproj(xs, vec(g_attn[l]), w_z, Bs * Ts)
        za3 = za.reshape(Bs, Ts, -1)
        new_t = lambda a: jnp.pad(jnp.transpose(a.reshape(Bs, Ts, H, dh), (0, 2, 3, 1)),
                                  ((0, 0), (0, 0), (0, 0), (0, LANES - Ts)))
        new_rows.append((new_t(za3[:, :, MIX_W:2 * MIX_W]), new_t(za3[:, :, 2 * MIX_W:])))
        oa, *windows = _attn_sample(l, _pad_rows(za3[:, :, :MIX_W], SAMPLE_ROWS), kt_all, vt_all,
                                    new_rows, tpast, tnew, Ts)
        seqs = lambda z: z.reshape(Bs, Ts, -1)
        ob, c2, n2, m2 = _mlstm(seqs(zb), seqs(zg), gate_bias, vec(g_mlstm[l]),
                                l, state_mlstm_C, n0_all, m0_all, Ts, sseq)
        oc, vrows = _gate(seqs(zc), w_s[l], bs_t, vec(g_cv[l]), sseq, Ts, True)
        od, s2 = _hgrn(seqs(zd), vec(lb_all[l]), vec(g_hgrn[l]), l, s0_all, Ts, sseq)
        flat = lambda o: o.reshape(Bs * Ts, MIX_W)
        xs = _post(l, xs, flat(oa[:, :Ts]), flat(ob), flat(oc), flat(od),
                   wout_b, vec(g_mlp[l]), wup_b, wdn_b, gf, Bs * Ts, 1024, final)
        outs[7].append(c2)
        outs[8].append(n2.reshape(Bs, H, dh))
        outs[9].append(m2.reshape(Bs, H))
        outs[11].append(s2)
        outs[12].append(vrows.reshape(Bs, Ts, H, dh))

    stacked = [jnp.stack(o) if o else None for o in outs]
    stacked[2], stacked[3] = (jnp.transpose(w, (0, 1, 4, 2, 3)) for w in windows)
    stacked[10], stacked[11] = (jnp.swapaxes(s, -1, -2) for s in stacked[10:12])
    return (xp.reshape(B, T, D_MODEL), xs.reshape(Bs, Ts, D_MODEL)) + tuple(stacked)
```

```python
import functools
import math

import numpy as np
import jax
import jax.numpy as jnp
from jax import lax
from jax.experimental import pallas as pl
from jax.experimental.pallas import tpu as pltpu

F32 = jnp.float32
BF16 = jnp.bfloat16

D_MODEL = 1024
N_HEADS = 4
HEAD_DIM = 64
MIX_W = N_HEADS * HEAD_DIM
DILATIONS = ((128, 1), (512, 4), (2048, 16))
MAX_WINDOW = 2048
N_BUCKETS = 32
D_FF = 4 * D_MODEL
EPS = 1e-6
NEG = -1e30
LB_FLOOR = 1e-30
CHUNK = 128
N_GATE_COLS = 2 * N_HEADS
GATE_W = 128
Z_WIDTHS = (3 * MIX_W, 4 * MIX_W, GATE_W, 2 * MIX_W, 4 * MIX_W)
Z_TOTAL = sum(Z_WIDTHS)
ATTN_SCALE = HEAD_DIM ** -0.5
N_LEVELS = 7
VMEM_LIMIT = 48 * 1024 * 1024
PROMPT_SEQS_PER_STEP = 2
SAMPLE_SEQS_PER_STEP = 4
GATE_ROWS = 4 * CHUNK
SEQS_PER_CHAIN = 2


def _dot(a, b):
    return jnp.dot(a, b, preferred_element_type=F32)


def _dot_nt(a, b):
    return lax.dot_general(a, b, (((1,), (1,)), ((), ())), preferred_element_type=F32)


def _dot_tn(a, b):
    return lax.dot_general(a, b, (((0,), (0,)), ((), ())), preferred_element_type=F32)


def _rms(x, g):
    return x * lax.rsqrt(jnp.mean(x * x, axis=-1, keepdims=True) + EPS) * g


def _log_sigmoid(x):
    return jnp.minimum(x, 0.0) - jnp.log1p(jnp.exp(-jnp.abs(x)))


def _dot_01(m01, f, pieces=3):
    out = None
    for _ in range(pieces):
        piece = f.astype(BF16)
        f = f - piece.astype(F32)
        out = _dot(m01, piece) if out is None else out + _dot(m01, piece)
    return out


def _head(h):
    return slice(h * HEAD_DIM, (h + 1) * HEAD_DIM)


def _head_scores(x, y):
    rows, width = x.shape
    tile = 2 * HEAD_DIM
    low = lax.broadcasted_iota(jnp.int32, (rows, tile), 1) < HEAD_DIM
    out = []
    for half in range(width // tile):
        lanes = slice(half * tile, (half + 1) * tile)
        xh = x[:, lanes].astype(BF16)
        zero = jnp.zeros_like(xh)
        x2 = jnp.concatenate([jnp.where(low, xh, zero), jnp.where(low, zero, xh)], axis=0)
        out.append(_dot_nt(x2, y[:, lanes].astype(BF16)))
    return jnp.concatenate(out, axis=0)


def _chunk_tile(ref, i, pad_ref):
    if pad_ref is None:
        return ref[i]
    pad_ref[0:ref.shape[1], :] = ref[i]
    return pad_ref[...]


def _zero_pads(pads):
    for p in pads:
        p[...] = jnp.zeros_like(p)


def _params(*sem):
    return pltpu.CompilerParams(dimension_semantics=sem, vmem_limit_bytes=VMEM_LIMIT)


def _inproj_body(x_ref, g_ref, w_ref, *out_refs):
    hn = _rms(x_ref[...], g_ref[...])
    h = hn.astype(BF16)
    off = 0
    for o_ref in out_refs:
        n = o_ref.shape[-1]
        if n == GATE_W:
            h_lo = (hn - h.astype(F32)).astype(BF16)
            z2 = _dot(h, w_ref[:, off:off + 2 * n])
            o_ref[...] = z2[:, :n] + z2[:, n:] + _dot(h_lo, w_ref[:, off:off + n])
            off += 2 * n
        else:
            o_ref[...] = _dot(h, w_ref[:, off:off + n])
            off += n


def _inproj(x2d, g, w, tm):
    n = x2d.shape[0]
    return pl.pallas_call(
        _inproj_body,
        grid=(n // tm,),
        in_specs=[pl.BlockSpec((tm, D_MODEL), lambda i: (i, 0)),
                  pl.BlockSpec((1, D_MODEL), lambda i: (0, 0)),
                  pl.BlockSpec((D_MODEL, Z_TOTAL + GATE_W), lambda i: (0, 0))],
        out_specs=[pl.BlockSpec((tm, wd), lambda i: (i, 0)) for wd in Z_WIDTHS],
        out_shape=[jax.ShapeDtypeStruct((n, wd), F32) for wd in Z_WIDTHS],
        compiler_params=_params("parallel"),
        name="inproj",
    )(x2d, g, w)


ATTN_ROWS = MAX_WINDOW
LANES = 128
UNITS_PER_ITER = 4
UNITS_PER_CHAIN = 2


def _attn_body(q0, q1, kp0, kp1, kc0, kc1, vp0, vp1, vc0, vc1, tab_ref, o_ref, os_ref, ls_ref):
    first = pl.program_id(1) == 0
    low_head = lax.broadcasted_iota(jnp.int32, (CHUNK, LANES), 1) < HEAD_DIM
    ones = jnp.ones((2 * CHUNK, LANES), BF16)
    q_refs, kp_refs, kc_refs, vp_refs, vc_refs = (q0, q1), (kp0, kp1), (kc0, kc1), (vp0, vp1), (vc0, vc1)

    def rows_of(start, d):
        if d == 1:
            return pl.ds(start if isinstance(start, int) else pl.multiple_of(start, CHUNK), CHUNK)
        return pl.ds(start, CHUNK, stride=d)

    def units(p, d, specs):
        loaded, scores = [], []
        for r, c, prev_in_block in specs:
            start = r + d * CHUNK * c
            cur = rows_of(start, d)
            prev = rows_of(start - d * CHUNK, d) if prev_in_block else rows_of(r + ATTN_ROWS - d * CHUNK, d)
            k_prev_ref, v_prev_ref = (kc_refs, vc_refs) if prev_in_block else (kp_refs, vp_refs)
            halves, vws = [], []
            for half in range(2):
                q = (q_refs[half][0, cur, :] * ATTN_SCALE).astype(BF16)
                kw = jnp.concatenate([k_prev_ref[half][0, prev, :], kc_refs[half][0, cur, :]], axis=0).astype(BF16)
                vws.append(jnp.concatenate([v_prev_ref[half][0, prev, :], vc_refs[half][0, cur, :]],
                                           axis=0).astype(BF16))
                zero = jnp.zeros_like(q)
                qm = jnp.concatenate([jnp.where(low_head, q, zero), jnp.where(low_head, zero, q)], axis=0)
                halves.append(_dot_nt(qm, kw))
            s = jnp.concatenate(halves, axis=0) + tab_ref[p]
            if not prev_in_block:
                s = jnp.concatenate([jnp.where(first, NEG, s[:, :CHUNK]), s[:, CHUNK:]], axis=1)
            loaded.append((cur, vws))
            scores.append(s)
        s = jnp.concatenate(scores, axis=0)
        m = jnp.max(jnp.maximum(s[:, :CHUNK], s[:, CHUNK:]), axis=1, keepdims=True)
        e = jnp.exp(s - m).astype(BF16)
        for u, (cur, vws) in enumerate(loaded):
            for half in range(2):
                base = (u * N_HEADS + 2 * half) * CHUNK
                rows = slice(base, base + 2 * CHUNK)
                od = _dot(e[rows], jnp.concatenate([vws[half], ones], axis=1))
                den = od[:, LANES:]
                o = od[:, :LANES] / den
                lse = m[rows] + jnp.log(den)
                os_ref[p, half, cur, :] = jnp.where(low_head, o[:CHUNK], o[CHUNK:])
                ls_ref[p, half, cur, :] = jnp.where(low_head, lse[:CHUNK], lse[CHUNK:])

    def run(p, d, specs):
        for j in range(0, len(specs), UNITS_PER_CHAIN):
            units(p, d, specs[j:j + UNITS_PER_CHAIN])

    for p, (_, d) in enumerate(DILATIONS):
        nblk = ATTN_ROWS // d // CHUNK
        if nblk >= UNITS_PER_ITER:
            def per_subsequence(r, carry, p=p, d=d, nblk=nblk):
                run(p, d, [(r, c, c > 0) for c in range(UNITS_PER_ITER)])

                def group(g, carry2):
                    run(p, d, [(r, g * UNITS_PER_ITER + u, True) for u in range(UNITS_PER_ITER)])
                    return carry2
                return lax.fori_loop(1, nblk // UNITS_PER_ITER, group, carry)
            if d == 1:
                per_subsequence(0, 0)
            else:
                lax.fori_loop(0, d, per_subsequence, 0)
        else:
            def group(g, carry, p=p, d=d):
                run(p, d, [(g * UNITS_PER_ITER + u, 0, False) for u in range(UNITS_PER_ITER)])
                return carry
            lax.fori_loop(0, d // UNITS_PER_ITER, group, 0)

    def merge(i, carry):
        rows = pl.ds(pl.multiple_of(i * CHUNK, CHUNK), CHUNK)
        for half in range(2):
            ls = [ls_ref[p, half, rows, :] for p in range(len(DILATIONS))]
            mx = jnp.maximum(jnp.maximum(ls[0], ls[1]), ls[2])
            es = [jnp.exp(l - mx) for l in ls]
            num = es[0] * os_ref[0, half, rows, :] + es[1] * os_ref[1, half, rows, :] + es[2] * os_ref[2, half, rows, :]
            o_ref[0, rows, half * LANES:(half + 1) * LANES] = num / (es[0] + es[1] + es[2])
        return carry
    lax.fori_loop(0, ATTN_ROWS // CHUNK, merge, 0)


def _attn_prompt(za, tabs):
    B, T, _ = za.shape
    blk = (1, ATTN_ROWS, LANES)
    cur = lambda j: pl.BlockSpec(blk, lambda b, i: (b, i, j))
    prv = lambda j: pl.BlockSpec(blk, lambda b, i: (b, jnp.maximum(i - 1, 0), j))
    npat = len(DILATIONS)
    return pl.pallas_call(
        _attn_body,
        grid=(B, T // ATTN_ROWS),
        in_specs=[cur(0), cur(1), prv(2), prv(3), cur(2), cur(3), prv(4), prv(5), cur(4), cur(5),
                  pl.BlockSpec((npat, N_HEADS * CHUNK, 2 * CHUNK), lambda b, i: (0, 0, 0))],
        out_specs=pl.BlockSpec((1, ATTN_ROWS, MIX_W), lambda b, i: (b, i, 0)),
        out_shape=jax.ShapeDtypeStruct((B, T, MIX_W), F32),
        scratch_shapes=[pltpu.VMEM((npat, 2, ATTN_ROWS, LANES), F32)] * 2,
        compiler_params=_params("parallel", "arbitrary"),
        name="attn_prompt",
    )(*([za] * 10), tabs.reshape(npat, N_HEADS * CHUNK, 2 * CHUNK))


SAMPLE_ROWS = 8


def _attn_sample_body(q_ref, tpast_ref, tnew_ref, *rest, n_new, n_earlier, write_windows):
    kt_ref, vt_ref, knt_ref, vnt_ref = rest[:4]
    earlier = [rest[4 + 4 * e:8 + 4 * e] for e in range(n_earlier)]
    outs = rest[4 + 4 * n_earlier:]
    o_ref = outs[0]
    npat = len(DILATIONS)
    q8 = q_ref[0] * ATTN_SCALE
    for h in range(N_HEADS):
        sl = _head(h)
        qh = q8[:, sl].astype(BF16)
        kt, vt = kt_ref[0, 0, h], vt_ref[0, 0, h]
        knt, vnt = knt_ref[0, h], vnt_ref[0, h]
        sp = _dot(qh, kt.astype(BF16))
        sn = _dot(qh, knt.astype(BF16))
        pps, pns, dens, lses = [], [], [], []
        for p in range(npat):
            lp = sp + tpast_ref[p, h]
            ln = sn + tnew_ref[p, h]
            m = jnp.maximum(jnp.max(lp, axis=1, keepdims=True), jnp.max(ln, axis=1, keepdims=True))
            pps.append(jnp.exp(lp - m))
            pns.append(jnp.exp(ln - m))
            dens.append(jnp.sum(pps[-1], axis=1, keepdims=True) + jnp.sum(pns[-1], axis=1, keepdims=True))
            lses.append(m + jnp.log(dens[-1]))
        pv = (_dot_nt(jnp.concatenate(pps, axis=0).astype(BF16), vt.astype(BF16))
              + _dot_nt(jnp.concatenate(pns, axis=0).astype(BF16), vnt.astype(BF16)))
        mx = jnp.maximum(jnp.maximum(lses[0], lses[1]), lses[2])
        es = [jnp.exp(l - mx) for l in lses]
        num = sum(es[p] * pv[p * SAMPLE_ROWS:(p + 1) * SAMPLE_ROWS] / dens[p] for p in range(npat))
        o_ref[0, :, sl] = num / (es[0] + es[1] + es[2])
        if write_windows:
            slabs = earlier + [(kt_ref, vt_ref, knt_ref, vnt_ref)]
            for layer, (kt_l, vt_l, knt_l, vnt_l) in enumerate(slabs):
                outs[1][layer, 0, h] = jnp.concatenate([kt_l[0, 0, h][:, n_new:], knt_l[0, h][:, :n_new]], axis=1)
                outs[2][layer, 0, h] = jnp.concatenate([vt_l[0, 0, h][:, n_new:], vnt_l[0, h][:, :n_new]], axis=1)


def _attn_sample(layer, q8, kt_all, vt_all, new_rows, tpast, tnew, n_new):
    depth, B, _, _, P = kt_all.shape
    npat = len(DILATIONS)
    write_windows = layer == depth - 1
    n_earlier = layer if write_windows else 0
    cache = lambda l: pl.BlockSpec((1, 1, N_HEADS, HEAD_DIM, P), lambda b: (l, b, 0, 0, 0))
    new = pl.BlockSpec((1, N_HEADS, HEAD_DIM, LANES), lambda b: (b, 0, 0, 0))
    rows = pl.BlockSpec((1, SAMPLE_ROWS, MIX_W), lambda b: (b, 0, 0))
    in_specs = [rows,
                pl.BlockSpec((npat, N_HEADS, SAMPLE_ROWS, P), lambda b: (0, 0, 0, 0)),
                pl.BlockSpec((npat, N_HEADS, SAMPLE_ROWS, LANES), lambda b: (0, 0, 0, 0))]
    args = [q8, tpast, tnew]
    for l in [layer] + list(range(n_earlier)):
        in_specs += [cache(l), cache(l), new, new]
        args += [kt_all, vt_all, *new_rows[l]]
    out_specs = [rows]
    out_shape = [jax.ShapeDtypeStruct((B, SAMPLE_ROWS, MIX_W), F32)]
    if write_windows:
        out_specs += [pl.BlockSpec((depth, 1, N_HEADS, HEAD_DIM, P), lambda b: (0, b, 0, 0, 0))] * 2
        out_shape += [jax.ShapeDtypeStruct(kt_all.shape, F32)] * 2
    return pl.pallas_call(
        functools.partial(_attn_sample_body, n_new=n_new, n_earlier=n_earlier, write_windows=write_windows),
        grid=(B,),
        in_specs=in_specs,
        out_specs=out_specs,
        out_shape=out_shape,
        compiler_params=_params("parallel"),
        name="attn_sample",
    )(*args)


def _mlstm_body(zq_ref, zk_ref, zv_ref, zo_ref, zg_ref, bias_ref, gn_ref, c0_ref, n0_ref, m0_ref,
                out_ref, c_out, n_out, m_out, c_s, n_s, m_s, *pads, n_valid):
    c = pl.program_id(1)
    nseq = zq_ref.shape[0]
    pq, pk, pv, po, pg = pads if pads else (None,) * 5
    _zero_pads(pads)

    @pl.when(c == 0)
    def _():
        c_s[...] = jnp.zeros_like(c_s)
        for i in range(nseq):
            for h in range(N_HEADS):
                c_s[i, _head(h), _head(h)] = c0_ref[i, h]
        n_s[...] = n0_ref[...]
        m_s[...] = m0_ref[...]

    rows4 = N_HEADS * CHUNK
    t4 = lax.broadcasted_iota(jnp.int32, (rows4, CHUNK), 0) & (CHUNK - 1)
    causal4 = t4 >= lax.broadcasted_iota(jnp.int32, (rows4, CHUNK), 1)
    valid4 = t4[:, 0:1] < n_valid
    tril = (lax.broadcasted_iota(jnp.int32, (CHUNK, CHUNK), 0)
            >= lax.broadcasted_iota(jnp.int32, (CHUNK, CHUNK), 1)).astype(BF16)
    lane_head = lax.broadcasted_iota(jnp.int32, (CHUNK, MIX_W), 1) // HEAD_DIM
    same_head = (lax.broadcasted_iota(jnp.int32, (MIX_W, MIX_W), 0) // HEAD_DIM
                 == lax.broadcasted_iota(jnp.int32, (MIX_W, MIX_W), 1) // HEAD_DIM)
    head_ones = same_head.astype(BF16)
    ones_l = jnp.ones((CHUNK, MIX_W), BF16)
    last = n_valid - 1

    def stack(f):
        return jnp.concatenate([f(h) for h in range(N_HEADS)], axis=0)

    def on_head_lanes(f, lanes):
        out = f(N_HEADS - 1)
        for h in range(N_HEADS - 2, -1, -1):
            out = jnp.where(lanes == h, f(h), out)
        return out

    for first_seq in range(0, nseq, SEQS_PER_CHAIN):
        group = list(range(first_seq, min(first_seq + SEQS_PER_CHAIN, nseq)))
        pre = []
        for i in group:
            gates = _chunk_tile(zg_ref, i, pg)
            ipre = gates + bias_ref[0:1, :]
            logf = _log_sigmoid(gates + bias_ref[1:2, :])
            bcum = _dot_01(tril, logf)
            bcum_t = bcum.T
            ipre_t = ipre.T
            q = _chunk_tile(zq_ref, i, pq)
            k = _chunk_tile(zk_ref, i, pk) * ATTN_SCALE
            v = _chunk_tile(zv_ref, i, pv)
            cbd = c_s[i]
            nrow = n_s[i]
            mprev = m_s[i]
            pre.append(dict(
                k=k, v=v, kb=k.astype(BF16), vb=v.astype(BF16), cbd=cbd, nrow=nrow, mprev=mprev,
                bcol=stack(lambda h: bcum[:, N_HEADS + h:N_HEADS + h + 1]),
                icol=stack(lambda h: ipre[:, h:h + 1]),
                brow=stack(lambda h: jnp.broadcast_to(bcum_t[N_HEADS + h:N_HEADS + h + 1, :], (CHUNK, CHUNK))),
                irow=stack(lambda h: jnp.broadcast_to(ipre_t[h:h + 1, :], (CHUNK, CHUNK))),
                mprev4=stack(lambda h: jnp.broadcast_to(mprev[:, h:h + 1], (CHUNK, 1))),
                scores=_head_scores(q, k),
                q_c=_dot_nt(q.astype(BF16), cbd.astype(BF16)),
                q_n=_dot((q * nrow).astype(BF16), head_ones)))

        cat = lambda key: jnp.concatenate([p[key] for p in pre], axis=0)
        bcol_g = cat("bcol")
        dmat = jnp.where(jnp.concatenate([causal4] * len(pre), axis=0), bcol_g - cat("brow") + cat("irow"), NEG)
        g = bcol_g + cat("mprev4")
        mt_g = jnp.maximum(g, jnp.max(dmat, axis=1, keepdims=True))
        sb_g = (cat("scores") * jnp.exp(dmat - mt_g)).astype(BF16)
        gexp_g = jnp.exp(g - mt_g)
        floor_g = jnp.exp(-mt_g)

        for j, (i, p) in enumerate(zip(group, pre)):
            rows = slice(j * rows4, (j + 1) * rows4)
            sb, gexp, mt, bcol, icol, mprev = sb_g[rows], gexp_g[rows], mt_g[rows], p["bcol"], p["icol"], p["mprev"]
            num = _dot(sb, p["vb"]) + gexp * jnp.concatenate([p["q_c"]] * N_HEADS, axis=0)
            nq = _dot(sb, ones_l) + gexp * jnp.concatenate([p["q_n"]] * N_HEADS, axis=0)
            hh4 = num / jnp.maximum(jnp.abs(nq), floor_g[rows])
            hh = on_head_lanes(lambda h: hh4[h * CHUNK:(h + 1) * CHUNK], lane_head)
            sq = hh * hh
            sq_hi = sq.astype(BF16)
            sq_lo = (sq - sq_hi.astype(F32)).astype(BF16)
            ms = (_dot(sq_hi, head_ones) + _dot(sq_lo, head_ones)) * (1.0 / HEAD_DIM)
            out = jax.nn.sigmoid(_chunk_tile(zo_ref, i, po)) * (hh * lax.rsqrt(ms + EPS) * gn_ref[...])
            out_ref[i] = out[0:out_ref.shape[1]]

            mnew = [mt[h * CHUNK + last:h * CHUNK + last + 1, :] for h in range(N_HEADS)]
            blast = [bcol[h * CHUNK + last:h * CHUNK + last + 1, :] for h in range(N_HEADS)]
            wk = jnp.exp(stack(lambda h: jnp.broadcast_to(blast[h] - mnew[h], (CHUNK, 1))) - bcol + icol)
            if n_valid < CHUNK:
                wk = jnp.where(valid4, wk, 0.0)
            wk_w = on_head_lanes(lambda h: jnp.broadcast_to(wk[h * CHUNK:(h + 1) * CHUNK], (CHUNK, MIX_W)), lane_head)
            dc = on_head_lanes(
                lambda h: jnp.broadcast_to(jnp.exp(blast[h] + mprev[:, h:h + 1] - mnew[h]), (1, MIX_W)), lane_head[0:1])
            c_s[i] = dc * p["cbd"] + jnp.where(same_head, _dot_tn((p["v"] * wk_w).astype(BF16), p["kb"]), 0.0)
            n_s[i] = dc * p["nrow"] + jnp.sum(wk_w * p["k"], axis=0, keepdims=True)
            for h in range(N_HEADS):
                m_s[i, :, h:h + 1] = mnew[h]

    @pl.when(c == pl.num_programs(1) - 1)
    def _():
        for i in range(nseq):
            for h in range(N_HEADS):
                c_out[i, h] = c_s[i, _head(h), _head(h)]
        n_out[...] = n_s[...]
        m_out[...] = m_s[...]


def _mlstm(zb, zg, bias, gn, layer, c0, n0, m0, n_valid, nseq):
    B, T, _ = zb.shape
    rows = min(T, CHUNK)
    nc = T // rows
    blk = (nseq, rows, MIX_W)
    zspec = lambda j: pl.BlockSpec(blk, lambda b, c: (b, c, j))
    pads = [pltpu.VMEM((CHUNK, wd), F32) for wd in (MIX_W,) * 4 + (GATE_W,)] if rows < CHUNK else []
    st4 = pl.BlockSpec((nseq, N_HEADS, HEAD_DIM, HEAD_DIM), lambda b, c: (b, 0, 0, 0))
    st3 = pl.BlockSpec((nseq, 1, MIX_W), lambda b, c: (b, 0, 0))
    st2 = pl.BlockSpec((nseq, 1, N_HEADS), lambda b, c: (b, 0, 0))
    init = lambda spec: pl.BlockSpec((None,) + spec.block_shape,
                                     lambda b, c: (layer, b) + (0,) * (len(spec.block_shape) - 1))
    return pl.pallas_call(
        functools.partial(_mlstm_body, n_valid=n_valid),
        grid=(B // nseq, nc),
        in_specs=[zspec(0), zspec(1), zspec(2), zspec(3),
                  pl.BlockSpec((nseq, rows, GATE_W), lambda b, c: (b, c, 0)),
                  pl.BlockSpec((2, GATE_W), lambda b, c: (0, 0)),
                  pl.BlockSpec((1, MIX_W), lambda b, c: (0, 0)),
                  init(st4), init(st3), init(st2)],
        out_specs=[pl.BlockSpec(blk, lambda b, c: (b, c, 0)), st4, st3, st2],
        out_shape=[jax.ShapeDtypeStruct((B, T, MIX_W), F32),
                   jax.ShapeDtypeStruct((B, N_HEADS, HEAD_DIM, HEAD_DIM), F32),
                   jax.ShapeDtypeStruct((B, 1, MIX_W), F32),
                   jax.ShapeDtypeStruct((B, 1, N_HEADS), F32)],
        scratch_shapes=[pltpu.VMEM((nseq, MIX_W, MIX_W), F32),
                        pltpu.VMEM((nseq, 1, MIX_W), F32),
                        pltpu.VMEM((nseq, 1, N_HEADS), F32)] + pads,
        compiler_params=_params("parallel", "arbitrary"),
        name="mlstm",
    )(zb, zb, zb, zb, zg, bias, gn, c0, n0, m0)


def _gate_body(zc_ref, w_ref, bs_ref, gcv_ref, *rest, want_vrows):
    out_ref = rest[0]
    vrow_ref = rest[1] if want_vrows else None
    pad = rest[-1] if zc_ref.shape[1] < CHUNK else None
    if pad is not None:
        _zero_pads([pad])
    row = lax.broadcasted_iota(jnp.int32, (CHUNK, CHUNK), 0)
    col = lax.broadcasted_iota(jnp.int32, (CHUNK, CHUNK), 1)
    lane_head = lax.broadcasted_iota(jnp.int32, (CHUNK, MIX_W), 1) // HEAD_DIM
    ws = [jnp.where(row >= col, w_ref[h], 0.0).astype(BF16) for h in range(N_HEADS)]
    n_rows = zc_ref.shape[1]
    for i in range(zc_ref.shape[0]):
        z = zc_ref[i] if pad is None else _chunk_tile(zc_ref, i, pad)
        u = jax.nn.gelu(z[:, :MIX_W])
        vn = _rms(jax.nn.gelu(z[:, MIX_W:]), gcv_ref[...])
        if want_vrows:
            vrow_ref[i] = vn[0:n_rows]
        vb = vn.astype(BF16)
        mixed = []
        for j in range(z.shape[0] // CHUNK):
            vj = vb[j * CHUNK:(j + 1) * CHUNK]
            s = _dot(ws[N_HEADS - 1], vj) + bs_ref[:, N_HEADS - 1:N_HEADS]
            for h in range(N_HEADS - 2, -1, -1):
                s = jnp.where(lane_head == h, _dot(ws[h], vj) + bs_ref[:, h:h + 1], s)
            mixed.append(s)
        out_ref[i] = (u * jnp.concatenate(mixed, axis=0))[0:n_rows]


def _gate(zc, w_s, bs_t, gcv, nseq, rows, want_vrows):
    B, T, _ = zc.shape
    blk = (nseq, rows, MIX_W)
    n_out = 2 if want_vrows else 1
    return pl.pallas_call(
        functools.partial(_gate_body, want_vrows=want_vrows),
        grid=(B // nseq, T // rows),
        in_specs=[pl.BlockSpec((nseq, rows, 2 * MIX_W), lambda b, c: (b, c, 0)),
                  pl.BlockSpec((N_HEADS, CHUNK, CHUNK), lambda b, c: (0, 0, 0)),
                  pl.BlockSpec((CHUNK, N_HEADS), lambda b, c: (0, 0)),
                  pl.BlockSpec((1, MIX_W), lambda b, c: (0, 0))],
        out_specs=[pl.BlockSpec(blk, lambda b, c: (b, c, 0))] * n_out,
        out_shape=[jax.ShapeDtypeStruct((B, T, MIX_W), F32)] * n_out,
        scratch_shapes=[pltpu.VMEM((CHUNK, 2 * MIX_W), F32)] if rows < CHUNK else [],
        compiler_params=_params("parallel", "parallel"),
        name="gate",
    )(zc, w_s, bs_t, gcv)


def _hgrn_levels(n_valid):
    return [l for l in range(N_LEVELS) if (CHUNK >> (l + 1)) < max(n_valid, 2)]


def _hgrn_tables(n_valid):
    p = np.arange(CHUNK)[:, None]
    u = np.arange(CHUNK)[None, :]
    mats = []
    for l in _hgrn_levels(n_valid):
        m = CHUNK >> (l + 1)
        start = (p // m) * m
        odd = ((p // m) % 2) == 1
        mats.append(np.where(odd, (u >= start) & (u <= p), (u > p) & (u <= start + m - 1)))
    mats.append(u <= p)
    mats.append((u > p) & (u <= n_valid - 1))
    mall = np.concatenate(mats, axis=0).astype(np.float32)
    t = np.arange(CHUNK)[:, None]
    s = np.arange(CHUNK)[None, :]
    x = t ^ s
    top = np.floor(np.log2(np.maximum(x, 1))).astype(np.int32)
    lvl = np.where(s < t, N_LEVELS - 1 - top, np.where(s == t, N_LEVELS, N_LEVELS + 1)).astype(np.int32)
    return jnp.asarray(mall, BF16), jnp.asarray(np.tile(lvl, (N_HEADS, 1)))


def _hgrn_body(zq_ref, zf_ref, zi_ref, zg_ref, lb_ref, gn_ref, mall_ref, lvl_ref, s0_ref,
               out_ref, s_out, s_s, *pads, n_valid):
    c = pl.program_id(1)
    nseq = zq_ref.shape[0]
    pq, pf, pi, pg = pads if pads else (None,) * 4
    _zero_pads(pads)

    @pl.when(c == 0)
    def _():
        s_s[...] = jnp.zeros_like(s_s)
        for i in range(nseq):
            for h in range(N_HEADS):
                s_s[i, _head(h), _head(h)] = s0_ref[i, h]

    lb = lb_ref[...]
    lb_floor = jnp.maximum(lb, LB_FLOOR)
    lvl = lvl_ref[...]
    valid_rows = lax.broadcasted_iota(jnp.int32, (CHUNK, 1), 0) < n_valid
    lane_head = lax.broadcasted_iota(jnp.int32, (CHUNK, MIX_W), 1) // HEAD_DIM
    same_head = (lax.broadcasted_iota(jnp.int32, (MIX_W, MIX_W), 0) // HEAD_DIM
                 == lax.broadcasted_iota(jnp.int32, (MIX_W, MIX_W), 1) // HEAD_DIM)
    head_ones = same_head.astype(BF16)
    levels = _hgrn_levels(n_valid)
    at_level = [lvl == l for l in levels]
    on_diagonal = lvl == N_LEVELS

    group_of = lambda x: jnp.concatenate([x] * min(SEQS_PER_CHAIN, nseq), axis=0)
    at_level = [group_of(m) for m in at_level]
    on_diagonal = group_of(on_diagonal)
    amats = {}
    for first_seq in range(0, nseq, SEQS_PER_CHAIN):
        group = list(range(first_seq, min(first_seq + SEQS_PER_CHAIN, nseq)))
        pre = []
        for i in group:
            q = _chunk_tile(zq_ref, i, pq)
            fx = _chunk_tile(zf_ref, i, pf)
            sig = jax.nn.sigmoid(fx)
            logf = jnp.log(lb_floor + (1.0 - lb) * sig)
            kd = (1.0 - lb) * (1.0 - sig)
            pre.append((q, kd, _dot_01(mall_ref[...], logf, pieces=2)))
        amat = None
        for j, l in enumerate(levels):
            scores = []
            for q, kd, gsum in pre:
                e = jnp.exp(gsum[j * CHUNK:(j + 1) * CHUNK])
                scores.append(_head_scores(q * e, kd * e))
            amat = jnp.where(at_level[j], jnp.concatenate(scores, axis=0), 0.0 if j == 0 else amat)
        amat = jnp.where(on_diagonal, jnp.concatenate([_head_scores(q, kd) for q, kd, _ in pre], axis=0), amat)
        for j, (i, p) in enumerate(zip(group, pre)):
            amats[i] = (amat[j * N_HEADS * CHUNK:(j + 1) * N_HEADS * CHUNK],) + p

    for i in range(nseq):
        amat, q, kd, gsum = amats[i]
        vb = _chunk_tile(zi_ref, i, pi).astype(BF16)
        bcum = gsum[len(levels) * CHUNK:(len(levels) + 1) * CHUNK]
        k_out = jnp.where(valid_rows, kd * jnp.exp(gsum[(len(levels) + 1) * CHUNK:]), 0.0).astype(BF16)
        e_last = jnp.exp(bcum[n_valid - 1:n_valid, :])
        sbd = s_s[i]
        o4 = _dot(amat.astype(BF16), vb)
        o = o4[(N_HEADS - 1) * CHUNK:]
        for h in range(N_HEADS - 2, -1, -1):
            o = jnp.where(lane_head == h, o4[h * CHUNK:(h + 1) * CHUNK], o)
        o = o + _dot_nt((q * jnp.exp(bcum)).astype(BF16), sbd.astype(BF16))
        sq = o * o
        sq_hi = sq.astype(BF16)
        sq_lo = (sq - sq_hi.astype(F32)).astype(BF16)
        ms = (_dot(sq_hi, head_ones) + _dot(sq_lo, head_ones)) * (1.0 / HEAD_DIM)
        gate = _chunk_tile(zg_ref, i, pg)
        out = o * lax.rsqrt(ms + EPS) * gn_ref[...] * (gate * jax.nn.sigmoid(gate))
        out_ref[i] = out[0:out_ref.shape[1]]
        s_s[i] = e_last * sbd + jnp.where(same_head, _dot_tn(vb, k_out), 0.0)

    @pl.when(c == pl.num_programs(1) - 1)
    def _():
        for i in range(nseq):
            for h in range(N_HEADS):
                s_out[i, h] = s_s[i, _head(h), _head(h)]


def _hgrn(zd, lb, gn, layer, s0_t, n_valid, nseq):
    B, T, _ = zd.shape
    mall, lvl = _hgrn_tables(n_valid)
    rows = min(T, CHUNK)
    blk = (nseq, rows, MIX_W)
    zspec = lambda j: pl.BlockSpec(blk, lambda b, c: (b, c, j))
    pads = [pltpu.VMEM((CHUNK, MIX_W), F32)] * 4 if rows < CHUNK else []
    st4 = pl.BlockSpec((nseq, N_HEADS, HEAD_DIM, HEAD_DIM), lambda b, c: (b, 0, 0, 0))
    vec = pl.BlockSpec((1, MIX_W), lambda b, c: (0, 0))
    return pl.pallas_call(
        functools.partial(_hgrn_body, n_valid=n_valid),
        grid=(B // nseq, T // rows),
        in_specs=[zspec(0), zspec(1), zspec(2), zspec(3), vec, vec,
                  pl.BlockSpec(mall.shape, lambda b, c: (0, 0)),
                  pl.BlockSpec(lvl.shape, lambda b, c: (0, 0)),
                  pl.BlockSpec((None,) + st4.block_shape, lambda b, c: (layer, b, 0, 0, 0))],
        out_specs=[pl.BlockSpec(blk, lambda b, c: (b, c, 0)), st4],
        out_shape=[jax.ShapeDtypeStruct((B, T, MIX_W), F32),
                   jax.ShapeDtypeStruct((B, N_HEADS, HEAD_DIM, HEAD_DIM), F32)],
        scratch_shapes=[pltpu.VMEM((nseq, MIX_W, MIX_W), F32)] + pads,
        compiler_params=_params("parallel", "arbitrary"),
        name="hgrn",
    )(zd, zd, zd, zd, lb, gn, mall, lvl, s0_t)


def _post_body(x_ref, oa_ref, ob_ref, oc_ref, od_ref, wout_hbm, gm_ref, wup_hbm, wdn_hbm, gf_ref, y_ref,
               wout_ref, wup_ref, wdn_ref, h_s, acc_s, *, layer, tf, final):
    @pl.when(pl.program_id(0) == 0)
    def _():
        pltpu.sync_copy(wout_hbm.at[layer], wout_ref)
        pltpu.sync_copy(wup_hbm.at[layer], wup_ref)
        pltpu.sync_copy(wdn_hbm.at[layer], wdn_ref)

    mix = jnp.concatenate([o_ref[...].astype(BF16) for o_ref in (oa_ref, ob_ref, oc_ref, od_ref)], axis=1)
    x1 = x_ref[...] + _dot(mix, wout_ref[...])
    acc_s[...] = x1
    h_s[...] = _rms(x1, gm_ref[...]).astype(BF16)
    for j in range(D_FF // tf):
        up = jnp.maximum(_dot(h_s[...], wup_ref[:, j * tf:(j + 1) * tf]), 0.0)
        acc_s[...] += _dot((up * up).astype(BF16), wdn_ref[j * tf:(j + 1) * tf, :])
    y_ref[...] = _rms(acc_s[...], gf_ref[...]) if final else acc_s[...]


def _post(layer, x2d, oa, ob, oc, od, wout, gm, wup, wdn, gf, tm, tf, final):
    n = x2d.shape[0]
    row = lambda wd: pl.BlockSpec((tm, wd), lambda i: (i, 0))
    vec = pl.BlockSpec((1, D_MODEL), lambda i: (0, 0))
    hbm = pl.BlockSpec(memory_space=pl.ANY)
    return pl.pallas_call(
        functools.partial(_post_body, layer=layer, tf=tf, final=final),
        grid=(n // tm,),
        in_specs=[row(D_MODEL)] + [row(MIX_W)] * 4 + [hbm, vec, hbm, hbm, vec],
        out_specs=row(D_MODEL),
        out_shape=jax.ShapeDtypeStruct((n, D_MODEL), F32),
        scratch_shapes=[pltpu.VMEM(wout.shape[1:], BF16), pltpu.VMEM(wup.shape[1:], BF16), pltpu.VMEM(wdn.shape[1:], BF16),
                        pltpu.VMEM((tm, D_MODEL), BF16), pltpu.VMEM((tm, D_MODEL), F32)],
        compiler_params=_params("arbitrary"),
        name="post",
    )(x2d, oa, ob, oc, od, wout, gm, wup, wdn, gf)


def _rel_bucket(dist):
    max_exact = N_BUCKETS // 2
    d = jnp.maximum(dist, 1).astype(F32)
    large = max_exact + (jnp.log(d / max_exact) / math.log(MAX_WINDOW / max_exact)
                         * (N_BUCKETS - max_exact)).astype(jnp.int32)
    large = jnp.clip(large, max_exact, N_BUCKETS - 1)
    return jnp.where(dist < max_exact, dist, large)


def _pattern_bias(rel_bias, w, d):
    offs = jnp.arange(w // d + 1, dtype=jnp.int32) * d
    return rel_bias[_rel_bucket(offs)].T.astype(F32)


def _prompt_table(bias):
    cols = 2 * CHUNK
    u = jnp.concatenate([bias[:, ::-1], jnp.full((N_HEADS, cols - CHUNK), NEG, F32)], axis=1)
    return jnp.tile(u, (1, CHUNK))[:, :CHUNK * cols].reshape(N_HEADS, CHUNK, cols)


def _sample_tables(bias, d, n_new, past):
    comb = jnp.concatenate([bias[:, :, None], jnp.full((N_HEADS, CHUNK + 1, d - 1), NEG, F32)], axis=2)
    comb = comb.reshape(N_HEADS, (CHUNK + 1) * d)
    length = past + SAMPLE_ROWS + 1
    comb = comb[:, :length]
    comb = jnp.pad(comb, ((0, 0), (0, length - comb.shape[1])), constant_values=NEG)
    rev = comb[:, ::-1]
    tpast, tnew = [], []
    for row in range(SAMPLE_ROWS):
        t = min(row, n_new - 1)
        start = length - 1 - past - t
        tpast.append(rev[:, start:start + past])
        start = length - 1 - t
        tnew.append(jnp.pad(rev[:, start:start + t + 1], ((0, 0), (0, CHUNK - t - 1)), constant_values=NEG))
    return jnp.stack(tpast, axis=1), jnp.stack(tnew, axis=1)


def _pad_rows(a, rows):
    return jnp.pad(a, ((0, 0), (0, rows - a.shape[1]), (0, 0)))


def kernel(x_prompt, x_sample, cache_k_win, cache_v_win, state_mlstm_C, state_mlstm_n, state_mlstm_m, state_hgrn_S, rel_bias, w_in, w_out, g_attn, g_mlp, w_up, w_down, b_i, b_f, g_mlstm, g_cv, w_s, b_s, hgrn_lb, g_hgrn, g_final):
    depth = w_in.shape[0]
    B, T, _ = x_prompt.shape
    Bs, Ts, _ = x_sample.shape
    past = cache_k_win.shape[2]
    assert past == MAX_WINDOW and T % (MAX_WINDOW) == 0 and Ts <= SAMPLE_ROWS
    H, dh = N_HEADS, HEAD_DIM
    keep_p = min(MAX_WINDOW, T)

    sm = jax.nn.softmax(hgrn_lb.astype(F32), axis=0)
    lb_all = jnp.cumsum(sm, axis=0) - sm[0:1]
    biases = [_pattern_bias(rel_bias, w, d) for w, d in DILATIONS]
    tabs_p = jnp.stack([_prompt_table(bb) for bb in biases])
    tabs_s = [_sample_tables(bb, d, Ts, past) for bb, (_, d) in zip(biases, DILATIONS)]
    tpast = jnp.stack([a for a, _ in tabs_s])
    tnew = jnp.stack([b for _, b in tabs_s])
    kt_all = jnp.transpose(cache_k_win, (0, 1, 3, 4, 2))
    vt_all = jnp.transpose(cache_v_win, (0, 1, 3, 4, 2))

    a_end = Z_WIDTHS[0] + Z_WIDTHS[1]
    g_end = a_end + N_GATE_COLS
    zeros_c = jnp.zeros((1, B, H, dh, dh), F32)
    zeros_n = jnp.zeros((1, B, 1, MIX_W), F32)
    zeros_m = jnp.zeros((1, B, 1, H), F32)
    n0_all = state_mlstm_n.reshape(depth, Bs, 1, MIX_W)
    m0_all = state_mlstm_m.reshape(depth, Bs, 1, H)
    s0_all = jnp.swapaxes(state_hgrn_S, -1, -2)
    vec = lambda a: a.reshape(1, -1).astype(F32)
    pseq = math.gcd(B, PROMPT_SEQS_PER_STEP)
    sseq = math.gcd(Bs, SAMPLE_SEQS_PER_STEP)

    wout_b, wup_b, wdn_b = w_out.astype(BF16), w_up.astype(BF16), w_down.astype(BF16)
    xp = x_prompt.reshape(B * T, D_MODEL)
    xs = x_sample.reshape(Bs * Ts, D_MODEL)
    outs = [[] for _ in range(13)]
    new_rows = []
    for l in range(depth):
        wl = w_in[l]
        w_gate = jnp.pad(wl[:, a_end:g_end], ((0, 0), (0, GATE_W - N_GATE_COLS)))
        w_gate_lo = w_gate - w_gate.astype(BF16).astype(F32)
        w_z = jnp.concatenate([wl[:, :a_end], w_gate, w_gate_lo, wl[:, g_end:]], axis=1).astype(BF16)
        gate_bias = jnp.zeros((2, GATE_W), F32).at[0, :H].set(b_i[l]).at[1, H:2 * H].set(b_f[l])
        bs_t = b_s[l].T.astype(F32)
        final = l == depth - 1
        gf = vec(g_final)

        za, zb, zg, zc, zd = _inproj(xp, vec(g_attn[l]), w_z, 512)
        za3 = za.reshape(B, T, -1)
        oa = _attn_prompt(za3, tabs_p)
        ob, c1, n1, m1 = _mlstm(zb.reshape(B, T, -1), zg.reshape(B, T, -1), gate_bias, vec(g_mlstm[l]),
                                0, zeros_c, zeros_n, zeros_m, CHUNK, pseq)
        (oc,) = _gate(zc.reshape(B, T, -1), w_s[l], bs_t, vec(g_cv[l]), 1, GATE_ROWS, False)
        od, s1 = _hgrn(zd.reshape(B, T, -1), vec(lb_all[l]), vec(g_hgrn[l]), 0, zeros_c, CHUNK, pseq)
        xp = _post(l, xp, oa.reshape(B * T, -1), ob.reshape(B * T, -1), oc.reshape(B * T, -1), od.reshape(B * T, -1),
                   wout_b, vec(g_mlp[l]), wup_b, wdn_b, gf, 512, 1024, final)
        outs[0].append(za3[:, T - keep_p:, MIX_W:2 * MIX_W].reshape(B, keep_p, H, dh))
        outs[1].append(za3[:, T - keep_p:, 2 * MIX_W:].reshape(B, keep_p, H, dh))
        outs[4].append(c1)
        outs[5].append(n1.reshape(B, H, dh))
        outs[6].append(m1.reshape(B, H))
        outs[10].append(s1)

        za, zb, zg, zc, zd = _inproj(xs, vec(g_attn[l]), w_z, Bs * Ts)
        za3 = za.reshape(Bs, Ts, -1)
        new_t = lambda a: jnp.pad(jnp.transpose(a.reshape(Bs, Ts, H, dh), (0, 2, 3, 1)),
                                  ((0, 0), (0, 0), (0, 0), (0, LANES - Ts)))
        new_rows.append((new_t(za3[:, :, MIX_W:2 * MIX_W]), new_t(za3[:, :, 2 * MIX_W:])))
        oa, *windows = _attn_sample(l, _pad_rows(za3[:, :, :MIX_W], SAMPLE_ROWS), kt_all, vt_all,
                                    new_rows, tpast, tnew, Ts)
        seqs = lambda z: z.reshape(Bs, Ts, -1)
        ob, c2, n2, m2 = _mlstm(seqs(zb), seqs(zg), gate_bias, vec(g_mlstm[l]),
                                l, state_mlstm_C, n0_all, m0_all, Ts, sseq)
        oc, vrows = _gate(seqs(zc), w_s[l], bs_t, vec(g_cv[l]), sseq, Ts, True)
        od, s2 = _hgrn(seqs(zd), vec(lb_all[l]), vec(g_hgrn[l]), l, s0_all, Ts, sseq)
        flat = lambda o: o.reshape(Bs * Ts, MIX_W)
        xs = _post(l, xs, flat(oa[:, :Ts]), flat(ob), flat(oc), flat(od),
                   wout_b, vec(g_mlp[l]), wup_b, wdn_b, gf, Bs * Ts, 1024, final)
        outs[7].append(c2)
        outs[8].append(n2.reshape(Bs, H, dh))
        outs[9].append(m2.reshape(Bs, H))
        outs[11].append(s2)
        outs[12].append(vrows.reshape(Bs, Ts, H, dh))

    stacked = [jnp.stack(o) if o else None for o in outs]
    stacked[2], stacked[3] = (jnp.transpose(w, (0, 1, 4, 2, 3)) for w in windows)
    stacked[10], stacked[11] = (jnp.swapaxes(s, -1, -2) for s in stacked[10:12])
    return (xp.reshape(B, T, D_MODEL), xs.reshape(Bs, Ts, D_MODEL)) + tuple(stacked)
```

```python
import functools
import math

import numpy as np
import jax
import jax.numpy as jnp
from jax import lax
from jax.experimental import pallas as pl
from jax.experimental.pallas import tpu as pltpu

F32 = jnp.float32
BF16 = jnp.bfloat16

D_MODEL = 1024
N_HEADS = 4
HEAD_DIM = 64
MIX_W = N_HEADS * HEAD_DIM
DILATIONS = ((128, 1), (512, 4), (2048, 16))
MAX_WINDOW = 2048
N_BUCKETS = 32
D_FF = 4 * D_MODEL
EPS = 1e-6
NEG = -1e30
LB_FLOOR = 1e-30
CHUNK = 128
N_GATE_COLS = 2 * N_HEADS
GATE_W = 128
Z_WIDTHS = (3 * MIX_W, 4 * MIX_W, GATE_W, 2 * MIX_W, 4 * MIX_W)
Z_TOTAL = sum(Z_WIDTHS)
ATTN_SCALE = HEAD_DIM ** -0.5
N_LEVELS = 7
VMEM_LIMIT = 48 * 1024 * 1024
PROMPT_SEQS_PER_STEP = 2
SAMPLE_SEQS_PER_STEP = 4
GATE_ROWS = 4 * CHUNK
SEQS_PER_CHAIN = 2


def _dot(a, b):
    return jnp.dot(a, b, preferred_element_type=F32)


def _dot_nt(a, b):
    return lax.dot_general(a, b, (((1,), (1,)), ((), ())), preferred_element_type=F32)


def _dot_tn(a, b):
    return lax.dot_general(a, b, (((0,), (0,)), ((), ())), preferred_element_type=F32)


def _rms(x, g):
    return x * lax.rsqrt(jnp.mean(x * x, axis=-1, keepdims=True) + EPS) * g


def _log_sigmoid(x):
    return jnp.minimum(x, 0.0) - jnp.log1p(jnp.exp(-jnp.abs(x)))


def _dot_01(m01, f, pieces=3):
    out = None
    for _ in range(pieces):
        piece = f.astype(BF16)
        f = f - piece.astype(F32)
        out = _dot(m01, piece) if out is None else out + _dot(m01, piece)
    return out


def _head(h):
    return slice(h * HEAD_DIM, (h + 1) * HEAD_DIM)


def _head_scores(x, y):
    rows, width = x.shape
    tile = 2 * HEAD_DIM
    low = lax.broadcasted_iota(jnp.int32, (rows, tile), 1) < HEAD_DIM
    out = []
    for half in range(width // tile):
        lanes = slice(half * tile, (half + 1) * tile)
        xh = x[:, lanes].astype(BF16)
        zero = jnp.zeros_like(xh)
        x2 = jnp.concatenate([jnp.where(low, xh, zero), jnp.where(low, zero, xh)], axis=0)
        out.append(_dot_nt(x2, y[:, lanes].astype(BF16)))
    return jnp.concatenate(out, axis=0)


def _chunk_tile(ref, i, pad_ref):
    if pad_ref is None:
        return ref[i]
    pad_ref[0:ref.shape[1], :] = ref[i]
    return pad_ref[...]


def _zero_pads(pads):
    for p in pads:
        p[...] = jnp.zeros_like(p)


def _params(*sem):
    return pltpu.CompilerParams(dimension_semantics=sem, vmem_limit_bytes=VMEM_LIMIT)


def _inproj_body(x_ref, g_ref, w_ref, *out_refs):
    hn = _rms(x_ref[...], g_ref[...])
    h = hn.astype(BF16)
    off = 0
    for o_ref in out_refs:
        n = o_ref.shape[-1]
        if n == GATE_W:
            h_lo = (hn - h.astype(F32)).astype(BF16)
            z2 = _dot(h, w_ref[:, off:off + 2 * n])
            o_ref[...] = z2[:, :n] + z2[:, n:] + _dot(h_lo, w_ref[:, off:off + n])
            off += 2 * n
        else:
            o_ref[...] = _dot(h, w_ref[:, off:off + n])
            off += n


def _inproj(x2d, g, w, tm):
    n = x2d.shape[0]
    return pl.pallas_call(
        _inproj_body,
        grid=(n // tm,),
        in_specs=[pl.BlockSpec((tm, D_MODEL), lambda i: (i, 0)),
                  pl.BlockSpec((1, D_MODEL), lambda i: (0, 0)),
                  pl.BlockSpec((D_MODEL, Z_TOTAL + GATE_W), lambda i: (0, 0))],
        out_specs=[pl.BlockSpec((tm, wd), lambda i: (i, 0)) for wd in Z_WIDTHS],
        out_shape=[jax.ShapeDtypeStruct((n, wd), F32) for wd in Z_WIDTHS],
        compiler_params=_params("parallel"),
        name="inproj",
    )(x2d, g, w)


ATTN_ROWS = MAX_WINDOW
LANES = 128
UNITS_PER_ITER = 4
UNITS_PER_CHAIN = 2


def _attn_body(q0, q1, kp0, kp1, kc0, kc1, vp0, vp1, vc0, vc1, tab_ref, o_ref, os_ref, ls_ref):
    first = pl.program_id(1) == 0
    low_head = lax.broadcasted_iota(jnp.int32, (CHUNK, LANES), 1) < HEAD_DIM
    ones = jnp.ones((2 * CHUNK, LANES), BF16)
    q_refs, kp_refs, kc_refs, vp_refs, vc_refs = (q0, q1), (kp0, kp1), (kc0, kc1), (vp0, vp1), (vc0, vc1)

    def rows_of(start, d):
        if d == 1:
            return pl.ds(start if isinstance(start, int) else pl.multiple_of(start, CHUNK), CHUNK)
        return pl.ds(start, CHUNK, stride=d)

    def units(p, d, specs):
        loaded, scores = [], []
        for r, c, prev_in_block in specs:
            start = r + d * CHUNK * c
            cur = rows_of(start, d)
            prev = rows_of(start - d * CHUNK, d) if prev_in_block else rows_of(r + ATTN_ROWS - d * CHUNK, d)
            k_prev_ref, v_prev_ref = (kc_refs, vc_refs) if prev_in_block else (kp_refs, vp_refs)
            halves, vws = [], []
            for half in range(2):
                q = (q_refs[half][0, cur, :] * ATTN_SCALE).astype(BF16)
                kw = jnp.concatenate([k_prev_ref[half][0, prev, :], kc_refs[half][0, cur, :]], axis=0).astype(BF16)
                vws.append(jnp.concatenate([v_prev_ref[half][0, prev, :], vc_refs[half][0, cur, :]],
                                           axis=0).astype(BF16))
                zero = jnp.zeros_like(q)
                qm = jnp.concatenate([jnp.where(low_head, q, zero), jnp.where(low_head, zero, q)], axis=0)
                halves.append(_dot_nt(qm, kw))
            s = jnp.concatenate(halves, axis=0) + tab_ref[p]
            if not prev_in_block:
                s = jnp.concatenate([jnp.where(first, NEG, s[:, :CHUNK]), s[:, CHUNK:]], axis=1)
            loaded.append((cur, vws))
            scores.append(s)
        s = jnp.concatenate(scores, axis=0)
        m = jnp.max(jnp.maximum(s[:, :CHUNK], s[:, CHUNK:]), axis=1, keepdims=True)
        e = jnp.exp(s - m).astype(BF16)
        for u, (cur, vws) in enumerate(loaded):
            for half in range(2):
                base = (u * N_HEADS + 2 * half) * CHUNK
                rows = slice(base, base + 2 * CHUNK)
                od = _dot(e[rows], jnp.concatenate([vws[half], ones], axis=1))
                den = od[:, LANES:]
                o = od[:, :LANES] / den
                lse = m[rows] + jnp.log(den)
                os_ref[p, half, cur, :] = jnp.where(low_head, o[:CHUNK], o[CHUNK:])
                ls_ref[p, half, cur, :] = jnp.where(low_head, lse[:CHUNK], lse[CHUNK:])

    def run(p, d, specs):
        for j in range(0, len(specs), UNITS_PER_CHAIN):
            units(p, d, specs[j:j + UNITS_PER_CHAIN])

    for p, (_, d) in enumerate(DILATIONS):
        nblk = ATTN_ROWS // d // CHUNK
        if nblk >= UNITS_PER_ITER:
            def per_subsequence(r, carry, p=p, d=d, nblk=nblk):
                run(p, d, [(r, c, c > 0) for c in range(UNITS_PER_ITER)])

                def group(g, carry2):
                    run(p, d, [(r, g * UNITS_PER_ITER + u, True) for u in range(UNITS_PER_ITER)])
                    return carry2
                return lax.fori_loop(1, nblk // UNITS_PER_ITER, group, carry)
            if d == 1:
                per_subsequence(0, 0)
            else:
                lax.fori_loop(0, d, per_subsequence, 0)
        else:
            def group(g, carry, p=p, d=d):
                run(p, d, [(g * UNITS_PER_ITER + u, 0, False) for u in range(UNITS_PER_ITER)])
                return carry
            lax.fori_loop(0, d // UNITS_PER_ITER, group, 0)

    def merge(i, carry):
        rows = pl.ds(pl.multiple_of(i * CHUNK, CHUNK), CHUNK)
        for half in range(2):
            ls = [ls_ref[p, half, rows, :] for p in range(len(DILATIONS))]
            mx = jnp.maximum(jnp.maximum(ls[0], ls[1]), ls[2])
            es = [jnp.exp(l - mx) for l in ls]
            num = es[0] * os_ref[0, half, rows, :] + es[1] * os_ref[1, half, rows, :] + es[2] * os_ref[2, half, rows, :]
            o_ref[0, rows, half * LANES:(half + 1) * LANES] = num / (es[0] + es[1] + es[2])
        return carry
    lax.fori_loop(0, ATTN_ROWS // CHUNK, merge, 0)


def _attn_prompt(za, tabs):
    B, T, _ = za.shape
    blk = (1, ATTN_ROWS, LANES)
    cur = lambda j: pl.BlockSpec(blk, lambda b, i: (b, i, j))
    prv = lambda j: pl.BlockSpec(blk, lambda b, i: (b, jnp.maximum(i - 1, 0), j))
    npat = len(DILATIONS)
    return pl.pallas_call(
        _attn_body,
        grid=(B, T // ATTN_ROWS),
        in_specs=[cur(0), cur(1), prv(2), prv(3), cur(2), cur(3), prv(4), prv(5), cur(4), cur(5),
                  pl.BlockSpec((npat, N_HEADS * CHUNK, 2 * CHUNK), lambda b, i: (0, 0, 0))],
        out_specs=pl.BlockSpec((1, ATTN_ROWS, MIX_W), lambda b, i: (b, i, 0)),
        out_shape=jax.ShapeDtypeStruct((B, T, MIX_W), F32),
        scratch_shapes=[pltpu.VMEM((npat, 2, ATTN_ROWS, LANES), F32)] * 2,
        compiler_params=_params("parallel", "arbitrary"),
        name="attn_prompt",
    )(*([za] * 10), tabs.reshape(npat, N_HEADS * CHUNK, 2 * CHUNK))


SAMPLE_ROWS = 8


def _attn_sample_body(q_ref, tpast_ref, tnew_ref, *rest, n_new, n_earlier, write_windows):
    kt_ref, vt_ref, knt_ref, vnt_ref = rest[:4]
    earlier = [rest[4 + 4 * e:8 + 4 * e] for e in range(n_earlier)]
    outs = rest[4 + 4 * n_earlier:]
    o_ref = outs[0]
    npat = len(DILATIONS)
    q8 = q_ref[0] * ATTN_SCALE
    for h in range(N_HEADS):
        sl = _head(h)
        qh = q8[:, sl].astype(BF16)
        kt, vt = kt_ref[0, 0, h], vt_ref[0, 0, h]
        knt, vnt = knt_ref[0, h], vnt_ref[0, h]
        sp = _dot(qh, kt.astype(BF16))
        sn = _dot(qh, knt.astype(BF16))
        pps, pns, dens, lses = [], [], [], []
        for p in range(npat):
            lp = sp + tpast_ref[p, h]
            ln = sn + tnew_ref[p, h]
            m = jnp.maximum(jnp.max(lp, axis=1, keepdims=True), jnp.max(ln, axis=1, keepdims=True))
            pps.append(jnp.exp(lp - m))
            pns.append(jnp.exp(ln - m))
            dens.append(jnp.sum(pps[-1], axis=1, keepdims=True) + jnp.sum(pns[-1], axis=1, keepdims=True))
            lses.append(m + jnp.log(dens[-1]))
        pv = (_dot_nt(jnp.concatenate(pps, axis=0).astype(BF16), vt.astype(BF16))
              + _dot_nt(jnp.concatenate(pns, axis=0).astype(BF16), vnt.astype(BF16)))
        mx = jnp.maximum(jnp.maximum(lses[0], lses[1]), lses[2])
        es = [jnp.exp(l - mx) for l in lses]
        num = sum(es[p] * pv[p * SAMPLE_ROWS:(p + 1) * SAMPLE_ROWS] / dens[p] for p in range(npat))
        o_ref[0, :, sl] = num / (es[0] + es[1] + es[2])
        if write_windows:
            slabs = earlier + [(kt_ref, vt_ref, knt_ref, vnt_ref)]
            for layer, (kt_l, vt_l, knt_l, vnt_l) in enumerate(slabs):
                outs[1][layer, 0, h] = jnp.concatenate([kt_l[0, 0, h][:, n_new:], knt_l[0, h][:, :n_new]], axis=1)
                outs[2][layer, 0, h] = jnp.concatenate([vt_l[0, 0, h][:, n_new:], vnt_l[0, h][:, :n_new]], axis=1)


def _attn_sample(layer, q8, kt_all, vt_all, new_rows, tpast, tnew, n_new):
    depth, B, _, _, P = kt_all.shape
    npat = len(DILATIONS)
    write_windows = layer == depth - 1
    n_earlier = layer if write_windows else 0
    cache = lambda l: pl.BlockSpec((1, 1, N_HEADS, HEAD_DIM, P), lambda b: (l, b, 0, 0, 0))
    new = pl.BlockSpec((1, N_HEADS, HEAD_DIM, LANES), lambda b: (b, 0, 0, 0))
    rows = pl.BlockSpec((1, SAMPLE_ROWS, MIX_W), lambda b: (b, 0, 0))
    in_specs = [rows,
                pl.BlockSpec((npat, N_HEADS, SAMPLE_ROWS, P), lambda b: (0, 0, 0, 0)),
                pl.BlockSpec((npat, N_HEADS, SAMPLE_ROWS, LANES), lambda b: (0, 0, 0, 0))]
    args = [q8, tpast, tnew]
    for l in [layer] + list(range(n_earlier)):
        in_specs += [cache(l), cache(l), new, new]
        args += [kt_all, vt_all, *new_rows[l]]
    out_specs = [rows]
    out_shape = [jax.ShapeDtypeStruct((B, SAMPLE_ROWS, MIX_W), F32)]
    if write_windows:
        out_specs += [pl.BlockSpec((depth, 1, N_HEADS, HEAD_DIM, P), lambda b: (0, b, 0, 0, 0))] * 2
        out_shape += [jax.ShapeDtypeStruct(kt_all.shape, F32)] * 2
    return pl.pallas_call(
        functools.partial(_attn_sample_body, n_new=n_new, n_earlier=n_earlier, write_windows=write_windows),
        grid=(B,),
        in_specs=in_specs,
        out_specs=out_specs,
        out_shape=out_shape,
        compiler_params=_params("parallel"),
        name="attn_sample",
    )(*args)


def _mlstm_body(zq_ref, zk_ref, zv_ref, zo_ref, zg_ref, bias_ref, gn_ref, c0_ref, n0_ref, m0_ref,
                out_ref, c_out, n_out, m_out, c_s, n_s, m_s, *pads, n_valid):
    c = pl.program_id(1)
    nseq = zq_ref.shape[0]
    pq, pk, pv, po, pg = pads if pads else (None,) * 5
    _zero_pads(pads)

    @pl.when(c == 0)
    def _():
        c_s[...] = jnp.zeros_like(c_s)
        for i in range(nseq):
            for h in range(N_HEADS):
                c_s[i, _head(h), _head(h)] = c0_ref[i, h]
        n_s[...] = n0_ref[...]
        m_s[...] = m0_ref[...]

    rows4 = N_HEADS * CHUNK
    t4 = lax.broadcasted_iota(jnp.int32, (rows4, CHUNK), 0) & (CHUNK - 1)
    causal4 = t4 >= lax.broadcasted_iota(jnp.int32, (rows4, CHUNK), 1)
    valid4 = t4[:, 0:1] < n_valid
    tril = (lax.broadcasted_iota(jnp.int32, (CHUNK, CHUNK), 0)
            >= lax.broadcasted_iota(jnp.int32, (CHUNK, CHUNK), 1)).astype(BF16)
    lane_head = lax.broadcasted_iota(jnp.int32, (CHUNK, MIX_W), 1) // HEAD_DIM
    same_head = (lax.broadcasted_iota(jnp.int32, (MIX_W, MIX_W), 0) // HEAD_DIM
                 == lax.broadcasted_iota(jnp.int32, (MIX_W, MIX_W), 1) // HEAD_DIM)
    head_ones = same_head.astype(BF16)
    ones_l = jnp.ones((CHUNK, MIX_W), BF16)
    last = n_valid - 1

    def stack(f):
        return jnp.concatenate([f(h) for h in range(N_HEADS)], axis=0)

    def on_head_lanes(f, lanes):
        out = f(N_HEADS - 1)
        for h in range(N_HEADS - 2, -1, -1):
            out = jnp.where(lanes == h, f(h), out)
        return out

    for first_seq in range(0, nseq, SEQS_PER_CHAIN):
        group = list(range(first_seq, min(first_seq + SEQS_PER_CHAIN, nseq)))
        pre = []
        for i in group:
            gates = _chunk_tile(zg_ref, i, pg)
            ipre = gates + bias_ref[0:1, :]
            logf = _log_sigmoid(gates + bias_ref[1:2, :])
            bcum = _dot_01(tril, logf)
            bcum_t = bcum.T
            ipre_t = ipre.T
            q = _chunk_tile(zq_ref, i, pq)
            k = _chunk_tile(zk_ref, i, pk) * ATTN_SCALE
            v = _chunk_tile(zv_ref, i, pv)
            cbd = c_s[i]
            nrow = n_s[i]
            mprev = m_s[i]
            pre.append(dict(
                k=k, v=v, kb=k.astype(BF16), vb=v.astype(BF16), cbd=cbd, nrow=nrow, mprev=mprev,
                bcol=stack(lambda h: bcum[:, N_HEADS + h:N_HEADS + h + 1]),
                icol=stack(lambda h: ipre[:, h:h + 1]),
                brow=stack(lambda h: jnp.broadcast_to(bcum_t[N_HEADS + h:N_HEADS + h + 1, :], (CHUNK, CHUNK))),
                irow=stack(lambda h: jnp.broadcast_to(ipre_t[h:h + 1, :], (CHUNK, CHUNK))),
                mprev4=stack(lambda h: jnp.broadcast_to(mprev[:, h:h + 1], (CHUNK, 1))),
                scores=_head_scores(q, k),
                q_c=_dot_nt(q.astype(BF16), cbd.astype(BF16)),
                q_n=_dot((q * nrow).astype(BF16), head_ones)))

        cat = lambda key: jnp.concatenate([p[key] for p in pre], axis=0)
        bcol_g = cat("bcol")
        dmat = jnp.where(jnp.concatenate([causal4] * len(pre), axis=0), bcol_g - cat("brow") + cat("irow"), NEG)
        g = bcol_g + cat("mprev4")
        mt_g = jnp.maximum(g, jnp.max(dmat, axis=1, keepdims=True))
        sb_g = (cat("scores") * jnp.exp(dmat - mt_g)).astype(BF16)
        gexp_g = jnp.exp(g - mt_g)
        floor_g = jnp.exp(-mt_g)

        seq_rows = [slice(j * rows4, (j + 1) * rows4) for j in range(len(pre))]
        num_g = (jnp.concatenate([_dot(sb_g[r], p["vb"]) for r, p in zip(seq_rows, pre)], axis=0)
                 + gexp_g * jnp.concatenate([p["q_c"] for p in pre for _ in range(N_HEADS)], axis=0))
        nq_g = _dot(sb_g, ones_l) + gexp_g * jnp.concatenate([p["q_n"] for p in pre for _ in range(N_HEADS)], axis=0)
        hh4_g = num_g / jnp.maximum(jnp.abs(nq_g), floor_g)

        for j, (i, p) in enumerate(zip(group, pre)):
            rows = seq_rows[j]
            hh4, mt, bcol, icol, mprev = hh4_g[rows], mt_g[rows], p["bcol"], p["icol"], p["mprev"]
            hh = on_head_lanes(lambda h: hh4[h * CHUNK:(h + 1) * CHUNK], lane_head)
            sq = hh * hh
            sq_hi = sq.astype(BF16)
            sq_lo = (sq - sq_hi.astype(F32)).astype(BF16)
            ms = (_dot(sq_hi, head_ones) + _dot(sq_lo, head_ones)) * (1.0 / HEAD_DIM)
            out = jax.nn.sigmoid(_chunk_tile(zo_ref, i, po)) * (hh * lax.rsqrt(ms + EPS) * gn_ref[...])
            out_ref[i] = out[0:out_ref.shape[1]]

            mnew = [mt[h * CHUNK + last:h * CHUNK + last + 1, :] for h in range(N_HEADS)]
            blast = [bcol[h * CHUNK + last:h * CHUNK + last + 1, :] for h in range(N_HEADS)]
            wk = jnp.exp(stack(lambda h: jnp.broadcast_to(blast[h] - mnew[h], (CHUNK, 1))) - bcol + icol)
            if n_valid < CHUNK:
                wk = jnp.where(valid4, wk, 0.0)
            wk_w = on_head_lanes(lambda h: jnp.broadcast_to(wk[h * CHUNK:(h + 1) * CHUNK], (CHUNK, MIX_W)), lane_head)
            dc = on_head_lanes(
                lambda h: jnp.broadcast_to(jnp.exp(blast[h] + mprev[:, h:h + 1] - mnew[h]), (1, MIX_W)), lane_head[0:1])
            c_s[i] = dc * p["cbd"] + jnp.where(same_head, _dot_tn((p["v"] * wk_w).astype(BF16), p["kb"]), 0.0)
            n_s[i] = dc * p["nrow"] + jnp.sum(wk_w * p["k"], axis=0, keepdims=True)
            for h in range(N_HEADS):
                m_s[i, :, h:h + 1] = mnew[h]

    @pl.when(c == pl.num_programs(1) - 1)
    def _():
        for i in range(nseq):
            for h in range(N_HEADS):
                c_out[i, h] = c_s[i, _head(h), _head(h)]
        n_out[...] = n_s[...]
        m_out[...] = m_s[...]


def _mlstm(zb, zg, bias, gn, layer, c0, n0, m0, n_valid, nseq):
    B, T, _ = zb.shape
    rows = min(T, CHUNK)
    nc = T // rows
    blk = (nseq, rows, MIX_W)
    zspec = lambda j: pl.BlockSpec(blk, lambda b, c: (b, c, j))
    pads = [pltpu.VMEM((CHUNK, wd), F32) for wd in (MIX_W,) * 4 + (GATE_W,)] if rows < CHUNK else []
    st4 = pl.BlockSpec((nseq, N_HEADS, HEAD_DIM, HEAD_DIM), lambda b, c: (b, 0, 0, 0))
    st3 = pl.BlockSpec((nseq, 1, MIX_W), lambda b, c: (b, 0, 0))
    st2 = pl.BlockSpec((nseq, 1, N_HEADS), lambda b, c: (b, 0, 0))
    init = lambda spec: pl.BlockSpec((None,) + spec.block_shape,
                                     lambda b, c: (layer, b) + (0,) * (len(spec.block_shape) - 1))
    return pl.pallas_call(
        functools.partial(_mlstm_body, n_valid=n_valid),
        grid=(B // nseq, nc),
        in_specs=[zspec(0), zspec(1), zspec(2), zspec(3),
                  pl.BlockSpec((nseq, rows, GATE_W), lambda b, c: (b, c, 0)),
                  pl.BlockSpec((2, GATE_W), lambda b, c: (0, 0)),
                  pl.BlockSpec((1, MIX_W), lambda b, c: (0, 0)),
                  init(st4), init(st3), init(st2)],
        out_specs=[pl.BlockSpec(blk, lambda b, c: (b, c, 0)), st4, st3, st2],
        out_shape=[jax.ShapeDtypeStruct((B, T, MIX_W), F32),
                   jax.ShapeDtypeStruct((B, N_HEADS, HEAD_DIM, HEAD_DIM), F32),
                   jax.ShapeDtypeStruct((B, 1, MIX_W), F32),
                   jax.ShapeDtypeStruct((B, 1, N_HEADS), F32)],
        scratch_shapes=[pltpu.VMEM((nseq, MIX_W, MIX_W), F32),
                        pltpu.VMEM((nseq, 1, MIX_W), F32),
                        pltpu.VMEM((nseq, 1, N_HEADS), F32)] + pads,
        compiler_params=_params("parallel", "arbitrary"),
        name="mlstm",
    )(zb, zb, zb, zb, zg, bias, gn, c0, n0, m0)


def _gate_body(zc_ref, w_ref, bs_ref, gcv_ref, *rest, want_vrows):
    out_ref = rest[0]
    vrow_ref = rest[1] if want_vrows else None
    pad = rest[-1] if zc_ref.shape[1] < CHUNK else None
    if pad is not None:
        _zero_pads([pad])
    row = lax.broadcasted_iota(jnp.int32, (CHUNK, CHUNK), 0)
    col = lax.broadcasted_iota(jnp.int32, (CHUNK, CHUNK), 1)
    lane_head = lax.broadcasted_iota(jnp.int32, (CHUNK, MIX_W), 1) // HEAD_DIM
    ws = [jnp.where(row >= col, w_ref[h], 0.0).astype(BF16) for h in range(N_HEADS)]
    n_rows = zc_ref.shape[1]
    for i in range(zc_ref.shape[0]):
        z = zc_ref[i] if pad is None else _chunk_tile(zc_ref, i, pad)
        u = jax.nn.gelu(z[:, :MIX_W])
        vn = _rms(jax.nn.gelu(z[:, MIX_W:]), gcv_ref[...])
        if want_vrows:
            vrow_ref[i] = vn[0:n_rows]
        vb = vn.astype(BF16)
        mixed = []
        for j in range(z.shape[0] // CHUNK):
            vj = vb[j * CHUNK:(j + 1) * CHUNK]
            s = _dot(ws[N_HEADS - 1], vj) + bs_ref[:, N_HEADS - 1:N_HEADS]
            for h in range(N_HEADS - 2, -1, -1):
                s = jnp.where(lane_head == h, _dot(ws[h], vj) + bs_ref[:, h:h + 1], s)
            mixed.append(s)
        out_ref[i] = (u * jnp.concatenate(mixed, axis=0))[0:n_rows]


def _gate(zc, w_s, bs_t, gcv, nseq, rows, want_vrows):
    B, T, _ = zc.shape
    blk = (nseq, rows, MIX_W)
    n_out = 2 if want_vrows else 1
    return pl.pallas_call(
        functools.partial(_gate_body, want_vrows=want_vrows),
        grid=(B // nseq, T // rows),
        in_specs=[pl.BlockSpec((nseq, rows, 2 * MIX_W), lambda b, c: (b, c, 0)),
                  pl.BlockSpec((N_HEADS, CHUNK, CHUNK), lambda b, c: (0, 0, 0)),
                  pl.BlockSpec((CHUNK, N_HEADS), lambda b, c: (0, 0)),
                  pl.BlockSpec((1, MIX_W), lambda b, c: (0, 0))],
        out_specs=[pl.BlockSpec(blk, lambda b, c: (b, c, 0))] * n_out,
        out_shape=[jax.ShapeDtypeStruct((B, T, MIX_W), F32)] * n_out,
        scratch_shapes=[pltpu.VMEM((CHUNK, 2 * MIX_W), F32)] if rows < CHUNK else [],
        compiler_params=_params("parallel", "parallel"),
        name="gate",
    )(zc, w_s, bs_t, gcv)


def _hgrn_levels(n_valid):
    return [l for l in range(N_LEVELS) if (CHUNK >> (l + 1)) < max(n_valid, 2)]


def _hgrn_tables(n_valid):
    p = np.arange(CHUNK)[:, None]
    u = np.arange(CHUNK)[None, :]
    mats = []
    for l in _hgrn_levels(n_valid):
        m = CHUNK >> (l + 1)
        start = (p // m) * m
        odd = ((p // m) % 2) == 1
        mats.append(np.where(odd, (u >= start) & (u <= p), (u > p) & (u <= start + m - 1)))
    mats.append(u <= p)
    mats.append((u > p) & (u <= n_valid - 1))
    mall = np.concatenate(mats, axis=0).astype(np.float32)
    t = np.arange(CHUNK)[:, None]
    s = np.arange(CHUNK)[None, :]
    x = t ^ s
    top = np.floor(np.log2(np.maximum(x, 1))).astype(np.int32)
    lvl = np.where(s < t, N_LEVELS - 1 - top, np.where(s == t, N_LEVELS, N_LEVELS + 1)).astype(np.int32)
    return jnp.asarray(mall, BF16), jnp.asarray(np.tile(lvl, (N_HEADS, 1)))


def _hgrn_body(zq_ref, zf_ref, zi_ref, zg_ref, lb_ref, gn_ref, mall_ref, lvl_ref, s0_ref,
               out_ref, s_out, s_s, *pads, n_valid):
    c = pl.program_id(1)
    nseq = zq_ref.shape[0]
    pq, pf, pi, pg = pads if pads else (None,) * 4
    _zero_pads(pads)

    @pl.when(c == 0)
    def _():
        s_s[...] = jnp.zeros_like(s_s)
        for i in range(nseq):
            for h in range(N_HEADS):
                s_s[i, _head(h), _head(h)] = s0_ref[i, h]

    lb = lb_ref[...]
    lb_floor = jnp.maximum(lb, LB_FLOOR)
    lvl = lvl_ref[...]
    valid_rows = lax.broadcasted_iota(jnp.int32, (CHUNK, 1), 0) < n_valid
    lane_head = lax.broadcasted_iota(jnp.int32, (CHUNK, MIX_W), 1) // HEAD_DIM
    same_head = (lax.broadcasted_iota(jnp.int32, (MIX_W, MIX_W), 0) // HEAD_DIM
                 == lax.broadcasted_iota(jnp.int32, (MIX_W, MIX_W), 1) // HEAD_DIM)
    head_ones = same_head.astype(BF16)
    levels = _hgrn_levels(n_valid)
    at_level = [lvl == l for l in levels]
    on_diagonal = lvl == N_LEVELS

    group_of = lambda x: jnp.concatenate([x] * min(SEQS_PER_CHAIN, nseq), axis=0)
    at_level = [group_of(m) for m in at_level]
    on_diagonal = group_of(on_diagonal)
    amats = {}
    for first_seq in range(0, nseq, SEQS_PER_CHAIN):
        group = list(range(first_seq, min(first_seq + SEQS_PER_CHAIN, nseq)))
        pre = []
        for i in group:
            q = _chunk_tile(zq_ref, i, pq)
            fx = _chunk_tile(zf_ref, i, pf)
            sig = jax.nn.sigmoid(fx)
            logf = jnp.log(lb_floor + (1.0 - lb) * sig)
            kd = (1.0 - lb) * (1.0 - sig)
            pre.append((q, kd, _dot_01(mall_ref[...], logf, pieces=2)))
        amat = None
        for j, l in enumerate(levels):
            scores = []
            for q, kd, gsum in pre:
                e = jnp.exp(gsum[j * CHUNK:(j + 1) * CHUNK])
                scores.append(_head_scores(q * e, kd * e))
            amat = jnp.where(at_level[j], jnp.concatenate(scores, axis=0), 0.0 if j == 0 else amat)
        amat = jnp.where(on_diagonal, jnp.concatenate([_head_scores(q, kd) for q, kd, _ in pre], axis=0), amat)
        for j, (i, p) in enumerate(zip(group, pre)):
            amats[i] = (amat[j * N_HEADS * CHUNK:(j + 1) * N_HEADS * CHUNK],) + p

    outs, gates = [], []
    for i in range(nseq):
        amat, q, kd, gsum = amats[i]
        vb = _chunk_tile(zi_ref, i, pi).astype(BF16)
        bcum = gsum[len(levels) * CHUNK:(len(levels) + 1) * CHUNK]
        k_out = jnp.where(valid_rows, kd * jnp.exp(gsum[(len(levels) + 1) * CHUNK:]), 0.0).astype(BF16)
        e_last = jnp.exp(bcum[n_valid - 1:n_valid, :])
        sbd = s_s[i]
        o4 = _dot(amat.astype(BF16), vb)
        o = o4[(N_HEADS - 1) * CHUNK:]
        for h in range(N_HEADS - 2, -1, -1):
            o = jnp.where(lane_head == h, o4[h * CHUNK:(h + 1) * CHUNK], o)
        outs.append(o + _dot_nt((q * jnp.exp(bcum)).astype(BF16), sbd.astype(BF16)))
        gates.append(_chunk_tile(zg_ref, i, pg))
        s_s[i] = e_last * sbd + jnp.where(same_head, _dot_tn(vb, k_out), 0.0)

    o = jnp.concatenate(outs, axis=0)
    gate = jnp.concatenate(gates, axis=0)
    sq = o * o
    sq_hi = sq.astype(BF16)
    sq_lo = (sq - sq_hi.astype(F32)).astype(BF16)
    ms = (_dot(sq_hi, head_ones) + _dot(sq_lo, head_ones)) * (1.0 / HEAD_DIM)
    out = o * lax.rsqrt(ms + EPS) * gn_ref[...] * (gate * jax.nn.sigmoid(gate))
    for i in range(nseq):
        out_ref[i] = out[i * CHUNK:i * CHUNK + out_ref.shape[1]]

    @pl.when(c == pl.num_programs(1) - 1)
    def _():
        for i in range(nseq):
            for h in range(N_HEADS):
                s_out[i, h] = s_s[i, _head(h), _head(h)]


def _hgrn(zd, lb, gn, layer, s0_t, n_valid, nseq):
    B, T, _ = zd.shape
    mall, lvl = _hgrn_tables(n_valid)
    rows = min(T, CHUNK)
    blk = (nseq, rows, MIX_W)
    zspec = lambda j: pl.BlockSpec(blk, lambda b, c: (b, c, j))
    pads = [pltpu.VMEM((CHUNK, MIX_W), F32)] * 4 if rows < CHUNK else []
    st4 = pl.BlockSpec((nseq, N_HEADS, HEAD_DIM, HEAD_DIM), lambda b, c: (b, 0, 0, 0))
    vec = pl.BlockSpec((1, MIX_W), lambda b, c: (0, 0))
    return pl.pallas_call(
        functools.partial(_hgrn_body, n_valid=n_valid),
        grid=(B // nseq, T // rows),
        in_specs=[zspec(0), zspec(1), zspec(2), zspec(3), vec, vec,
                  pl.BlockSpec(mall.shape, lambda b, c: (0, 0)),
                  pl.BlockSpec(lvl.shape, lambda b, c: (0, 0)),
                  pl.BlockSpec((None,) + st4.block_shape, lambda b, c: (layer, b, 0, 0, 0))],
        out_specs=[pl.BlockSpec(blk, lambda b, c: (b, c, 0)), st4],
        out_shape=[jax.ShapeDtypeStruct((B, T, MIX_W), F32),
                   jax.ShapeDtypeStruct((B, N_HEADS, HEAD_DIM, HEAD_DIM), F32)],
        scratch_shapes=[pltpu.VMEM((nseq, MIX_W, MIX_W), F32)] + pads,
        compiler_params=_params("parallel", "arbitrary"),
        name="hgrn",
    )(zd, zd, zd, zd, lb, gn, mall, lvl, s0_t)


def _post_body(x_ref, oa_ref, ob_ref, oc_ref, od_ref, wout_hbm, gm_ref, wup_hbm, wdn_hbm, gf_ref, y_ref,
               wout_ref, wup_ref, wdn_ref, h_s, acc_s, *, layer, tf, final):
    @pl.when(pl.program_id(0) == 0)
    def _():
        pltpu.sync_copy(wout_hbm.at[layer], wout_ref)
        pltpu.sync_copy(wup_hbm.at[layer], wup_ref)
        pltpu.sync_copy(wdn_hbm.at[layer], wdn_ref)

    mix = jnp.concatenate([o_ref[...].astype(BF16) for o_ref in (oa_ref, ob_ref, oc_ref, od_ref)], axis=1)
    x1 = x_ref[...] + _dot(mix, wout_ref[...])
    acc_s[...] = x1
    h_s[...] = _rms(x1, gm_ref[...]).astype(BF16)
    for j in range(D_FF // tf):
        up = jnp.maximum(_dot(h_s[...], wup_ref[:, j * tf:(j + 1) * tf]), 0.0)
        acc_s[...] += _dot((up * up).astype(BF16), wdn_ref[j * tf:(j + 1) * tf, :])
    y_ref[...] = _rms(acc_s[...], gf_ref[...]) if final else acc_s[...]


def _post(layer, x2d, oa, ob, oc, od, wout, gm, wup, wdn, gf, tm, tf, final):
    n = x2d.shape[0]
    row = lambda wd: pl.BlockSpec((tm, wd), lambda i: (i, 0))
    vec = pl.BlockSpec((1, D_MODEL), lambda i: (0, 0))
    hbm = pl.BlockSpec(memory_space=pl.ANY)
    return pl.pallas_call(
        functools.partial(_post_body, layer=layer, tf=tf, final=final),
        grid=(n // tm,),
        in_specs=[row(D_MODEL)] + [row(MIX_W)] * 4 + [hbm, vec, hbm, hbm, vec],
        out_specs=row(D_MODEL),
        out_shape=jax.ShapeDtypeStruct((n, D_MODEL), F32),
        scratch_shapes=[pltpu.VMEM(wout.shape[1:], BF16), pltpu.VMEM(wup.shape[1:], BF16), pltpu.VMEM(wdn.shape[1:], BF16),
                        pltpu.VMEM((tm, D_MODEL), BF16), pltpu.VMEM((tm, D_MODEL), F32)],
        compiler_params=_params("arbitrary"),
        name="post",
    )(x2d, oa, ob, oc, od, wout, gm, wup, wdn, gf)


def _rel_bucket(dist):
    max_exact = N_BUCKETS // 2
    d = jnp.maximum(dist, 1).astype(F32)
    large = max_exact + (jnp.log(d / max_exact) / math.log(MAX_WINDOW / max_exact)
                         * (N_BUCKETS - max_exact)).astype(jnp.int32)
    large = jnp.clip(large, max_exact, N_BUCKETS - 1)
    return jnp.where(dist < max_exact, dist, large)


def _pattern_bias(rel_bias, w, d):
    offs = jnp.arange(w // d + 1, dtype=jnp.int32) * d
    return rel_bias[_rel_bucket(offs)].T.astype(F32)


def _prompt_table(bias):
    cols = 2 * CHUNK
    u = jnp.concatenate([bias[:, ::-1], jnp.full((N_HEADS, cols - CHUNK), NEG, F32)], axis=1)
    return jnp.tile(u, (1, CHUNK))[:, :CHUNK * cols].reshape(N_HEADS, CHUNK, cols)


def _sample_tables(bias, d, n_new, past):
    comb = jnp.concatenate([bias[:, :, None], jnp.full((N_HEADS, CHUNK + 1, d - 1), NEG, F32)], axis=2)
    comb = comb.reshape(N_HEADS, (CHUNK + 1) * d)
    length = past + SAMPLE_ROWS + 1
    comb = comb[:, :length]
    comb = jnp.pad(comb, ((0, 0), (0, length - comb.shape[1])), constant_values=NEG)
    rev = comb[:, ::-1]
    tpast, tnew = [], []
    for row in range(SAMPLE_ROWS):
        t = min(row, n_new - 1)
        start = length - 1 - past - t
        tpast.append(rev[:, start:start + past])
        start = length - 1 - t
        tnew.append(jnp.pad(rev[:, start:start + t + 1], ((0, 0), (0, CHUNK - t - 1)), constant_values=NEG))
    return jnp.stack(tpast, axis=1), jnp.stack(tnew, axis=1)


def _pad_rows(a, rows):
    return jnp.pad(a, ((0, 0), (0, rows - a.shape[1]), (0, 0)))


def kernel(x_prompt, x_sample, cache_k_win, cache_v_win, state_mlstm_C, state_mlstm_n, state_mlstm_m, state_hgrn_S, rel_bias, w_in, w_out, g_attn, g_mlp, w_up, w_down, b_i, b_f, g_mlstm, g_cv, w_s, b_s, hgrn_lb, g_hgrn, g_final):
    depth = w_in.shape[0]
    B, T, _ = x_prompt.shape
    Bs, Ts, _ = x_sample.shape
    past = cache_k_win.shape[2]
    assert past == MAX_WINDOW and T % (MAX_WINDOW) == 0 and Ts <= SAMPLE_ROWS
    H, dh = N_HEADS, HEAD_DIM
    keep_p = min(MAX_WINDOW, T)

    sm = jax.nn.softmax(hgrn_lb.astype(F32), axis=0)
    lb_all = jnp.cumsum(sm, axis=0) - sm[0:1]
    biases = [_pattern_bias(rel_bias, w, d) for w, d in DILATIONS]
    tabs_p = jnp.stack([_prompt_table(bb) for bb in biases])
    tabs_s = [_sample_tables(bb, d, Ts, past) for bb, (_, d) in zip(biases, DILATIONS)]
    tpast = jnp.stack([a for a, _ in tabs_s])
    tnew = jnp.stack([b for _, b in tabs_s])
    kt_all = jnp.transpose(cache_k_win, (0, 1, 3, 4, 2))
    vt_all = jnp.transpose(cache_v_win, (0, 1, 3, 4, 2))

    a_end = Z_WIDTHS[0] + Z_WIDTHS[1]
    g_end = a_end + N_GATE_COLS
    zeros_c = jnp.zeros((1, B, H, dh, dh), F32)
    zeros_n = jnp.zeros((1, B, 1, MIX_W), F32)
    zeros_m = jnp.zeros((1, B, 1, H), F32)
    n0_all = state_mlstm_n.reshape(depth, Bs, 1, MIX_W)
    m0_all = state_mlstm_m.reshape(depth, Bs, 1, H)
    s0_all = jnp.swapaxes(state_hgrn_S, -1, -2)
    vec = lambda a: a.reshape(1, -1).astype(F32)
    pseq = math.gcd(B, PROMPT_SEQS_PER_STEP)
    sseq = math.gcd(Bs, SAMPLE_SEQS_PER_STEP)

    wout_b, wup_b, wdn_b = w_out.astype(BF16), w_up.astype(BF16), w_down.astype(BF16)
    xp = x_prompt.reshape(B * T, D_MODEL)
    xs = x_sample.reshape(Bs * Ts, D_MODEL)
    outs = [[] for _ in range(13)]
    new_rows = []
    for l in range(depth):
        wl = w_in[l]
        w_gate = jnp.pad(wl[:, a_end:g_end], ((0, 0), (0, GATE_W - N_GATE_COLS)))
        w_gate_lo = w_gate - w_gate.astype(BF16).astype(F32)
        w_z = jnp.concatenate([wl[:, :a_end], w_gate, w_gate_lo, wl[:, g_end:]], axis=1).astype(BF16)
        gate_bias = jnp.zeros((2, GATE_W), F32).at[0, :H].set(b_i[l]).at[1, H:2 * H].set(b_f[l])
        bs_t = b_s[l].T.astype(F32)
        final = l == depth - 1
        gf = vec(g_final)

        za, zb, zg, zc, zd = _inproj(xp, vec(g_attn[l]), w_z, 512)
        za3 = za.reshape(B, T, -1)
        oa = _attn_prompt(za3, tabs_p)
        ob, c1, n1, m1 = _mlstm(zb.reshape(B, T, -1), zg.reshape(B, T, -1), gate_bias, vec(g_mlstm[l]),
                                0, zeros_c, zeros_n, zeros_m, CHUNK, pseq)
        (oc,) = _gate(zc.reshape(B, T, -1), w_s[l], bs_t, vec(g_cv[l]), 1, GATE_ROWS, False)
        od, s1 = _hgrn(zd.reshape(B, T, -1), vec(lb_all[l]), vec(g_hgrn[l]), 0, zeros_c, CHUNK, pseq)
        xp = _post(l, xp, oa.reshape(B * T, -1), ob.reshape(B * T, -1), oc.reshape(B * T, -1), od.reshape(B * T, -1),
                   wout_b, vec(g_mlp[l]), wup_b, wdn_b, gf, 512, 1024, final)
        outs[0].append(za3[:, T - keep_p:, MIX_W:2 * MIX_W].reshape(B, keep_p, H, dh))
        outs[1].append(za3[:, T - keep_p:, 2 * MIX_W:].reshape(B, keep_p, H, dh))
        outs[4].append(c1)
        outs[5].append(n1.reshape(B, H, dh))
        outs[6].append(m1.reshape(B, H))
        outs[10].append(s1)

        za, zb, zg, zc, zd = _inproj(xs, vec(g_attn[l]), w_z, Bs * Ts)
        za3 = za.reshape(Bs, Ts, -1)
        new_t = lambda a: jnp.pad(jnp.transpose(a.reshape(Bs, Ts, H, dh), (0, 2, 3, 1)),
                                  ((0, 0), (0, 0), (0, 0), (0, LANES - Ts)))
        new_rows.append((new_t(za3[:, :, MIX_W:2 * MIX_W]), new_t(za3[:, :, 2 * MIX_W:])))
        oa, *windows = _attn_sample(l, _pad_rows(za3[:, :, :MIX_W], SAMPLE_ROWS), kt_all, vt_all,
                                    new_rows, tpast, tnew, Ts)
        seqs = lambda z: z.reshape(Bs, Ts, -1)
        ob, c2, n2, m2 = _mlstm(seqs(zb), seqs(zg), gate_bias, vec(g_mlstm[l]),
                                l, state_mlstm_C, n0_all, m0_all, Ts, sseq)
        oc, vrows = _gate(seqs(zc), w_s[l], bs_t, vec(g_cv[l]), sseq, Ts, True)
        od, s2 = _hgrn(seqs(zd), vec(lb_all[l]), vec(g_hgrn[l]), l, s0_all, Ts, sseq)
        flat = lambda o: o.reshape(Bs * Ts, MIX_W)
        xs = _post(l, xs, flat(oa[:, :Ts]), flat(ob), flat(oc), flat(od),
                   wout_b, vec(g_mlp[l]), wup_b, wdn_b, gf, Bs * Ts, 1024, final)
        outs[7].append(c2)
        outs[8].append(n2.reshape(Bs, H, dh))
        outs[9].append(m2.reshape(Bs, H))
        outs[11].append(s2)
        outs[12].append(vrows.reshape(Bs, Ts, H, dh))

    stacked = [jnp.stack(o) if o else None for o in outs]
    stacked[2], stacked[3] = (jnp.transpose(w, (0, 1, 4, 2, 3)) for w in windows)
    stacked[10], stacked[11] = (jnp.swapaxes(s, -1, -2) for s in stacked[10:12])
    return (xp.reshape(B, T, D_MODEL), xs.reshape(Bs, Ts, D_MODEL)) + tuple(stacked)
```

```python
import functools
import math

import numpy as np
import jax
import jax.numpy as jnp
from jax import lax
from jax.experimental import pallas as pl
from jax.experimental.pallas import tpu as pltpu

F32 = jnp.float32
BF16 = jnp.bfloat16

D_MODEL = 1024
N_HEADS = 4
HEAD_DIM = 64
MIX_W = N_HEADS * HEAD_DIM
DILATIONS = ((128, 1), (512, 4), (2048, 16))
MAX_WINDOW = 2048
N_BUCKETS = 32
D_FF = 4 * D_MODEL
EPS = 1e-6
NEG = -1e30
LB_FLOOR = 1e-30
CHUNK = 128
N_GATE_COLS = 2 * N_HEADS
GATE_W = 128
Z_WIDTHS = (3 * MIX_W, 4 * MIX_W, GATE_W, 2 * MIX_W, 4 * MIX_W)
Z_TOTAL = sum(Z_WIDTHS)
ATTN_SCALE = HEAD_DIM ** -0.5
N_LEVELS = 7
VMEM_LIMIT = 48 * 1024 * 1024
PROMPT_SEQS_PER_STEP = 2
SAMPLE_SEQS_PER_STEP = 4
GATE_ROWS = 4 * CHUNK
SEQS_PER_CHAIN = 2


def _dot(a, b):
    return jnp.dot(a, b, preferred_element_type=F32)


def _dot_nt(a, b):
    return lax.dot_general(a, b, (((1,), (1,)), ((), ())), preferred_element_type=F32)


def _dot_tn(a, b):
    return lax.dot_general(a, b, (((0,), (0,)), ((), ())), preferred_element_type=F32)


def _rms(x, g):
    return x * lax.rsqrt(jnp.mean(x * x, axis=-1, keepdims=True) + EPS) * g


def _log_sigmoid(x):
    return jnp.minimum(x, 0.0) - jnp.log1p(jnp.exp(-jnp.abs(x)))


def _dot_01(m01, f, pieces=3):
    out = None
    for _ in range(pieces):
        piece = f.astype(BF16)
        f = f - piece.astype(F32)
        out = _dot(m01, piece) if out is None else out + _dot(m01, piece)
    return out


def _head(h):
    return slice(h * HEAD_DIM, (h + 1) * HEAD_DIM)


def _head_scores(x, y):
    rows, width = x.shape
    tile = 2 * HEAD_DIM
    low = lax.broadcasted_iota(jnp.int32, (rows, tile), 1) < HEAD_DIM
    out = []
    for half in range(width // tile):
        lanes = slice(half * tile, (half + 1) * tile)
        xh = x[:, lanes].astype(BF16)
        zero = jnp.zeros_like(xh)
        x2 = jnp.concatenate([jnp.where(low, xh, zero), jnp.where(low, zero, xh)], axis=0)
        out.append(_dot_nt(x2, y[:, lanes].astype(BF16)))
    return jnp.concatenate(out, axis=0)


def _chunk_tile(ref, i, pad_ref):
    if pad_ref is None:
        return ref[i]
    pad_ref[0:ref.shape[1], :] = ref[i]
    return pad_ref[...]


def _zero_pads(pads):
    for p in pads:
        p[...] = jnp.zeros_like(p)


def _params(*sem):
    return pltpu.CompilerParams(dimension_semantics=sem, vmem_limit_bytes=VMEM_LIMIT)


def _inproj_body(x_ref, g_ref, w_ref, *out_refs):
    hn = _rms(x_ref[...], g_ref[...])
    h = hn.astype(BF16)
    off = 0
    for o_ref in out_refs:
        n = o_ref.shape[-1]
        if n == GATE_W:
            h_lo = (hn - h.astype(F32)).astype(BF16)
            z2 = _dot(h, w_ref[:, off:off + 2 * n])
            o_ref[...] = z2[:, :n] + z2[:, n:] + _dot(h_lo, w_ref[:, off:off + n])
            off += 2 * n
        else:
            o_ref[...] = _dot(h, w_ref[:, off:off + n])
            off += n


def _inproj(x2d, g, w, tm):
    n = x2d.shape[0]
    return pl.pallas_call(
        _inproj_body,
        grid=(n // tm,),
        in_specs=[pl.BlockSpec((tm, D_MODEL), lambda i: (i, 0)),
                  pl.BlockSpec((1, D_MODEL), lambda i: (0, 0)),
                  pl.BlockSpec((D_MODEL, Z_TOTAL + GATE_W), lambda i: (0, 0))],
        out_specs=[pl.BlockSpec((tm, wd), lambda i: (i, 0)) for wd in Z_WIDTHS],
        out_shape=[jax.ShapeDtypeStruct((n, wd), F32) for wd in Z_WIDTHS],
        compiler_params=_params("parallel"),
        name="inproj",
    )(x2d, g, w)


ATTN_ROWS = MAX_WINDOW
LANES = 128
UNITS_PER_ITER = 4
UNITS_PER_CHAIN = 2


def _attn_body(q0, q1, kp0, kp1, kc0, kc1, vp0, vp1, vc0, vc1, tab_ref, o_ref, os_ref, ls_ref):
    first = pl.program_id(1) == 0
    low_head = lax.broadcasted_iota(jnp.int32, (CHUNK, LANES), 1) < HEAD_DIM
    ones = jnp.ones((2 * CHUNK, LANES), BF16)
    q_refs, kp_refs, kc_refs, vp_refs, vc_refs = (q0, q1), (kp0, kp1), (kc0, kc1), (vp0, vp1), (vc0, vc1)

    def rows_of(start, d):
        if d == 1:
            return pl.ds(start if isinstance(start, int) else pl.multiple_of(start, CHUNK), CHUNK)
        return pl.ds(start, CHUNK, stride=d)

    def units(p, d, specs):
        loaded, scores = [], []
        for r, c, prev_in_block in specs:
            start = r + d * CHUNK * c
            cur = rows_of(start, d)
            prev = rows_of(start - d * CHUNK, d) if prev_in_block else rows_of(r + ATTN_ROWS - d * CHUNK, d)
            k_prev_ref, v_prev_ref = (kc_refs, vc_refs) if prev_in_block else (kp_refs, vp_refs)
            halves, vws = [], []
            for half in range(2):
                q = (q_refs[half][0, cur, :] * ATTN_SCALE).astype(BF16)
                kw = jnp.concatenate([k_prev_ref[half][0, prev, :], kc_refs[half][0, cur, :]], axis=0).astype(BF16)
                vws.append(jnp.concatenate([v_prev_ref[half][0, prev, :], vc_refs[half][0, cur, :]],
                                           axis=0).astype(BF16))
                zero = jnp.zeros_like(q)
                qm = jnp.concatenate([jnp.where(low_head, q, zero), jnp.where(low_head, zero, q)], axis=0)
                halves.append(_dot_nt(qm, kw))
            s = jnp.concatenate(halves, axis=0) + tab_ref[p]
            if not prev_in_block:
                s = jnp.concatenate([jnp.where(first, NEG, s[:, :CHUNK]), s[:, CHUNK:]], axis=1)
            loaded.append((cur, vws))
            scores.append(s)
        s = jnp.concatenate(scores, axis=0)
        m = jnp.max(jnp.maximum(s[:, :CHUNK], s[:, CHUNK:]), axis=1, keepdims=True)
        e = jnp.exp(s - m).astype(BF16)
        for u, (cur, vws) in enumerate(loaded):
            for half in range(2):
                base = (u * N_HEADS + 2 * half) * CHUNK
                rows = slice(base, base + 2 * CHUNK)
                od = _dot(e[rows], jnp.concatenate([vws[half], ones], axis=1))
                den = od[:, LANES:]
                o = od[:, :LANES] / den
                lse = m[rows] + jnp.log(den)
                os_ref[p, half, cur, :] = jnp.where(low_head, o[:CHUNK], o[CHUNK:])
                ls_ref[p, half, cur, :] = jnp.where(low_head, lse[:CHUNK], lse[CHUNK:])

    def run(p, d, specs):
        for j in range(0, len(specs), UNITS_PER_CHAIN):
            units(p, d, specs[j:j + UNITS_PER_CHAIN])

    for p, (_, d) in enumerate(DILATIONS):
        nblk = ATTN_ROWS // d // CHUNK
        if nblk >= UNITS_PER_ITER:
            def per_subsequence(r, carry, p=p, d=d, nblk=nblk):
                run(p, d, [(r, c, c > 0) for c in range(UNITS_PER_ITER)])

                def group(g, carry2):
                    run(p, d, [(r, g * UNITS_PER_ITER + u, True) for u in range(UNITS_PER_ITER)])
                    return carry2
                return lax.fori_loop(1, nblk // UNITS_PER_ITER, group, carry)
            if d == 1:
                per_subsequence(0, 0)
            else:
                lax.fori_loop(0, d, per_subsequence, 0)
        else:
            def group(g, carry, p=p, d=d):
                run(p, d, [(g * UNITS_PER_ITER + u, 0, False) for u in range(UNITS_PER_ITER)])
                return carry
            lax.fori_loop(0, d // UNITS_PER_ITER, group, 0)

    def merge(i, carry):
        rows = pl.ds(pl.multiple_of(i * CHUNK, CHUNK), CHUNK)
        for half in range(2):
            ls = [ls_ref[p, half, rows, :] for p in range(len(DILATIONS))]
            mx = jnp.maximum(jnp.maximum(ls[0], ls[1]), ls[2])
            es = [jnp.exp(l - mx) for l in ls]
            num = es[0] * os_ref[0, half, rows, :] + es[1] * os_ref[1, half, rows, :] + es[2] * os_ref[2, half, rows, :]
            o_ref[0, rows, half * LANES:(half + 1) * LANES] = num / (es[0] + es[1] + es[2])
        return carry
    lax.fori_loop(0, ATTN_ROWS // CHUNK, merge, 0)


def _attn_prompt(za, tabs):
    B, T, _ = za.shape
    blk = (1, ATTN_ROWS, LANES)
    cur = lambda j: pl.BlockSpec(blk, lambda b, i: (b, i, j))
    prv = lambda j: pl.BlockSpec(blk, lambda b, i: (b, jnp.maximum(i - 1, 0), j))
    npat = len(DILATIONS)
    return pl.pallas_call(
        _attn_body,
        grid=(B, T // ATTN_ROWS),
        in_specs=[cur(0), cur(1), prv(2), prv(3), cur(2), cur(3), prv(4), prv(5), cur(4), cur(5),
                  pl.BlockSpec((npat, N_HEADS * CHUNK, 2 * CHUNK), lambda b, i: (0, 0, 0))],
        out_specs=pl.BlockSpec((1, ATTN_ROWS, MIX_W), lambda b, i: (b, i, 0)),
        out_shape=jax.ShapeDtypeStruct((B, T, MIX_W), F32),
        scratch_shapes=[pltpu.VMEM((npat, 2, ATTN_ROWS, LANES), F32)] * 2,
        compiler_params=_params("parallel", "arbitrary"),
        name="attn_prompt",
    )(*([za] * 10), tabs.reshape(npat, N_HEADS * CHUNK, 2 * CHUNK))


SAMPLE_ROWS = 8


def _attn_sample_body(q_ref, tpast_ref, tnew_ref, *rest, n_new, n_earlier, write_windows):
    kt_ref, vt_ref, knt_ref, vnt_ref = rest[:4]
    earlier = [rest[4 + 4 * e:8 + 4 * e] for e in range(n_earlier)]
    outs = rest[4 + 4 * n_earlier:]
    o_ref = outs[0]
    npat = len(DILATIONS)
    q8 = q_ref[0] * ATTN_SCALE
    for h in range(N_HEADS):
        sl = _head(h)
        qh = q8[:, sl].astype(BF16)
        kt, vt = kt_ref[0, 0, h], vt_ref[0, 0, h]
        knt, vnt = knt_ref[0, h], vnt_ref[0, h]
        sp = _dot(qh, kt.astype(BF16))
        sn = _dot(qh, knt.astype(BF16))
        pps, pns, dens, lses = [], [], [], []
        for p in range(npat):
            lp = sp + tpast_ref[p, h]
            ln = sn + tnew_ref[p, h]
            m = jnp.maximum(jnp.max(lp, axis=1, keepdims=True), jnp.max(ln, axis=1, keepdims=True))
            pps.append(jnp.exp(lp - m))
            pns.append(jnp.exp(ln - m))
            dens.append(jnp.sum(pps[-1], axis=1, keepdims=True) + jnp.sum(pns[-1], axis=1, keepdims=True))
            lses.append(m + jnp.log(dens[-1]))
        pv = (_dot_nt(jnp.concatenate(pps, axis=0).astype(BF16), vt.astype(BF16))
              + _dot_nt(jnp.concatenate(pns, axis=0).astype(BF16), vnt.astype(BF16)))
        mx = jnp.maximum(jnp.maximum(lses[0], lses[1]), lses[2])
        es = [jnp.exp(l - mx) for l in lses]
        num = sum(es[p] * pv[p * SAMPLE_ROWS:(p + 1) * SAMPLE_ROWS] / dens[p] for p in range(npat))
        o_ref[0, :, sl] = num / (es[0] + es[1] + es[2])
        if write_windows:
            slabs = earlier + [(kt_ref, vt_ref, knt_ref, vnt_ref)]
            for layer, (kt_l, vt_l, knt_l, vnt_l) in enumerate(slabs):
                outs[1][layer, 0, h] = jnp.concatenate([kt_l[0, 0, h][:, n_new:], knt_l[0, h][:, :n_new]], axis=1)
                outs[2][layer, 0, h] = jnp.concatenate([vt_l[0, 0, h][:, n_new:], vnt_l[0, h][:, :n_new]], axis=1)


def _attn_sample(layer, q8, kt_all, vt_all, new_rows, tpast, tnew, n_new):
    depth, B, _, _, P = kt_all.shape
    npat = len(DILATIONS)
    write_windows = layer == depth - 1
    n_earlier = layer if write_windows else 0
    cache = lambda l: pl.BlockSpec((1, 1, N_HEADS, HEAD_DIM, P), lambda b: (l, b, 0, 0, 0))
    new = pl.BlockSpec((1, N_HEADS, HEAD_DIM, LANES), lambda b: (b, 0, 0, 0))
    rows = pl.BlockSpec((1, SAMPLE_ROWS, MIX_W), lambda b: (b, 0, 0))
    in_specs = [rows,
                pl.BlockSpec((npat, N_HEADS, SAMPLE_ROWS, P), lambda b: (0, 0, 0, 0)),
                pl.BlockSpec((npat, N_HEADS, SAMPLE_ROWS, LANES), lambda b: (0, 0, 0, 0))]
    args = [q8, tpast, tnew]
    for l in [layer] + list(range(n_earlier)):
        in_specs += [cache(l), cache(l), new, new]
        args += [kt_all, vt_all, *new_rows[l]]
    out_specs = [rows]
    out_shape = [jax.ShapeDtypeStruct((B, SAMPLE_ROWS, MIX_W), F32)]
    if write_windows:
        out_specs += [pl.BlockSpec((depth, 1, N_HEADS, HEAD_DIM, P), lambda b: (0, b, 0, 0, 0))] * 2
        out_shape += [jax.ShapeDtypeStruct(kt_all.shape, F32)] * 2
    return pl.pallas_call(
        functools.partial(_attn_sample_body, n_new=n_new, n_earlier=n_earlier, write_windows=write_windows),
        grid=(B,),
        in_specs=in_specs,
        out_specs=out_specs,
        out_shape=out_shape,
        compiler_params=_params("parallel"),
        name="attn_sample",
    )(*args)


def _mlstm_body(zq_ref, zk_ref, zv_ref, zo_ref, zg_ref, bias_ref, gn_ref, c0_ref, n0_ref, m0_ref,
                out_ref, c_out, n_out, m_out, c_s, n_s, m_s, *pads, n_valid):
    c = pl.program_id(1)
    nseq = zq_ref.shape[0]
    pq, pk, pv, po, pg = pads if pads else (None,) * 5
    _zero_pads(pads)

    @pl.when(c == 0)
    def _():
        c_s[...] = jnp.zeros_like(c_s)
        for i in range(nseq):
            for h in range(N_HEADS):
                c_s[i, _head(h), _head(h)] = c0_ref[i, h]
        n_s[...] = n0_ref[...]
        m_s[...] = m0_ref[...]

    rows4 = N_HEADS * CHUNK
    t4 = lax.broadcasted_iota(jnp.int32, (rows4, CHUNK), 0) & (CHUNK - 1)
    causal4 = t4 >= lax.broadcasted_iota(jnp.int32, (rows4, CHUNK), 1)
    valid4 = t4[:, 0:1] < n_valid
    tril = (lax.broadcasted_iota(jnp.int32, (CHUNK, CHUNK), 0)
            >= lax.broadcasted_iota(jnp.int32, (CHUNK, CHUNK), 1)).astype(BF16)
    lane_head = lax.broadcasted_iota(jnp.int32, (CHUNK, MIX_W), 1) // HEAD_DIM
    same_head = (lax.broadcasted_iota(jnp.int32, (MIX_W, MIX_W), 0) // HEAD_DIM
                 == lax.broadcasted_iota(jnp.int32, (MIX_W, MIX_W), 1) // HEAD_DIM)
    head_ones = same_head.astype(BF16)
    ones_l = jnp.ones((CHUNK, MIX_W), BF16)
    last = n_valid - 1

    def stack(f):
        return jnp.concatenate([f(h) for h in range(N_HEADS)], axis=0)

    def on_head_lanes(f, lanes):
        out = f(N_HEADS - 1)
        for h in range(N_HEADS - 2, -1, -1):
            out = jnp.where(lanes == h, f(h), out)
        return out

    for first_seq in range(0, nseq, SEQS_PER_CHAIN):
        group = list(range(first_seq, min(first_seq + SEQS_PER_CHAIN, nseq)))
        pre = []
        for i in group:
            gates = _chunk_tile(zg_ref, i, pg)
            ipre = gates + bias_ref[0:1, :]
            logf = _log_sigmoid(gates + bias_ref[1:2, :])
            bcum = _dot_01(tril, logf)
            bcum_t = bcum.T
            ipre_t = ipre.T
            q = _chunk_tile(zq_ref, i, pq)
            k = _chunk_tile(zk_ref, i, pk) * ATTN_SCALE
            v = _chunk_tile(zv_ref, i, pv)
            cbd = c_s[i]
            nrow = n_s[i]
            mprev = m_s[i]
            pre.append(dict(
                k=k, v=v, kb=k.astype(BF16), vb=v.astype(BF16), cbd=cbd, nrow=nrow, mprev=mprev,
                bcol=stack(lambda h: bcum[:, N_HEADS + h:N_HEADS + h + 1]),
                icol=stack(lambda h: ipre[:, h:h + 1]),
                brow=stack(lambda h: jnp.broadcast_to(bcum_t[N_HEADS + h:N_HEADS + h + 1, :], (CHUNK, CHUNK))),
                irow=stack(lambda h: jnp.broadcast_to(ipre_t[h:h + 1, :], (CHUNK, CHUNK))),
                mprev4=stack(lambda h: jnp.broadcast_to(mprev[:, h:h + 1], (CHUNK, 1))),
                scores=_head_scores(q, k),
                q_c=_dot_nt(q.astype(BF16), cbd.astype(BF16)),
                q_n=_dot((q * nrow).astype(BF16), head_ones)))

        cat = lambda key: jnp.concatenate([p[key] for p in pre], axis=0)
        bcol_g = cat("bcol")
        dmat = jnp.where(jnp.concatenate([causal4] * len(pre), axis=0), bcol_g - cat("brow") + cat("irow"), NEG)
        g = bcol_g + cat("mprev4")
        mt_g = jnp.maximum(g, jnp.max(dmat, axis=1, keepdims=True))
        sb_g = (cat("scores") * jnp.exp(dmat - mt_g)).astype(BF16)
        gexp_g = jnp.exp(g - mt_g)
        floor_g = jnp.exp(-mt_g)

        seq_rows = [slice(j * rows4, (j + 1) * rows4) for j in range(len(pre))]
        num_g = (jnp.concatenate([_dot(sb_g[r], p["vb"]) for r, p in zip(seq_rows, pre)], axis=0)
                 + gexp_g * jnp.concatenate([p["q_c"] for p in pre for _ in range(N_HEADS)], axis=0))
        nq_g = _dot(sb_g, ones_l) + gexp_g * jnp.concatenate([p["q_n"] for p in pre for _ in range(N_HEADS)], axis=0)
        hh4_g = num_g / jnp.maximum(jnp.abs(nq_g), floor_g)

        for j, (i, p) in enumerate(zip(group, pre)):
            rows = seq_rows[j]
            hh4, mt, bcol, icol, mprev = hh4_g[rows], mt_g[rows], p["bcol"], p["icol"], p["mprev"]
            hh = on_head_lanes(lambda h: hh4[h * CHUNK:(h + 1) * CHUNK], lane_head)
            sq = hh * hh
            sq_hi = sq.astype(BF16)
            sq_lo = (sq - sq_hi.astype(F32)).astype(BF16)
            ms = (_dot(sq_hi, head_ones) + _dot(sq_lo, head_ones)) * (1.0 / HEAD_DIM)
            out = jax.nn.sigmoid(_chunk_tile(zo_ref, i, po)) * (hh * lax.rsqrt(ms + EPS) * gn_ref[...])
            out_ref[i] = out[0:out_ref.shape[1]]

            mnew = [mt[h * CHUNK + last:h * CHUNK + last + 1, :] for h in range(N_HEADS)]
            blast = [bcol[h * CHUNK + last:h * CHUNK + last + 1, :] for h in range(N_HEADS)]
            wk = jnp.exp(stack(lambda h: jnp.broadcast_to(blast[h] - mnew[h], (CHUNK, 1))) - bcol + icol)
            if n_valid < CHUNK:
                wk = jnp.where(valid4, wk, 0.0)
            wk_w = on_head_lanes(lambda h: jnp.broadcast_to(wk[h * CHUNK:(h + 1) * CHUNK], (CHUNK, MIX_W)), lane_head)
            dc = on_head_lanes(
                lambda h: jnp.broadcast_to(jnp.exp(blast[h] + mprev[:, h:h + 1] - mnew[h]), (1, MIX_W)), lane_head[0:1])
            c_s[i] = dc * p["cbd"] + jnp.where(same_head, _dot_tn((p["v"] * wk_w).astype(BF16), p["kb"]), 0.0)
            n_s[i] = dc * p["nrow"] + jnp.sum(wk_w * p["k"], axis=0, keepdims=True)
            for h in range(N_HEADS):
                m_s[i, :, h:h + 1] = mnew[h]

    @pl.when(c == pl.num_programs(1) - 1)
    def _():
        for i in range(nseq):
            for h in range(N_HEADS):
                c_out[i, h] = c_s[i, _head(h), _head(h)]
        n_out[...] = n_s[...]
        m_out[...] = m_s[...]


def _mlstm(zb, zg, bias, gn, layer, c0, n0, m0, n_valid, nseq):
    B, T, _ = zb.shape
    rows = min(T, CHUNK)
    nc = T // rows
    blk = (nseq, rows, MIX_W)
    zspec = lambda j: pl.BlockSpec(blk, lambda b, c: (b, c, j))
    pads = [pltpu.VMEM((CHUNK, wd), F32) for wd in (MIX_W,) * 4 + (GATE_W,)] if rows < CHUNK else []
    st4 = pl.BlockSpec((nseq, N_HEADS, HEAD_DIM, HEAD_DIM), lambda b, c: (b, 0, 0, 0))
    st3 = pl.BlockSpec((nseq, 1, MIX_W), lambda b, c: (b, 0, 0))
    st2 = pl.BlockSpec((nseq, 1, N_HEADS), lambda b, c: (b, 0, 0))
    init = lambda spec: pl.BlockSpec((None,) + spec.block_shape,
                                     lambda b, c: (layer, b) + (0,) * (len(spec.block_shape) - 1))
    return pl.pallas_call(
        functools.partial(_mlstm_body, n_valid=n_valid),
        grid=(B // nseq, nc),
        in_specs=[zspec(0), zspec(1), zspec(2), zspec(3),
                  pl.BlockSpec((nseq, rows, GATE_W), lambda b, c: (b, c, 0)),
                  pl.BlockSpec((2, GATE_W), lambda b, c: (0, 0)),
                  pl.BlockSpec((1, MIX_W), lambda b, c: (0, 0)),
                  init(st4), init(st3), init(st2)],
        out_specs=[pl.BlockSpec(blk, lambda b, c: (b, c, 0)), st4, st3, st2],
        out_shape=[jax.ShapeDtypeStruct((B, T, MIX_W), F32),
                   jax.ShapeDtypeStruct((B, N_HEADS, HEAD_DIM, HEAD_DIM), F32),
                   jax.ShapeDtypeStruct((B, 1, MIX_W), F32),
                   jax.ShapeDtypeStruct((B, 1, N_HEADS), F32)],
        scratch_shapes=[pltpu.VMEM((nseq, MIX_W, MIX_W), F32),
                        pltpu.VMEM((nseq, 1, MIX_W), F32),
                        pltpu.VMEM((nseq, 1, N_HEADS), F32)] + pads,
        compiler_params=_params("parallel", "arbitrary"),
        name="mlstm",
    )(zb, zb, zb, zb, zg, bias, gn, c0, n0, m0)


def _gate_body(zc_ref, w_ref, bs_ref, gcv_ref, *rest, want_vrows):
    out_ref = rest[0]
    vrow_ref = rest[1] if want_vrows else None
    pad = rest[-1] if zc_ref.shape[1] < CHUNK else None
    if pad is not None:
        _zero_pads([pad])
    row = lax.broadcasted_iota(jnp.int32, (CHUNK, CHUNK), 0)
    col = lax.broadcasted_iota(jnp.int32, (CHUNK, CHUNK), 1)
    lane_head = lax.broadcasted_iota(jnp.int32, (CHUNK, MIX_W), 1) // HEAD_DIM
    ws = [jnp.where(row >= col, w_ref[h], 0.0).astype(BF16) for h in range(N_HEADS)]
    n_rows = zc_ref.shape[1]
    for i in range(zc_ref.shape[0]):
        z = zc_ref[i] if pad is None else _chunk_tile(zc_ref, i, pad)
        u = jax.nn.gelu(z[:, :MIX_W])
        vn = _rms(jax.nn.gelu(z[:, MIX_W:]), gcv_ref[...])
        if want_vrows:
            vrow_ref[i] = vn[0:n_rows]
        vb = vn.astype(BF16)
        mixed = []
        for j in range(z.shape[0] // CHUNK):
            vj = vb[j * CHUNK:(j + 1) * CHUNK]
            s = _dot(ws[N_HEADS - 1], vj) + bs_ref[:, N_HEADS - 1:N_HEADS]
            for h in range(N_HEADS - 2, -1, -1):
                s = jnp.where(lane_head == h, _dot(ws[h], vj) + bs_ref[:, h:h + 1], s)
            mixed.append(s)
        out_ref[i] = (u * jnp.concatenate(mixed, axis=0))[0:n_rows]


def _gate(zc, w_s, bs_t, gcv, nseq, rows, want_vrows):
    B, T, _ = zc.shape
    blk = (nseq, rows, MIX_W)
    n_out = 2 if want_vrows else 1
    return pl.pallas_call(
        functools.partial(_gate_body, want_vrows=want_vrows),
        grid=(B // nseq, T // rows),
        in_specs=[pl.BlockSpec((nseq, rows, 2 * MIX_W), lambda b, c: (b, c, 0)),
                  pl.BlockSpec((N_HEADS, CHUNK, CHUNK), lambda b, c: (0, 0, 0)),
                  pl.BlockSpec((CHUNK, N_HEADS), lambda b, c: (0, 0)),
                  pl.BlockSpec((1, MIX_W), lambda b, c: (0, 0))],
        out_specs=[pl.BlockSpec(blk, lambda b, c: (b, c, 0))] * n_out,
        out_shape=[jax.ShapeDtypeStruct((B, T, MIX_W), F32)] * n_out,
        scratch_shapes=[pltpu.VMEM((CHUNK, 2 * MIX_W), F32)] if rows < CHUNK else [],
        compiler_params=_params("parallel", "parallel"),
        name="gate",
    )(zc, w_s, bs_t, gcv)


def _hgrn_levels(n_valid):
    return [l for l in range(N_LEVELS) if (CHUNK >> (l + 1)) < max(n_valid, 2)]


def _hgrn_tables(n_valid):
    p = np.arange(CHUNK)[:, None]
    u = np.arange(CHUNK)[None, :]
    mats = []
    for l in _hgrn_levels(n_valid):
        m = CHUNK >> (l + 1)
        start = (p // m) * m
        odd = ((p // m) % 2) == 1
        mats.append(np.where(odd, (u >= start) & (u <= p), (u > p) & (u <= start + m - 1)))
    mats.append(u <= p)
    mats.append((u > p) & (u <= n_valid - 1))
    mall = np.concatenate(mats, axis=0).astype(np.float32)
    t = np.arange(CHUNK)[:, None]
    s = np.arange(CHUNK)[None, :]
    x = t ^ s
    top = np.floor(np.log2(np.maximum(x, 1))).astype(np.int32)
    lvl = np.where(s < t, N_LEVELS - 1 - top, np.where(s == t, N_LEVELS, N_LEVELS + 1)).astype(np.int32)
    return jnp.asarray(mall, BF16), jnp.asarray(np.tile(lvl, (N_HEADS, 1)))


def _hgrn_body(zq_ref, zf_ref, zi_ref, zg_ref, lb_ref, gn_ref, mall_ref, lvl_ref, s0_ref,
               out_ref, s_out, s_s, *pads, n_valid):
    c = pl.program_id(1)
    nseq = zq_ref.shape[0]
    pq, pf, pi, pg = pads if pads else (None,) * 4
    _zero_pads(pads)

    @pl.when(c == 0)
    def _():
        s_s[...] = jnp.zeros_like(s_s)
        for i in range(nseq):
            for h in range(N_HEADS):
                s_s[i, _head(h), _head(h)] = s0_ref[i, h]

    lb = lb_ref[...]
    lb_floor = jnp.maximum(lb, LB_FLOOR)
    lvl = lvl_ref[...]
    valid_rows = lax.broadcasted_iota(jnp.int32, (CHUNK, 1), 0) < n_valid
    lane_head = lax.broadcasted_iota(jnp.int32, (CHUNK, MIX_W), 1) // HEAD_DIM
    same_head = (lax.broadcasted_iota(jnp.int32, (MIX_W, MIX_W), 0) // HEAD_DIM
                 == lax.broadcasted_iota(jnp.int32, (MIX_W, MIX_W), 1) // HEAD_DIM)
    head_ones = same_head.astype(BF16)
    levels = _hgrn_levels(n_valid)
    at_level = [lvl == l for l in levels]
    on_diagonal = lvl == N_LEVELS

    group_of = lambda x: jnp.concatenate([x] * min(SEQS_PER_CHAIN, nseq), axis=0)
    at_level = [group_of(m) for m in at_level]
    on_diagonal = group_of(on_diagonal)
    amats = {}
    for first_seq in range(0, nseq, SEQS_PER_CHAIN):
        group = list(range(first_seq, min(first_seq + SEQS_PER_CHAIN, nseq)))
        pre = []
        for i in group:
            q = _chunk_tile(zq_ref, i, pq)
            fx = _chunk_tile(zf_ref, i, pf)
            sig = jax.nn.sigmoid(fx)
            logf = jnp.log(lb_floor + (1.0 - lb) * sig)
            kd = (1.0 - lb) * (1.0 - sig)
            pre.append((q, kd, _dot_01(mall_ref[...], logf, pieces=2)))
        amat = None
        for j, l in enumerate(levels):
            scores = []
            for q, kd, gsum in pre:
                e = jnp.exp(gsum[j * CHUNK:(j + 1) * CHUNK])
                scores.append(_head_scores(q * e, kd * e))
            amat = jnp.where(at_level[j], jnp.concatenate(scores, axis=0), 0.0 if j == 0 else amat)
        amat = jnp.where(on_diagonal, jnp.concatenate([_head_scores(q, kd) for q, kd, _ in pre], axis=0), amat)
        for j, (i, p) in enumerate(zip(group, pre)):
            amats[i] = (amat[j * N_HEADS * CHUNK:(j + 1) * N_HEADS * CHUNK],) + p

    outs, gates = [], []
    for i in range(nseq):
        amat, q, kd, gsum = amats[i]
        vb = _chunk_tile(zi_ref, i, pi).astype(BF16)
        bcum = gsum[len(levels) * CHUNK:(len(levels) + 1) * CHUNK]
        k_out = jnp.where(valid_rows, kd * jnp.exp(gsum[(len(levels) + 1) * CHUNK:]), 0.0).astype(BF16)
        e_last = jnp.exp(bcum[n_valid - 1:n_valid, :])
        sbd = s_s[i]
        o4 = _dot(amat.astype(BF16), vb)
        o = o4[(N_HEADS - 1) * CHUNK:]
        for h in range(N_HEADS - 2, -1, -1):
            o = jnp.where(lane_head == h, o4[h * CHUNK:(h + 1) * CHUNK], o)
        outs.append(o + _dot_nt((q * jnp.exp(bcum)).astype(BF16), sbd.astype(BF16)))
        gates.append(_chunk_tile(zg_ref, i, pg))
        s_s[i] = e_last * sbd + jnp.where(same_head, _dot_tn(vb, k_out), 0.0)

    o = jnp.concatenate(outs, axis=0)
    gate = jnp.concatenate(gates, axis=0)
    sq = o * o
    sq_hi = sq.astype(BF16)
    sq_lo = (sq - sq_hi.astype(F32)).astype(BF16)
    ms = (_dot(sq_hi, head_ones) + _dot(sq_lo, head_ones)) * (1.0 / HEAD_DIM)
    out = o * lax.rsqrt(ms + EPS) * gn_ref[...] * (gate * jax.nn.sigmoid(gate))
    for i in range(nseq):
        out_ref[i] = out[i * CHUNK:i * CHUNK + out_ref.shape[1]]

    @pl.when(c == pl.num_programs(1) - 1)
    def _():
        for i in range(nseq):
            for h in range(N_HEADS):
                s_out[i, h] = s_s[i, _head(h), _head(h)]


def _hgrn(zd, lb, gn, layer, s0_t, n_valid, nseq):
    B, T, _ = zd.shape
    mall, lvl = _hgrn_tables(n_valid)
    rows = min(T, CHUNK)
    blk = (nseq, rows, MIX_W)
    zspec = lambda j: pl.BlockSpec(blk, lambda b, c: (b, c, j))
    pads = [pltpu.VMEM((CHUNK, MIX_W), F32)] * 4 if rows < CHUNK else []
    st4 = pl.BlockSpec((nseq, N_HEADS, HEAD_DIM, HEAD_DIM), lambda b, c: (b, 0, 0, 0))
    vec = pl.BlockSpec((1, MIX_W), lambda b, c: (0, 0))
    return pl.pallas_call(
        functools.partial(_hgrn_body, n_valid=n_valid),
        grid=(B // nseq, T // rows),
        in_specs=[zspec(0), zspec(1), zspec(2), zspec(3), vec, vec,
                  pl.BlockSpec(mall.shape, lambda b, c: (0, 0)),
                  pl.BlockSpec(lvl.shape, lambda b, c: (0, 0)),
                  pl.BlockSpec((None,) + st4.block_shape, lambda b, c: (layer, b, 0, 0, 0))],
        out_specs=[pl.BlockSpec(blk, lambda b, c: (b, c, 0)), st4],
        out_shape=[jax.ShapeDtypeStruct((B, T, MIX_W), F32),
                   jax.ShapeDtypeStruct((B, N_HEADS, HEAD_DIM, HEAD_DIM), F32)],
        scratch_shapes=[pltpu.VMEM((nseq, MIX_W, MIX_W), F32)] + pads,
        compiler_params=_params("parallel", "arbitrary"),
        name="hgrn",
    )(zd, zd, zd, zd, lb, gn, mall, lvl, s0_t)


def _post_body(x_ref, oa_ref, ob_ref, oc_ref, od_ref, wout_hbm, gm_ref, wup_hbm, wdn_hbm, gf_ref, y_ref,
               wout_ref, wup_ref, wdn_ref, h_s, acc_s, sem, *, layer, tf, final):
    first = pl.program_id(0) == 0
    chunks = range(D_FF // tf)

    def up_copy(j):
        cols = pl.ds(j * tf, tf)
        return pltpu.make_async_copy(wup_hbm.at[layer, :, cols], wup_ref.at[:, cols], sem.at[0, j])

    def down_copy(j):
        rows = pl.ds(j * tf, tf)
        return pltpu.make_async_copy(wdn_hbm.at[layer, rows, :], wdn_ref.at[rows, :], sem.at[1, j])

    @pl.when(first)
    def _():
        for j in chunks:
            up_copy(j).start()
            down_copy(j).start()
        pltpu.sync_copy(wout_hbm.at[layer], wout_ref)

    mix = jnp.concatenate([o_ref[...].astype(BF16) for o_ref in (oa_ref, ob_ref, oc_ref, od_ref)], axis=1)
    x1 = x_ref[...] + _dot(mix, wout_ref[...])
    acc_s[...] = x1
    h_s[...] = _rms(x1, gm_ref[...]).astype(BF16)
    for j in chunks:
        pl.when(first)(up_copy(j).wait)
        up = jnp.maximum(_dot(h_s[...], wup_ref[:, j * tf:(j + 1) * tf]), 0.0)
        pl.when(first)(down_copy(j).wait)
        acc_s[...] += _dot((up * up).astype(BF16), wdn_ref[j * tf:(j + 1) * tf, :])
    y_ref[...] = _rms(acc_s[...], gf_ref[...]) if final else acc_s[...]


def _post(layer, x2d, oa, ob, oc, od, wout, gm, wup, wdn, gf, tm, tf, final):
    n = x2d.shape[0]
    row = lambda wd: pl.BlockSpec((tm, wd), lambda i: (i, 0))
    vec = pl.BlockSpec((1, D_MODEL), lambda i: (0, 0))
    hbm = pl.BlockSpec(memory_space=pl.ANY)
    return pl.pallas_call(
        functools.partial(_post_body, layer=layer, tf=tf, final=final),
        grid=(n // tm,),
        in_specs=[row(D_MODEL)] + [row(MIX_W)] * 4 + [hbm, vec, hbm, hbm, vec],
        out_specs=row(D_MODEL),
        out_shape=jax.ShapeDtypeStruct((n, D_MODEL), F32),
        scratch_shapes=[pltpu.VMEM(wout.shape[1:], BF16), pltpu.VMEM(wup.shape[1:], BF16), pltpu.VMEM(wdn.shape[1:], BF16),
                        pltpu.VMEM((tm, D_MODEL), BF16), pltpu.VMEM((tm, D_MODEL), F32),
                        pltpu.SemaphoreType.DMA((2, D_FF // tf))],
        compiler_params=_params("arbitrary"),
        name="post",
    )(x2d, oa, ob, oc, od, wout, gm, wup, wdn, gf)


def _rel_bucket(dist):
    max_exact = N_BUCKETS // 2
    d = jnp.maximum(dist, 1).astype(F32)
    large = max_exact + (jnp.log(d / max_exact) / math.log(MAX_WINDOW / max_exact)
                         * (N_BUCKETS - max_exact)).astype(jnp.int32)
    large = jnp.clip(large, max_exact, N_BUCKETS - 1)
    return jnp.where(dist < max_exact, dist, large)


def _pattern_bias(rel_bias, w, d):
    offs = jnp.arange(w // d + 1, dtype=jnp.int32) * d
    return rel_bias[_rel_bucket(offs)].T.astype(F32)


def _prompt_table(bias):
    cols = 2 * CHUNK
    u = jnp.concatenate([bias[:, ::-1], jnp.full((N_HEADS, cols - CHUNK), NEG, F32)], axis=1)
    return jnp.tile(u, (1, CHUNK))[:, :CHUNK * cols].reshape(N_HEADS, CHUNK, cols)


def _sample_tables(bias, d, n_new, past):
    comb = jnp.concatenate([bias[:, :, None], jnp.full((N_HEADS, CHUNK + 1, d - 1), NEG, F32)], axis=2)
    comb = comb.reshape(N_HEADS, (CHUNK + 1) * d)
    length = past + SAMPLE_ROWS + 1
    comb = comb[:, :length]
    comb = jnp.pad(comb, ((0, 0), (0, length - comb.shape[1])), constant_values=NEG)
    rev = comb[:, ::-1]
    tpast, tnew = [], []
    for row in range(SAMPLE_ROWS):
        t = min(row, n_new - 1)
        start = length - 1 - past - t
        tpast.append(rev[:, start:start + past])
        start = length - 1 - t
        tnew.append(jnp.pad(rev[:, start:start + t + 1], ((0, 0), (0, CHUNK - t - 1)), constant_values=NEG))
    return jnp.stack(tpast, axis=1), jnp.stack(tnew, axis=1)


def _pad_rows(a, rows):
    return jnp.pad(a, ((0, 0), (0, rows - a.shape[1]), (0, 0)))


def kernel(x_prompt, x_sample, cache_k_win, cache_v_win, state_mlstm_C, state_mlstm_n, state_mlstm_m, state_hgrn_S, rel_bias, w_in, w_out, g_attn, g_mlp, w_up, w_down, b_i, b_f, g_mlstm, g_cv, w_s, b_s, hgrn_lb, g_hgrn, g_final):
    depth = w_in.shape[0]
    B, T, _ = x_prompt.shape
    Bs, Ts, _ = x_sample.shape
    past = cache_k_win.shape[2]
    assert past == MAX_WINDOW and T % (MAX_WINDOW) == 0 and Ts <= SAMPLE_ROWS
    H, dh = N_HEADS, HEAD_DIM
    keep_p = min(MAX_WINDOW, T)

    sm = jax.nn.softmax(hgrn_lb.astype(F32), axis=0)
    lb_all = jnp.cumsum(sm, axis=0) - sm[0:1]
    biases = [_pattern_bias(rel_bias, w, d) for w, d in DILATIONS]
    tabs_p = jnp.stack([_prompt_table(bb) for bb in biases])
    tabs_s = [_sample_tables(bb, d, Ts, past) for bb, (_, d) in zip(biases, DILATIONS)]
    tpast = jnp.stack([a for a, _ in tabs_s])
    tnew = jnp.stack([b for _, b in tabs_s])
    kt_all = jnp.transpose(cache_k_win, (0, 1, 3, 4, 2))
    vt_all = jnp.transpose(cache_v_win, (0, 1, 3, 4, 2))

    a_end = Z_WIDTHS[0] + Z_WIDTHS[1]
    g_end = a_end + N_GATE_COLS
    zeros_c = jnp.zeros((1, B, H, dh, dh), F32)
    zeros_n = jnp.zeros((1, B, 1, MIX_W), F32)
    zeros_m = jnp.zeros((1, B, 1, H), F32)
    n0_all = state_mlstm_n.reshape(depth, Bs, 1, MIX_W)
    m0_all = state_mlstm_m.reshape(depth, Bs, 1, H)
    s0_all = jnp.swapaxes(state_hgrn_S, -1, -2)
    vec = lambda a: a.reshape(1, -1).astype(F32)
    pseq = math.gcd(B, PROMPT_SEQS_PER_STEP)
    sseq = math.gcd(Bs, SAMPLE_SEQS_PER_STEP)

    wout_b, wup_b, wdn_b = w_out.astype(BF16), w_up.astype(BF16), w_down.astype(BF16)
    xp = x_prompt.reshape(B * T, D_MODEL)
    xs = x_sample.reshape(Bs * Ts, D_MODEL)
    outs = [[] for _ in range(13)]
    new_rows = []
    for l in range(depth):
        wl = w_in[l]
        w_gate = jnp.pad(wl[:, a_end:g_end], ((0, 0), (0, GATE_W - N_GATE_COLS)))
        w_gate_lo = w_gate - w_gate.astype(BF16).astype(F32)
        w_z = jnp.concatenate([wl[:, :a_end], w_gate, w_gate_lo, wl[:, g_end:]], axis=1).astype(BF16)
        gate_bias = jnp.zeros((2, GATE_W), F32).at[0, :H].set(b_i[l]).at[1, H:2 * H].set(b_f[l])
        bs_t = b_s[l].T.astype(F32)
        final = l == depth - 1
        gf = vec(g_final)

        za, zb, zg, zc, zd = _inproj(xp, vec(g_attn[l]), w_z, 512)
        za3 = za.reshape(B, T, -1)
        oa = _attn_prompt(za3, tabs_p)
        ob, c1, n1, m1 = _mlstm(zb.reshape(B, T, -1), zg.reshape(B, T, -1), gate_bias, vec(g_mlstm[l]),
                                0, zeros_c, zeros_n, zeros_m, CHUNK, pseq)
        (oc,) = _gate(zc.reshape(B, T, -1), w_s[l], bs_t, vec(g_cv[l]), 1, GATE_ROWS, False)
        od, s1 = _hgrn(zd.reshape(B, T, -1), vec(lb_all[l]), vec(g_hgrn[l]), 0, zeros_c, CHUNK, pseq)
        xp = _post(l, xp, oa.reshape(B * T, -1), ob.reshape(B * T, -1), oc.reshape(B * T, -1), od.reshape(B * T, -1),
                   wout_b, vec(g_mlp[l]), wup_b, wdn_b, gf, 512, 1024, final)
        outs[0].append(za3[:, T - keep_p:, MIX_W:2 * MIX_W].reshape(B, keep_p, H, dh))
        outs[1].append(za3[:, T - keep_p:, 2 * MIX_W:].reshape(B, keep_p, H, dh))
        outs[4].append(c1)
        outs[5].append(n1.reshape(B, H, dh))
        outs[6].append(m1.reshape(B, H))
        outs[10].append(s1)

        za, zb, zg, zc, zd = _inproj(xs, vec(g_attn[l]), w_z, Bs * Ts)
        za3 = za.reshape(Bs, Ts, -1)
        new_t = lambda a: jnp.pad(jnp.transpose(a.reshape(Bs, Ts, H, dh), (0, 2, 3, 1)),
                                  ((0, 0), (0, 0), (0, 0), (0, LANES - Ts)))
        new_rows.append((new_t(za3[:, :, MIX_W:2 * MIX_W]), new_t(za3[:, :, 2 * MIX_W:])))
        oa, *windows = _attn_sample(l, _pad_rows(za3[:, :, :MIX_W], SAMPLE_ROWS), kt_all, vt_all,
                                    new_rows, tpast, tnew, Ts)
        seqs = lambda z: z.reshape(Bs, Ts, -1)
        ob, c2, n2, m2 = _mlstm(seqs(zb), seqs(zg), gate_bias, vec(g_mlstm[l]),
                                l, state_mlstm_C, n0_all, m0_all, Ts, sseq)
        oc, vrows = _gate(seqs(zc), w_s[l], bs_t, vec(g_cv[l]), sseq, Ts, True)
        od, s2 = _hgrn(seqs(zd), vec(lb_all[l]), vec(g_hgrn[l]), l, s0_all, Ts, sseq)
        flat = lambda o: o.reshape(Bs * Ts, MIX_W)
        xs = _post(l, xs, flat(oa[:, :Ts]), flat(ob), flat(oc), flat(od),
                   wout_b, vec(g_mlp[l]), wup_b, wdn_b, gf, Bs * Ts, 1024, final)
        outs[7].append(c2)
        outs[8].append(n2.reshape(Bs, H, dh))
        outs[9].append(m2.reshape(Bs, H))
        outs[11].append(s2)
        outs[12].append(vrows.reshape(Bs, Ts, H, dh))

    stacked = [jnp.stack(o) if o else None for o in outs]
    stacked[2], stacked[3] = (jnp.transpose(w, (0, 1, 4, 2, 3)) for w in windows)
    stacked[10], stacked[11] = (jnp.swapaxes(s, -1, -2) for s in stacked[10:12])
    return (xp.reshape(B, T, D_MODEL), xs.reshape(Bs, Ts, D_MODEL)) + tuple(stacked)
```
